```python
import math
import jax, jax.numpy as jnp
from jax import lax
import numpy as np

D_MODEL = 1024
BATCH = 8
SEQ = 2048
DEPTH = 2

D_MIX = D_MODEL
HEAD_DIM = 64
N_ATT_HEADS = 8
D_ATT = N_ATT_HEADS * HEAD_DIM
DILATED_PATTERNS = ((128, 1), (512, 4), (2048, 16))
ATT_BLOCK = 128
SSM_GROUP = 16
D_SSM = D_MIX // 4
N_SSM_GROUPS = D_SSM // SSM_GROUP
SSM_STATE = 64
POOL_WINDOWS = (2, 4, 8, 16)
D_POOL = D_MIX - D_ATT - D_SSM
POOL_GROUP = D_POOL // len(POOL_WINDOWS)
D_IN = 3 * D_ATT + D_SSM + D_POOL
IN_SPLITS = (D_ATT, 2 * D_ATT, 3 * D_ATT, 3 * D_ATT + D_SSM)
D_FF = 256 * int(math.ceil(8 * D_MODEL / 3 / 256))
N_BUCKETS = 32
MAX_DISTANCE = 2048
ALPHA = (2 * DEPTH) ** 0.25
BETA = (8 * DEPTH) ** -0.25
FFN_RES = 0.5
LN_EPS = 1e-5
NEG = -1e30

kernel_name = "hybrid_dilated_attn_s5_pool_macaron_deepnorm"


def _layernorm(x):
    xf = x.astype(jnp.float32)
    mu = xf.mean(-1, keepdims=True)
    var = jnp.square(xf - mu).mean(-1, keepdims=True)
    return ((xf - mu) * lax.rsqrt(var + LN_EPS)).astype(x.dtype)


def _layernorm_affine(x, gain, bias):
    return _layernorm(x) * gain + bias


def _modulate(x, shift, scale):
    return _layernorm(x) * (1.0 + scale) + shift


def _swiglu(h, w_gate, w_up, w_down):
    return (jax.nn.silu(h @ w_gate) * (h @ w_up)) @ w_down


def _t5_bucket(dist):
    max_exact = N_BUCKETS // 2
    d = np.maximum(dist, 1).astype(np.float32)
    large = max_exact + (np.log(d / max_exact) / math.log(MAX_DISTANCE / max_exact)
                         * (N_BUCKETS - max_exact)).astype(np.int32)
    large = np.minimum(large, N_BUCKETS - 1)
    return np.where(dist < max_exact, dist, large).astype(np.int32)


def _dilated_branch(q, k, v, rel_bias, window, dilation):
    B, S, H, E = q.shape
    Q = ATT_BLOCK
    n_keys = window // dilation
    L = S // dilation
    nb = -(-L // Q)
    Lp = nb * Q

    def by_residue(t):
        t = t.reshape(B, L, dilation, H, E).transpose(0, 2, 1, 3, 4)
        return jnp.pad(t, ((0, 0), (0, 0), (0, Lp - L), (0, 0), (0, 0)))

    def band(t):
        t = jnp.pad(t, ((0, 0), (0, 0), (Q, 0), (0, 0), (0, 0))).reshape(B, dilation, nb + 1, Q, H, E)
        return jnp.concatenate([t[:, :, :-1], t[:, :, 1:]], axis=3)

    qb = by_residue(q).reshape(B, dilation, nb, Q, H, E)
    kb = band(by_residue(k))
    vb = band(by_residue(v))

    i = np.arange(Q)[:, None]
    j = np.arange(2 * Q)[None, :]
    r = i + Q - j
    in_band = (r >= 0) & (r <= n_keys)
    k_abs = np.arange(nb)[:, None, None] * Q + j[None] - Q
    valid = (in_band[None] & (k_abs >= 0))[:, None]
    bucket = _t5_bucket(np.clip(r, 0, None) * dilation)
    bias = jnp.transpose(rel_bias[bucket], (2, 0, 1)).astype(jnp.float32)

    s = jnp.einsum('brnqhe,brnkhe->brnhqk', qb, kb, preferred_element_type=jnp.float32)
    s = jnp.where(valid, s + bias, NEG)
    m = s.max(-1, keepdims=True)
    p = jnp.exp(s - m)
    den = p.sum(-1, keepdims=True)
    o = jnp.einsum('brnhqk,brnkhe->brnqhe', p, vb.astype(jnp.float32))
    o = o / jnp.swapaxes(den, 3, 4)
    lse = jnp.swapaxes((m + jnp.log(den))[..., 0], 3, 4)

    o = o.reshape(B, dilation, Lp, H, E)[:, :, :L].transpose(0, 2, 1, 3, 4).reshape(B, S, H, E)
    lse = lse.reshape(B, dilation, Lp, H)[:, :, :L].transpose(0, 2, 1, 3).reshape(B, S, H)
    return o, lse


def _dilated_attention(q, k, v, rel_bias):
    outs, lses = [], []
    for window, dilation in DILATED_PATTERNS:
        o, lse = _dilated_branch(q, k, v, rel_bias, window, dilation)
        outs.append(o)
        lses.append(lse)
    w = jax.nn.softmax(jnp.stack(lses, 0), axis=0)
    return jnp.einsum('pbsh,pbshe->bshe', w, jnp.stack(outs, 0))


def _s5(u, a_re, a_im, log_dt, b_re, b_im, c_re, c_im, d_skip, glu_w, glu_b):
    Bsz, S, _ = u.shape
    f32 = jnp.float32
    lam = lax.complex(a_re.astype(f32), a_im.astype(f32))
    dt = jnp.exp(log_dt.astype(f32))[:, None]
    a_bar = jnp.exp(lam * dt)
    b_bar = ((a_bar - 1.0) / lam)[:, :, None] * lax.complex(b_re.astype(f32), b_im.astype(f32))
    uf = u.astype(f32)
    bu = jnp.einsum('bsgc,gpc->bsgp', uf.reshape(Bsz, S, N_SSM_GROUPS, SSM_GROUP), b_bar)
    a_full = jnp.broadcast_to(a_bar, bu.shape)

    def combine(e1, e2):
        a1, b1 = e1
        a2, b2 = e2
        return a2 * a1, a2 * b1 + b2

    _, states = lax.associative_scan(combine, (a_full, bu), axis=1)
    cm = lax.complex(c_re.astype(f32), c_im.astype(f32))
    y = jnp.einsum('gcp,bsgp->bsgc', cm, states).real.reshape(Bsz, S, D_SSM)
    y = y + d_skip.astype(f32) * uf
    return y * jax.nn.sigmoid(jax.nn.gelu(y) @ glu_w.astype(f32) + glu_b.astype(f32))


def _pool_mixer(u, pool_w, pool_scale):
    Bsz, S, _ = u.shape
    ug = u.astype(jnp.float32).reshape(Bsz, S, len(POOL_WINDOWS), POOL_GROUP)
    cs = jnp.cumsum(ug, axis=1)
    count = jnp.arange(1, S + 1, dtype=jnp.float32)
    means = []
    for g, w in enumerate(POOL_WINDOWS):
        c_g = cs[:, :, g]
        lagged = jnp.pad(c_g[:, :-w], ((0, 0), (w, 0), (0, 0)))
        means.append((c_g - lagged) / jnp.minimum(count, float(w))[None, :, None])
    pooled = jnp.stack(means, axis=2) - ug
    y = jnp.einsum('bsgc,gcd->bsgd', pooled, pool_w.astype(jnp.float32)).reshape(Bsz, S, D_POOL)
    return y * pool_scale.astype(jnp.float32)


def _hybrid_mixer(h, rel_bias, w_in, w_out, a_re, a_im, log_dt, b_re, b_im, c_re, c_im,
                  d_skip, glu_w, glu_b, pool_w, pool_scale):
    Bsz, S, _ = h.shape
    z = h @ w_in
    q, k, v, u_ssm, u_pool = jnp.split(z, IN_SPLITS, axis=-1)
    heads = lambda t: t.reshape(Bsz, S, N_ATT_HEADS, HEAD_DIM)
    y_att = _dilated_attention(heads(q) * HEAD_DIM ** -0.5, heads(k), heads(v), rel_bias)
    y_att = y_att.reshape(Bsz, S, D_ATT)
    y_ssm = _s5(u_ssm, a_re, a_im, log_dt, b_re, b_im, c_re, c_im, d_skip, glu_w, glu_b)
    y_pool = _pool_mixer(u_pool, pool_w, pool_scale)
    y = jnp.concatenate([y_att.astype(h.dtype), y_ssm.astype(h.dtype), y_pool.astype(h.dtype)], axis=-1)
    return y @ w_out


def setup_inputs(seed: int = 0) -> dict:
    key = jax.random.key(seed)
    ks = jax.random.split(key, 26)
    f32 = jnp.float32
    nrm = lambda i, shape, std: std * jax.random.normal(ks[i], shape, f32)
    L, G, P = DEPTH, N_SSM_GROUPS, SSM_STATE
    n = jnp.arange(P, dtype=f32)
    return {
        "x": nrm(0, (BATCH, SEQ, D_MODEL), 1.0),
        "c": nrm(1, (BATCH, D_MODEL), 1.0),
        "rel_bias": nrm(2, (N_BUCKETS, N_ATT_HEADS), 0.1),
        "ada_w": nrm(3, (L, D_MODEL, 9 * D_MODEL), D_MODEL ** -0.5),
        "ada_b": nrm(4, (L, 9 * D_MODEL), 0.02),
        "ln_g": 1.0 + nrm(5, (L, 3, D_MODEL), 0.02),
        "ln_b": nrm(6, (L, 3, D_MODEL), 0.02),
        "ffn_w_gate": nrm(7, (L, 2, D_MODEL, D_FF), D_MODEL ** -0.5),
        "ffn_w_up": nrm(8, (L, 2, D_MODEL, D_FF), D_MODEL ** -0.5),
        "ffn_w_down": nrm(9, (L, 2, D_FF, D_MODEL), BETA * D_FF ** -0.5),
        "w_in": nrm(10, (L, D_MODEL, D_IN), D_MODEL ** -0.5),
        "w_out": nrm(11, (L, D_MIX, D_MODEL), BETA * D_MIX ** -0.5),
        "ssm_a_re": -0.5 + nrm(12, (L, G, P), 0.01),
        "ssm_a_im": math.pi * n + nrm(13, (L, G, P), 0.01),
        "ssm_log_dt": jax.random.uniform(ks[14], (L, G), f32, math.log(1e-3), math.log(1e-1)),
        "ssm_b_re": nrm(15, (L, G, P, SSM_GROUP), (2 * SSM_GROUP) ** -0.5),
        "ssm_b_im": nrm(16, (L, G, P, SSM_GROUP), (2 * SSM_GROUP) ** -0.5),
        "ssm_c_re": nrm(17, (L, G, SSM_GROUP, P), (2 * P) ** -0.5),
        "ssm_c_im": nrm(18, (L, G, SSM_GROUP, P), (2 * P) ** -0.5),
        "ssm_d": nrm(19, (L, D_SSM), 1.0),
        "glu_w": nrm(20, (L, D_SSM, D_SSM), D_SSM ** -0.5),
        "glu_b": nrm(21, (L, D_SSM), 0.02),
        "pool_w": nrm(22, (L, len(POOL_WINDOWS), POOL_GROUP, POOL_GROUP), POOL_GROUP ** -0.5),
        "pool_scale": 1.0 + nrm(23, (L, D_POOL), 0.02),
    }


def reference(x, c, rel_bias, ada_w, ada_b, ln_g, ln_b, ffn_w_gate, ffn_w_up, ffn_w_down,
              w_in, w_out, ssm_a_re, ssm_a_im, ssm_log_dt, ssm_b_re, ssm_b_im, ssm_c_re,
              ssm_c_im, ssm_d, glu_w, glu_b, pool_w, pool_scale):
    Bsz = x.shape[0]
    cond = jax.nn.silu(c)
    for l in range(DEPTH):
        mod = (cond @ ada_w[l] + ada_b[l]).reshape(Bsz, 3, 3, 1, D_MODEL)
        h = _modulate(x, mod[:, 0, 0], mod[:, 0, 1])
        f = _swiglu(h, ffn_w_gate[l, 0], ffn_w_up[l, 0], ffn_w_down[l, 0])
        x = _layernorm_affine(ALPHA * x + FFN_RES * mod[:, 0, 2] * f, ln_g[l, 0], ln_b[l, 0])
        h = _modulate(x, mod[:, 1, 0], mod[:, 1, 1])
        y = _hybrid_mixer(h, rel_bias, w_in[l], w_out[l], ssm_a_re[l], ssm_a_im[l], ssm_log_dt[l],
                          ssm_b_re[l], ssm_b_im[l], ssm_c_re[l], ssm_c_im[l], ssm_d[l],
                          glu_w[l], glu_b[l], pool_w[l], pool_scale[l])
        x = _layernorm_affine(ALPHA * x + mod[:, 1, 2] * y, ln_g[l, 1], ln_b[l, 1])
        h = _modulate(x, mod[:, 2, 0], mod[:, 2, 1])
        f = _swiglu(h, ffn_w_gate[l, 1], ffn_w_up[l, 1], ffn_w_down[l, 1])
        x = _layernorm_affine(ALPHA * x + FFN_RES * mod[:, 2, 2] * f, ln_g[l, 2], ln_b[l, 2])
    return x
```

```python
import functools
import math

import jax
import jax.numpy as jnp
import numpy as np
from jax import lax
from jax.experimental import pallas as pl
from jax.experimental.pallas import tpu as pltpu

F32 = jnp.float32
BF16 = jnp.bfloat16

D_MODEL = 1024
DEPTH = 2
HEAD_DIM = 64
N_HEADS = 8
D_ATT = N_HEADS * HEAD_DIM
DILATED_PATTERNS = ((128, 1), (512, 4), (2048, 16))
ATT_BLOCK = 128
SSM_GROUP = 16
D_SSM = 256
N_SSM_GROUPS = D_SSM // SSM_GROUP
SSM_STATE = 64
POOL_WINDOWS = (2, 4, 8, 16)
D_POOL = 256
POOL_GROUP = D_POOL // len(POOL_WINDOWS)
D_IN = 3 * D_ATT + D_SSM + D_POOL
D_FF = 2816
N_BUCKETS = 32
MAX_DISTANCE = 2048
ALPHA = (2 * DEPTH) ** 0.25
FFN_RES = 0.5
LN_EPS = 1e-5
NEG = -1e30

LANES = 128
TOKEN_TILE = 512
FF_CHUNK = 512
ADA_COL_TILE = 1152
SSM_CHUNK = 8
POOL_HALO = max(POOL_WINDOWS)
VMEM_LIMIT = 56 * 1024 * 1024


def _sigmoid(x):
    return 1.0 / (1.0 + jnp.exp(-x))


def _layernorm(x):
    mu = jnp.mean(x, axis=-1, keepdims=True)
    xc = x - mu
    var = jnp.mean(xc * xc, axis=-1, keepdims=True)
    return xc * lax.rsqrt(var + LN_EPS)


def _resident(shape):
    zeros = (0,) * len(shape)
    return pl.BlockSpec(shape, lambda *_: zeros, pipeline_mode=pl.Buffered(1))


def _adaln_kernel(c_ref, w_ref, b_ref, o_ref):
    c = c_ref[...]
    cond = (c * _sigmoid(c)).astype(BF16)
    o_ref[...] = jnp.dot(cond, w_ref[...].astype(BF16), preferred_element_type=F32) + b_ref[...]


def _adaln(c, ada_w, ada_b):
    nl, d, n = ada_w.shape
    bsz = c.shape[0]
    return pl.pallas_call(
        _adaln_kernel,
        out_shape=jax.ShapeDtypeStruct((nl, bsz, n), F32),
        grid=(nl, n // ADA_COL_TILE),
        in_specs=[
            pl.BlockSpec((bsz, d), lambda l, j: (0, 0)),
            pl.BlockSpec((None, d, ADA_COL_TILE), lambda l, j: (l, 0, j)),
            pl.BlockSpec((None, 1, ADA_COL_TILE), lambda l, j: (l, 0, j)),
        ],
        out_specs=pl.BlockSpec((None, bsz, ADA_COL_TILE), lambda l, j: (l, 0, j)),
        compiler_params=pltpu.CompilerParams(vmem_limit_bytes=VMEM_LIMIT),
        name="adaln",
    )(c, ada_w, ada_b.reshape(nl, 1, n))


def _ff_chunks():
    chunks, c0 = [], 0
    while c0 < D_FF:
        cw = min(FF_CHUNK, D_FF - c0)
        chunks.append((c0, cw))
        c0 += cw
    return tuple(chunks)


def _ffn_kernel(x_ref, mod_ref, wg_ref, wu_ref, wd_ref, lg_ref, lb_ref, o_ref, acc_ref, *, sub):
    x = x_ref[...]
    shift = mod_ref[3 * sub + 0:3 * sub + 1, :]
    scale = mod_ref[3 * sub + 1:3 * sub + 2, :]
    gate = mod_ref[3 * sub + 2:3 * sub + 3, :]
    h = (_layernorm(x) * (1.0 + scale) + shift).astype(BF16)
    for idx, (c0, cw) in enumerate(_ff_chunks()):
        g = jnp.dot(h, wg_ref[:, c0:c0 + cw], preferred_element_type=F32)
        u = jnp.dot(h, wu_ref[:, c0:c0 + cw], preferred_element_type=F32)
        a = (g * _sigmoid(g) * u).astype(BF16)
        d = jnp.dot(a, wd_ref[c0:c0 + cw, :], preferred_element_type=F32)
        if idx == 0:
            acc_ref[...] = d
        else:
            acc_ref[...] += d
    y = ALPHA * x + (FFN_RES * gate) * acc_ref[...]
    o_ref[...] = _layernorm(y) * lg_ref[...] + lb_ref[...]


def _ffn(x, mod, wg, wu, wd, ln_g, ln_b, sub):
    bsz, s, d = x.shape
    tm = TOKEN_TILE
    return pl.pallas_call(
        functools.partial(_ffn_kernel, sub=sub),
        out_shape=jax.ShapeDtypeStruct(x.shape, F32),
        grid=(bsz, s // tm),
        in_specs=[
            pl.BlockSpec((None, tm, d), lambda b, i: (b, i, 0)),
            pl.BlockSpec((None, 9, d), lambda b, i: (b, 0, 0)),
            _resident(wg.shape),
            _resident(wu.shape),
            _resident(wd.shape),
            _resident((1, d)),
            _resident((1, d)),
        ],
        out_specs=pl.BlockSpec((None, tm, d), lambda b, i: (b, i, 0)),
        scratch_shapes=[pltpu.VMEM((tm, d), F32)],
        compiler_params=pltpu.CompilerParams(vmem_limit_bytes=VMEM_LIMIT),
        name=f"ffn{sub}",
    )(x, mod, wg, wu, wd, ln_g.reshape(1, d), ln_b.reshape(1, d))


def _inproj_kernel(x_ref, mod_ref, w_ref, pw_ref, ps_ref, q_ref, k_ref, v_ref, us_ref, yp_ref, ext_ref):
    i = pl.program_id(1)
    tm = x_ref.shape[0]
    x = x_ref[...]
    shift = mod_ref[3:4, :]
    scale = mod_ref[4:5, :]
    h = (_layernorm(x) * (1.0 + scale) + shift).astype(BF16)
    z = jnp.dot(h, w_ref[...], preferred_element_type=F32)
    q_ref[...] = (z[:, 0:D_ATT] * (HEAD_DIM ** -0.5)).astype(BF16)
    k_ref[...] = z[:, D_ATT:2 * D_ATT].astype(BF16)
    v_ref[...] = z[:, 2 * D_ATT:3 * D_ATT].astype(BF16)
    us_ref[...] = z[:, 3 * D_ATT:3 * D_ATT + D_SSM]
    up = z[:, 3 * D_ATT + D_SSM:]

    @pl.when(i == 0)
    def _():
        ext_ref[0:POOL_HALO, :] = jnp.zeros((POOL_HALO, D_POOL), F32)

    @pl.when(i > 0)
    def _():
        ext_ref[0:POOL_HALO, :] = ext_ref[tm:tm + POOL_HALO, :]

    ext_ref[POOL_HALO:POOL_HALO + tm, :] = up
    sums = {}
    acc = up
    for lag in range(1, POOL_HALO):
        acc = acc + ext_ref[POOL_HALO - lag:POOL_HALO - lag + tm, :]
        if lag + 1 in POOL_WINDOWS:
            sums[lag + 1] = acc
    pos = (i * tm + lax.broadcasted_iota(jnp.int32, (tm, 1), 0) + 1).astype(F32)
    group = lax.broadcasted_iota(jnp.int32, (1, D_POOL), 1) // POOL_GROUP
    mean = sums[POOL_WINDOWS[-1]] / jnp.minimum(pos, float(POOL_WINDOWS[-1]))
    for gi in range(len(POOL_WINDOWS) - 2, -1, -1):
        w = POOL_WINDOWS[gi]
        mean = jnp.where(group == gi, sums[w] / jnp.minimum(pos, float(w)), mean)
    pooled = (mean - up).astype(BF16)
    yp = jnp.dot(pooled, pw_ref[...], preferred_element_type=F32) * ps_ref[...]
    yp_ref[...] = yp.astype(BF16)


def _inproj(x, mod, w_in, pool_w_bd, pool_scale):
    bsz, s, d = x.shape
    tm = TOKEN_TILE
    tok = lambda width: pl.BlockSpec((None, tm, width), lambda b, i: (b, i, 0))
    return pl.pallas_call(
        _inproj_kernel,
        out_shape=(
            jax.ShapeDtypeStruct((bsz, s, D_ATT), BF16),
            jax.ShapeDtypeStruct((bsz, s, D_ATT), BF16),
            jax.ShapeDtypeStruct((bsz, s, D_ATT), BF16),
            jax.ShapeDtypeStruct((bsz, s, D_SSM), F32),
            jax.ShapeDtypeStruct((bsz, s, D_POOL), BF16),
        ),
        grid=(bsz, s // tm),
        in_specs=[
            tok(d),
            pl.BlockSpec((None, 9, d), lambda b, i: (b, 0, 0)),
            _resident(w_in.shape),
            _resident(pool_w_bd.shape),
            _resident((1, D_POOL)),
        ],
        out_specs=(tok(D_ATT), tok(D_ATT), tok(D_ATT), tok(D_SSM), tok(D_POOL)),
        scratch_shapes=[pltpu.VMEM((tm + 2 * POOL_HALO, D_POOL), F32)],
        compiler_params=pltpu.CompilerParams(
            dimension_semantics=("arbitrary", "arbitrary"), vmem_limit_bytes=VMEM_LIMIT),
        name="inproj",
    )(x, mod, w_in, pool_w_bd, pool_scale.reshape(1, D_POOL))


def _t5_bucket(dist):
    max_exact = N_BUCKETS // 2
    dd = np.maximum(dist, 1).astype(np.float32)
    large = max_exact + (np.log(dd / max_exact) / math.log(MAX_DISTANCE / max_exact)
                         * (N_BUCKETS - max_exact)).astype(np.int32)
    large = np.minimum(large, N_BUCKETS - 1)
    return np.where(dist < max_exact, dist, large).astype(np.int32)


def _branch_bias(rel_bias, window, dilation, has_prev):
    qb = ATT_BLOCK
    n_keys = window // dilation
    i = np.arange(qb)[:, None]
    j = np.arange(2 * qb)[None, :]
    r = i + qb - j
    in_band = (r >= 0) & (r <= n_keys)
    bucket = _t5_bucket(np.clip(r, 0, None) * dilation)
    bias = jnp.transpose(rel_bias[bucket], (2, 0, 1)).astype(F32)
    bias = jnp.where(in_band[None], bias, NEG)
    return bias if has_prev else bias[:, :, qb:]


def _attn_kernel(*refs, has_prev, has_state, emit_lse):
    refs = list(refs)
    q_ref = refs.pop(0)
    kc_ref = refs.pop(0)
    kp_ref = refs.pop(0) if has_prev else None
    vc_ref = refs.pop(0)
    vp_ref = refs.pop(0) if has_prev else None
    bias_ref = refs.pop(0)
    op_ref = refs.pop(0) if has_state else None
    lp_ref = refs.pop(0) if has_state else None
    o_ref = refs.pop(0)
    lo_ref = refs.pop(0) if emit_lse else None

    qb = ATT_BLOCK
    pair = 2 * HEAD_DIM
    lane = lax.broadcasted_iota(jnp.int32, (1, pair), 1)
    low = lane < HEAD_DIM
    if has_prev:
        first_block = pl.program_id(2) == 0
        kcol = lax.broadcasted_iota(jnp.int32, (1, 2 * qb), 1)
        dead = jnp.logical_and(first_block, kcol < qb)
    lse_lane = lax.broadcasted_iota(jnp.int32, (1, LANES), 1)
    lse_tile = jnp.zeros((qb, LANES), F32)
    lse_prev = lp_ref[...] if has_state else None

    for pi in range(N_HEADS // 2):
        cols = slice(pi * pair, (pi + 1) * pair)
        q2 = q_ref[:, cols]
        if has_prev:
            k2 = jnp.concatenate([kp_ref[:, cols], kc_ref[:, cols]], axis=0)
            v2 = jnp.concatenate([vp_ref[:, cols], vc_ref[:, cols]], axis=0)
        else:
            k2 = kc_ref[:, cols]
            v2 = vc_ref[:, cols]
        zero = jnp.zeros_like(q2)
        outs, scales_new, scales_old = [], [], []
        for hi in range(2):
            h = 2 * pi + hi
            qh = jnp.where(low, q2, zero) if hi == 0 else jnp.where(low, zero, q2)
            s = lax.dot_general(qh, k2, (((1,), (1,)), ((), ())), preferred_element_type=F32)
            s = s + bias_ref[h]
            if has_prev:
                s = jnp.where(dead, NEG, s)
            m = jnp.max(s, axis=-1, keepdims=True)
            p = jnp.exp(s - m)
            l = jnp.sum(p, axis=-1, keepdims=True)
            pv = jnp.dot(p.astype(BF16), v2, preferred_element_type=F32)
            lse_b = m + jnp.log(l)
            if has_state:
                lse_a = lse_prev[:, h:h + 1]
                mx = jnp.maximum(lse_a, lse_b)
                wa = jnp.exp(lse_a - mx)
                wb = jnp.exp(lse_b - mx)
                den = wa + wb
                scales_old.append(wa / den)
                scales_new.append(wb / (den * l))
                lse_new = mx + jnp.log(den)
            else:
                scales_new.append(1.0 / l)
                lse_new = lse_b
            outs.append(pv)
            if emit_lse:
                lse_tile = jnp.where(lse_lane == h, lse_new, lse_tile)
        o2 = jnp.where(low, outs[0] * scales_new[0], outs[1] * scales_new[1])
        if has_state:
            o2 = o2 + op_ref[:, cols].astype(F32) * jnp.where(low, scales_old[0], scales_old[1])
        o_ref[:, cols] = o2.astype(BF16)
    if emit_lse:
        lo_ref[...] = lse_tile


def _attn_branch(q, k, v, bias, dilation, state, emit_lse):
    bsz, s, _ = q.shape
    d = dilation
    ln = s // d
    nb = ln // ATT_BLOCK
    has_prev = nb > 1
    has_state = state is not None
    view = lambda a: a.reshape(bsz, ln, d * a.shape[-1])
    cur = lambda width: pl.BlockSpec((None, ATT_BLOCK, width), lambda b, r, j: (b, j, r))
    prev = lambda width: pl.BlockSpec((None, ATT_BLOCK, width), lambda b, r, j: (b, jnp.maximum(j - 1, 0), r))
    args = [view(q), view(k)]
    specs = [cur(D_ATT), cur(D_ATT)]
    if has_prev:
        args.append(view(k))
        specs.append(prev(D_ATT))
    args.append(view(v))
    specs.append(cur(D_ATT))
    if has_prev:
        args.append(view(v))
        specs.append(prev(D_ATT))
    args.append(bias)
    specs.append(_resident(bias.shape))
    if has_state:
        args += [view(state[0]), view(state[1])]
        specs += [cur(D_ATT), cur(LANES)]
    out_shape = [jax.ShapeDtypeStruct((bsz, ln, d * D_ATT), BF16)]
    out_specs = [cur(D_ATT)]
    if emit_lse:
        out_shape.append(jax.ShapeDtypeStruct((bsz, ln, d * LANES), F32))
        out_specs.append(cur(LANES))
    res = pl.pallas_call(
        functools.partial(_attn_kernel, has_prev=has_prev, has_state=has_state, emit_lse=emit_lse),
        out_shape=tuple(out_shape),
        grid=(bsz, d, nb),
        in_specs=specs,
        out_specs=tuple(out_specs),
        compiler_params=pltpu.CompilerParams(vmem_limit_bytes=VMEM_LIMIT),
        name=f"attn_d{d}",
    )(*args)
    o = res[0].reshape(bsz, s, D_ATT)
    if emit_lse:
        return o, res[1].reshape(bsz, s, LANES)
    return o


def _dilated_attention(q, k, v, rel_bias):
    state = None
    for idx, (window, dilation) in enumerate(DILATED_PATTERNS):
        nb = (q.shape[1] // dilation) // ATT_BLOCK
        bias = _branch_bias(rel_bias, window, dilation, nb > 1)
        last = idx == len(DILATED_PATTERNS) - 1
        state = _attn_branch(q, k, v, bias, dilation, state, emit_lse=not last)
    return state


def _ssm_tables(a_re, a_im, log_dt, b_re, b_im, c_re, c_im, d_skip):
    hi = lax.Precision.HIGHEST
    t, g, p, c = SSM_CHUNK, N_SSM_GROUPS, SSM_STATE, SSM_GROUP
    dt = jnp.exp(log_dt)[:, None]
    mag = jnp.exp(a_re * dt)
    ar, ai = mag * jnp.cos(a_im * dt), mag * jnp.sin(a_im * dt)
    den = a_re * a_re + a_im * a_im
    fr = ((ar - 1.0) * a_re + ai * a_im) / den
    fi = (ai * a_re - (ar - 1.0) * a_im) / den
    bbr = fr[:, :, None] * b_re - fi[:, :, None] * b_im
    bbi = fr[:, :, None] * b_im + fi[:, :, None] * b_re
    pr, pi_ = [jnp.ones_like(ar)], [jnp.zeros_like(ar)]
    for _ in range(t):
        pr.append(pr[-1] * ar - pi_[-1] * ai)
        pi_.append(pr[-2] * ai + pi_[-1] * ar)
    eye = jnp.eye(g, dtype=F32)

    win = []
    for j in range(t):
        qr, qi = pr[t - 1 - j][:, :, None], pi_[t - 1 - j][:, :, None]
        win.append(jnp.stack([qr * bbr - qi * bbi, qr * bbi + qi * bbr], 0))
    win = jnp.stack(win, 0)
    w_in = jnp.einsum('jrgpc,gh->jgcrhp', win, eye).reshape(t * g * c, 2 * g * p)

    wout, hs = [], []
    for j in range(t + 1):
        wr = c_re * pr[j][:, None, :] - c_im * pi_[j][:, None, :]
        wi = c_re * pi_[j][:, None, :] + c_im * pr[j][:, None, :]
        if j >= 1:
            wout.append(jnp.stack([wr, -wi], 0))
        if j < t:
            hs.append(jnp.einsum('gcp,gpd->gdc', wr, bbr, precision=hi)
                      - jnp.einsum('gcp,gpd->gdc', wi, bbi, precision=hi))
    wout = jnp.stack(wout, 0)
    w_out = jnp.einsum('jrgcp,gh->rgpjhc', wout, eye).reshape(2 * g * p, t * g * c)

    hs = jnp.stack(hs, 0)
    lagi = np.arange(t)[None, :] - np.arange(t)[:, None]
    toep = hs[np.clip(lagi, 0, None)] * jnp.asarray(lagi >= 0, F32)[:, :, None, None, None]
    w_intra = jnp.einsum('abgdc,gh->agdbhc', toep, eye).reshape(t * g * c, t * g * c)

    abar_t = jnp.concatenate([pr[t].reshape(1, g * p), pi_[t].reshape(1, g * p)], axis=1)
    d_row = jnp.tile(d_skip.reshape(1, D_SSM), (1, t))
    return w_in.astype(BF16), w_intra.astype(BF16), w_out.astype(BF16), abar_t, d_row


def _ssm_kernel(u_ref, win_ref, wintra_ref, wout_ref, abar_ref, d_ref, y_ref, z_ref, xp_ref):
    n = u_ref.shape[0]
    half = N_SSM_GROUPS * SSM_STATE
    u = u_ref[...]
    ub = u.astype(BF16)
    z_ref[...] = jnp.dot(ub, win_ref[...], preferred_element_type=F32)
    ar = abar_ref[:, 0:half]
    ai = abar_ref[:, half:]

    def step(k, carry):
        xr, xi = carry
        xp_ref[pl.ds(k, 1), 0:half] = xr
        xp_ref[pl.ds(k, 1), half:] = xi
        zr = z_ref[pl.ds(k, 1), 0:half]
        zi = z_ref[pl.ds(k, 1), half:]
        return ar * xr - ai * xi + zr, ar * xi + ai * xr + zi

    zero = jnp.zeros((1, half), F32)
    lax.fori_loop(0, n, step, (zero, zero))
    y = jnp.dot(ub, wintra_ref[...], preferred_element_type=F32)
    y = y + jnp.dot(xp_ref[...].astype(BF16), wout_ref[...], preferred_element_type=F32)
    y_ref[...] = y + d_ref[...] * u


def _ssm(u, tables):
    w_in, w_intra, w_out, abar_t, d_row = tables
    bsz, s, c = u.shape
    n, width = s // SSM_CHUNK, SSM_CHUNK * c
    row = pl.BlockSpec((None, n, width), lambda b: (b, 0, 0))
    y = pl.pallas_call(
        _ssm_kernel,
        out_shape=jax.ShapeDtypeStruct((bsz, n, width), F32),
        grid=(bsz,),
        in_specs=[row, _resident(w_in.shape), _resident(w_intra.shape), _resident(w_out.shape),
                  _resident(abar_t.shape), _resident(d_row.shape)],
        out_specs=row,
        scratch_shapes=[pltpu.VMEM((n, w_in.shape[1]), F32), pltpu.VMEM((n, w_in.shape[1]), F32)],
        compiler_params=pltpu.CompilerParams(vmem_limit_bytes=VMEM_LIMIT),
        name="ssm",
    )(u.reshape(bsz, n, width), w_in, w_intra, w_out, abar_t, d_row)
    return y.reshape(bsz, s, c)


def _outproj_kernel(x_ref, mod_ref, ya_ref, ys_ref, yp_ref, gw_ref, gb_ref, wa_ref, ws_ref, wp_ref,
                    lg_ref, lb_ref, o_ref):
    x = x_ref[...]
    gate = mod_ref[5:6, :]
    ys = ys_ref[...]
    cdf = 0.5 * (1.0 + jnp.tanh(math.sqrt(2.0 / math.pi) * (ys + 0.044715 * (ys * ys * ys))))
    t = jnp.dot((ys * cdf).astype(BF16), gw_ref[...], preferred_element_type=F32) + gb_ref[...]
    yg = (ys * _sigmoid(t)).astype(BF16)
    y = jnp.dot(ya_ref[...], wa_ref[...], preferred_element_type=F32)
    y = y + jnp.dot(yg, ws_ref[...], preferred_element_type=F32)
    y = y + jnp.dot(yp_ref[...], wp_ref[...], preferred_element_type=F32)
    r = ALPHA * x + gate * y
    o_ref[...] = _layernorm(r) * lg_ref[...] + lb_ref[...]


def _outproj(x, mod, y_att, y_ssm, y_pool, glu_w, glu_b, w_out, ln_g, ln_b):
    bsz, s, d = x.shape
    tm = TOKEN_TILE
    tok = lambda width: pl.BlockSpec((None, tm, width), lambda b, i: (b, i, 0))
    wa, ws, wp = w_out[:D_ATT], w_out[D_ATT:D_ATT + D_SSM], w_out[D_ATT + D_SSM:]
    return pl.pallas_call(
        _outproj_kernel,
        out_shape=jax.ShapeDtypeStruct(x.shape, F32),
        grid=(bsz, s // tm),
        in_specs=[
            tok(d),
            pl.BlockSpec((None, 9, d), lambda b, i: (b, 0, 0)),
            tok(D_ATT), tok(D_SSM), tok(D_POOL),
            _resident(glu_w.shape), _resident((1, D_SSM)),
            _resident(wa.shape), _resident(ws.shape), _resident(wp.shape),
            _resident((1, d)), _resident((1, d)),
        ],
        out_specs=tok(d),
        compiler_params=pltpu.CompilerParams(vmem_limit_bytes=VMEM_LIMIT),
        name="outproj",
    )(x, mod, y_att, y_ssm, y_pool, glu_w, glu_b.reshape(1, D_SSM), wa, ws, wp,
      ln_g.reshape(1, d), ln_b.reshape(1, d))


def _block_diag(w):
    g, n, _ = w.shape
    return jnp.einsum('gab,gh->gahb', w, jnp.eye(g, dtype=w.dtype)).reshape(g * n, g * n)


def kernel(x, c, rel_bias, ada_w, ada_b, ln_g, ln_b, ffn_w_gate, ffn_w_up, ffn_w_down, w_in, w_out,
           ssm_a_re, ssm_a_im, ssm_log_dt, ssm_b_re, ssm_b_im, ssm_c_re, ssm_c_im, ssm_d, glu_w, glu_b,
           pool_w, pool_scale):
    bsz = x.shape[0]
    mod_all = _adaln(c, ada_w, ada_b).reshape(DEPTH, bsz, 9, D_MODEL)
    wg, wu, wd = ffn_w_gate.astype(BF16), ffn_w_up.astype(BF16), ffn_w_down.astype(BF16)
    w_in_b, w_out_b, glu_w_b = w_in.astype(BF16), w_out.astype(BF16), glu_w.astype(BF16)
    for l in range(DEPTH):
        mod = mod_all[l]
        x = _ffn(x, mod, wg[l, 0], wu[l, 0], wd[l, 0], ln_g[l, 0], ln_b[l, 0], sub=0)
        q, k, v, u_ssm, y_pool = _inproj(x, mod, w_in_b[l], _block_diag(pool_w[l]).astype(BF16), pool_scale[l])
        y_att = _dilated_attention(q, k, v, rel_bias)
        tables = _ssm_tables(ssm_a_re[l], ssm_a_im[l], ssm_log_dt[l], ssm_b_re[l], ssm_b_im[l],
                             ssm_c_re[l], ssm_c_im[l], ssm_d[l])
        y_ssm = _ssm(u_ssm, tables)
        x = _outproj(x, mod, y_att, y_ssm, y_pool, glu_w_b[l], glu_b[l], w_out_b[l], ln_g[l, 1], ln_b[l, 1])
        x = _ffn(x, mod, wg[l, 1], wu[l, 1], wd[l, 1], ln_g[l, 2], ln_b[l, 2], sub=2)
    return x
```

```python
import functools
import math

import jax
import jax.numpy as jnp
import numpy as np
from jax import lax
from jax.experimental import pallas as pl
from jax.experimental.pallas import tpu as pltpu

F32 = jnp.float32
BF16 = jnp.bfloat16

D_MODEL = 1024
DEPTH = 2
HEAD_DIM = 64
N_HEADS = 8
D_ATT = N_HEADS * HEAD_DIM
DILATED_PATTERNS = ((128, 1), (512, 4), (2048, 16))
DILATIONS = tuple(d for _, d in DILATED_PATTERNS)
ATT_BLOCK = 128
SSM_GROUP = 16
D_SSM = 256
N_SSM_GROUPS = D_SSM // SSM_GROUP
SSM_STATE = 64
POOL_WINDOWS = (2, 4, 8, 16)
D_POOL = 256
POOL_GROUP = D_POOL // len(POOL_WINDOWS)
D_IN = 3 * D_ATT + D_SSM + D_POOL
D_FF = 2816
N_BUCKETS = 32
MAX_DISTANCE = 2048
ALPHA = (2 * DEPTH) ** 0.25
FFN_RES = 0.5
LN_EPS = 1e-5
NEG = -1e30

LANES = 128
TOKEN_TILE = 512
FF_CHUNK = 512
ADA_COL_TILE = 1152
SSM_CHUNK = 8
POOL_HALO = max(POOL_WINDOWS)
VMEM_LIMIT = 56 * 1024 * 1024


def _sigmoid(x):
    return 1.0 / (1.0 + jnp.exp(-x))


def _layernorm(x):
    mu = jnp.mean(x, axis=-1, keepdims=True)
    xc = x - mu
    var = jnp.mean(xc * xc, axis=-1, keepdims=True)
    return xc * lax.rsqrt(var + LN_EPS)


def _resident(shape):
    zeros = (0,) * len(shape)
    return pl.BlockSpec(shape, lambda *_: zeros, pipeline_mode=pl.Buffered(1))


def _stacked(tail, *lead):
    idx = tuple(lead) + (0,) * len(tail)
    return pl.BlockSpec((None,) * len(lead) + tuple(tail), lambda *_: idx, pipeline_mode=pl.Buffered(1))


def _adaln_kernel(c_ref, w_ref, b_ref, o_ref):
    c = c_ref[...]
    cond = (c * _sigmoid(c)).astype(BF16)
    o_ref[...] = jnp.dot(cond, w_ref[...].astype(BF16), preferred_element_type=F32) + b_ref[...]


def _adaln(c, ada_w, ada_b):
    nl, d, n = ada_w.shape
    bsz = c.shape[0]
    return pl.pallas_call(
        _adaln_kernel,
        out_shape=jax.ShapeDtypeStruct((nl, bsz, n), F32),
        grid=(nl, n // ADA_COL_TILE),
        in_specs=[
            pl.BlockSpec((bsz, d), lambda l, j: (0, 0)),
            pl.BlockSpec((None, d, ADA_COL_TILE), lambda l, j: (l, 0, j)),
            pl.BlockSpec((None, 1, ADA_COL_TILE), lambda l, j: (l, 0, j)),
        ],
        out_specs=pl.BlockSpec((None, bsz, ADA_COL_TILE), lambda l, j: (l, 0, j)),
        compiler_params=pltpu.CompilerParams(vmem_limit_bytes=VMEM_LIMIT),
        name="adaln",
    )(c, ada_w, ada_b.reshape(nl, 1, n))


def _ff_chunks():
    chunks, c0 = [], 0
    while c0 < D_FF:
        cw = min(FF_CHUNK, D_FF - c0)
        chunks.append((c0, cw))
        c0 += cw
    return tuple(chunks)


def _ffn_kernel(x_ref, mod_ref, wg_ref, wu_ref, wd_ref, lg_ref, lb_ref, o_ref, acc_ref, *, sub):
    x = x_ref[...]
    shift = mod_ref[3 * sub + 0:3 * sub + 1, :]
    scale = mod_ref[3 * sub + 1:3 * sub + 2, :]
    gate = mod_ref[3 * sub + 2:3 * sub + 3, :]
    h = (_layernorm(x) * (1.0 + scale) + shift).astype(BF16)
    for idx, (c0, cw) in enumerate(_ff_chunks()):
        g = jnp.dot(h, wg_ref[:, c0:c0 + cw], preferred_element_type=F32)
        u = jnp.dot(h, wu_ref[:, c0:c0 + cw], preferred_element_type=F32)
        a = (g * _sigmoid(g) * u).astype(BF16)
        d = jnp.dot(a, wd_ref[c0:c0 + cw, :], preferred_element_type=F32)
        if idx == 0:
            acc_ref[...] = d
        else:
            acc_ref[...] += d
    y = ALPHA * x + (FFN_RES * gate) * acc_ref[...]
    o_ref[...] = _layernorm(y) * lg_ref[...] + lb_ref[...]


def _mod_spec(l, d):
    return pl.BlockSpec((None, None, 9, d), lambda b, i: (l, b, 0, 0))


def _ffn(x, mod_all, wg, wu, wd, ln_g, ln_b, l, which):
    bsz, s, d = x.shape
    tm = TOKEN_TILE
    sub = 2 * which
    return pl.pallas_call(
        functools.partial(_ffn_kernel, sub=sub),
        out_shape=jax.ShapeDtypeStruct(x.shape, F32),
        grid=(bsz, s // tm),
        in_specs=[
            pl.BlockSpec((None, tm, d), lambda b, i: (b, i, 0)),
            _mod_spec(l, d),
            _stacked(wg.shape[2:], l, which),
            _stacked(wu.shape[2:], l, which),
            _stacked(wd.shape[2:], l, which),
            _stacked((1, d), l, sub),
            _stacked((1, d), l, sub),
        ],
        out_specs=pl.BlockSpec((None, tm, d), lambda b, i: (b, i, 0)),
        scratch_shapes=[pltpu.VMEM((tm, d), F32)],
        compiler_params=pltpu.CompilerParams(vmem_limit_bytes=VMEM_LIMIT),
        name=f"ffn{sub}",
    )(x, mod_all, wg, wu, wd, ln_g, ln_b)


def _inproj_kernel(x_ref, mod_ref, w_ref, pw_ref, ps_ref, *rest):
    qkv_refs = rest[:3 * len(DILATIONS)]
    us_ref, yp_ref, ext_ref, zs_ref = rest[3 * len(DILATIONS):]
    i = pl.program_id(1)
    tm = x_ref.shape[0]
    x = x_ref[...]
    shift = mod_ref[3:4, :]
    scale = mod_ref[4:5, :]
    h = (_layernorm(x) * (1.0 + scale) + shift).astype(BF16)
    z = jnp.dot(h, w_ref[...], preferred_element_type=F32)
    us_ref[...] = z[:, 3 * D_ATT:3 * D_ATT + D_SSM]
    up = z[:, 3 * D_ATT + D_SSM:]

    slabs_per = D_ATT // LANES
    for c in range(3 * slabs_per):
        col = z[:, c * LANES:(c + 1) * LANES]
        zs_ref[c] = col * (HEAD_DIM ** -0.5) if c < slabs_per else col
    for di, d in enumerate(DILATIONS):
        rows = tm // d
        for c in range(3 * slabs_per):
            dst = qkv_refs[3 * di + c // slabs_per]
            lanes = slice((c % slabs_per) * LANES, (c % slabs_per + 1) * LANES)
            for r in range(d):
                src = zs_ref[c] if d == 1 else zs_ref[c, pl.ds(r, rows, stride=d), :]
                dst[r, :, lanes] = src.astype(BF16)

    @pl.when(i == 0)
    def _():
        ext_ref[0:POOL_HALO, :] = jnp.zeros((POOL_HALO, D_POOL), F32)

    @pl.when(i > 0)
    def _():
        ext_ref[0:POOL_HALO, :] = ext_ref[tm:tm + POOL_HALO, :]

    ext_ref[POOL_HALO:POOL_HALO + tm, :] = up
    sums = {}
    acc = up
    for lag in range(1, POOL_HALO):
        acc = acc + ext_ref[POOL_HALO - lag:POOL_HALO - lag + tm, :]
        if lag + 1 in POOL_WINDOWS:
            sums[lag + 1] = acc
    pos = (i * tm + lax.broadcasted_iota(jnp.int32, (tm, 1), 0) + 1).astype(F32)
    group = lax.broadcasted_iota(jnp.int32, (1, D_POOL), 1) // POOL_GROUP
    mean = sums[POOL_WINDOWS[-1]] / jnp.minimum(pos, float(POOL_WINDOWS[-1]))
    for gi in range(len(POOL_WINDOWS) - 2, -1, -1):
        w = POOL_WINDOWS[gi]
        mean = jnp.where(group == gi, sums[w] / jnp.minimum(pos, float(w)), mean)
    pooled = (mean - up).astype(BF16)
    yp = jnp.dot(pooled, pw_ref[...], preferred_element_type=F32) * ps_ref[...]
    yp_ref[...] = yp.astype(BF16)


def _inproj(x, mod_all, w_in, pool_w_bd, pool_scale, l):
    bsz, s, d = x.shape
    tm = TOKEN_TILE
    tok = lambda width: pl.BlockSpec((None, tm, width), lambda b, i: (b, i, 0))
    qkv_shapes, qkv_specs = [], []
    for dil in DILATIONS:
        for _ in range(3):
            qkv_shapes.append(jax.ShapeDtypeStruct((bsz, dil, s // dil, D_ATT), BF16))
            qkv_specs.append(pl.BlockSpec((None, dil, tm // dil, D_ATT), lambda b, i: (b, 0, i, 0)))
    res = pl.pallas_call(
        _inproj_kernel,
        out_shape=tuple(qkv_shapes) + (
            jax.ShapeDtypeStruct((bsz, s, D_SSM), F32),
            jax.ShapeDtypeStruct((bsz, s, D_POOL), BF16),
        ),
        grid=(bsz, s // tm),
        in_specs=[
            tok(d),
            _mod_spec(l, d),
            _stacked(w_in.shape[1:], l),
            _resident(pool_w_bd.shape),
            _stacked((1, D_POOL), l),
        ],
        out_specs=tuple(qkv_specs) + (tok(D_SSM), tok(D_POOL)),
        scratch_shapes=[pltpu.VMEM((tm + POOL_HALO, D_POOL), F32),
                        pltpu.VMEM((3 * D_ATT // LANES, tm, LANES), F32)],
        compiler_params=pltpu.CompilerParams(
            dimension_semantics=("arbitrary", "arbitrary"), vmem_limit_bytes=VMEM_LIMIT),
        name="inproj",
    )(x, mod_all, w_in, pool_w_bd, pool_scale)
    qkv = [tuple(res[3 * di:3 * di + 3]) for di in range(len(DILATIONS))]
    return qkv, res[-2], res[-1]


def _t5_bucket(dist):
    max_exact = N_BUCKETS // 2
    dd = np.maximum(dist, 1).astype(np.float32)
    large = max_exact + (np.log(dd / max_exact) / math.log(MAX_DISTANCE / max_exact)
                         * (N_BUCKETS - max_exact)).astype(np.int32)
    large = np.minimum(large, N_BUCKETS - 1)
    return np.where(dist < max_exact, dist, large).astype(np.int32)


def _branch_bias(rel_bias, window, dilation, has_prev):
    qb = ATT_BLOCK
    n_keys = window // dilation
    assert n_keys == qb
    period = 3 * qb
    dist = np.arange(n_keys, -1, -1)
    row = rel_bias[_t5_bucket(dist * dilation)].T.astype(F32)
    row = jnp.concatenate([row, jnp.full((N_HEADS, period - n_keys - 1), NEG, F32)], axis=1)
    flat = jnp.tile(row, (1, qb))[:, :qb * (period - 1)]
    bias = flat.reshape(N_HEADS, qb, period - 1)[:, :, :2 * qb]
    return bias if has_prev else bias[:, :, qb:]


def _attn_kernel(*refs, has_prev):
    refs = list(refs)
    q_ref = refs.pop(0)
    kc_ref = refs.pop(0)
    kp_ref = refs.pop(0) if has_prev else None
    vc_ref = refs.pop(0)
    vp_ref = refs.pop(0) if has_prev else None
    bias_ref, o_ref, lse_ref = refs

    qb = ATT_BLOCK
    pair = 2 * HEAD_DIM
    lane = lax.broadcasted_iota(jnp.int32, (1, pair), 1)
    low = lane < HEAD_DIM
    if has_prev:
        first_block = pl.program_id(2) == 0
        kcol = lax.broadcasted_iota(jnp.int32, (1, 2 * qb), 1)
        dead = jnp.logical_and(first_block, kcol < qb)
    lse_lane = lax.broadcasted_iota(jnp.int32, (1, LANES), 1)
    lse_tile = jnp.zeros((qb, LANES), F32)

    for pi in range(N_HEADS // 2):
        cols = slice(pi * pair, (pi + 1) * pair)
        q2 = q_ref[:, cols]
        if has_prev:
            k2 = jnp.concatenate([kp_ref[:, cols], kc_ref[:, cols]], axis=0)
            v2 = jnp.concatenate([vp_ref[:, cols], vc_ref[:, cols]], axis=0)
        else:
            k2 = kc_ref[:, cols]
            v2 = vc_ref[:, cols]
        zero = jnp.zeros_like(q2)
        outs = []
        for hi in range(2):
            h = 2 * pi + hi
            qh = jnp.where(low, q2, zero) if hi == 0 else jnp.where(low, zero, q2)
            s = lax.dot_general(qh, k2, (((1,), (1,)), ((), ())), preferred_element_type=F32)
            s = s + bias_ref[h]
            if has_prev:
                s = jnp.where(dead, NEG, s)
            m = jnp.max(s, axis=-1, keepdims=True)
            p = jnp.exp(s - m)
            l = jnp.sum(p, axis=-1, keepdims=True)
            pv = jnp.dot(p.astype(BF16), v2, preferred_element_type=F32)
            outs.append(pv * (1.0 / l))
            lse_tile = jnp.where(lse_lane == h, m + jnp.log(l), lse_tile)
        o_ref[:, cols] = jnp.where(low, outs[0], outs[1]).astype(BF16)
    lse_ref[...] = lse_tile


def _attn_branch(q, k, v, bias):
    bsz, d, ln, _ = q.shape
    nb = ln // ATT_BLOCK
    has_prev = nb > 1
    cur = lambda width: pl.BlockSpec((None, None, ATT_BLOCK, width), lambda b, r, j: (b, r, j, 0))
    prev = lambda width: pl.BlockSpec((None, None, ATT_BLOCK, width),
                                      lambda b, r, j: (b, r, jnp.maximum(j - 1, 0), 0))
    args, specs = [q, k], [cur(D_ATT), cur(D_ATT)]
    if has_prev:
        args.append(k)
        specs.append(prev(D_ATT))
    args.append(v)
    specs.append(cur(D_ATT))
    if has_prev:
        args.append(v)
        specs.append(prev(D_ATT))
    args.append(bias)
    specs.append(_resident(bias.shape))
    return pl.pallas_call(
        functools.partial(_attn_kernel, has_prev=has_prev),
        out_shape=(jax.ShapeDtypeStruct((bsz, d, ln, D_ATT), BF16),
                   jax.ShapeDtypeStruct((bsz, d, ln, LANES), F32)),
        grid=(bsz, d, nb),
        in_specs=specs,
        out_specs=(cur(D_ATT), cur(LANES)),
        compiler_params=pltpu.CompilerParams(vmem_limit_bytes=VMEM_LIMIT),
        name=f"attn_d{d}",
    )(*args)


def _dilated_attention(qkv, rel_bias):
    outs = []
    for (window, dilation), (q, k, v) in zip(DILATED_PATTERNS, qkv):
        has_prev = q.shape[2] // ATT_BLOCK > 1
        outs.append(_attn_branch(q, k, v, _branch_bias(rel_bias, window, dilation, has_prev)))
    return outs


def _ssm_tables(a_re, a_im, log_dt, b_re, b_im, c_re, c_im, d_skip):
    hi = lax.Precision.HIGHEST
    t, g, p, c = SSM_CHUNK, N_SSM_GROUPS, SSM_STATE, SSM_GROUP
    dt = jnp.exp(log_dt)[:, None]
    mag = jnp.exp(a_re * dt)
    ar, ai = mag * jnp.cos(a_im * dt), mag * jnp.sin(a_im * dt)
    den = a_re * a_re + a_im * a_im
    fr = ((ar - 1.0) * a_re + ai * a_im) / den
    fi = (ai * a_re - (ar - 1.0) * a_im) / den
    bbr = fr[:, :, None] * b_re - fi[:, :, None] * b_im
    bbi = fr[:, :, None] * b_im + fi[:, :, None] * b_re
    pr, pi_ = [jnp.ones_like(ar)], [jnp.zeros_like(ar)]
    for _ in range(t):
        pr.append(pr[-1] * ar - pi_[-1] * ai)
        pi_.append(pr[-2] * ai + pi_[-1] * ar)
    eye = jnp.eye(g, dtype=F32)

    win = []
    for j in range(t):
        qr, qi = pr[t - 1 - j][:, :, None], pi_[t - 1 - j][:, :, None]
        win.append(jnp.stack([qr * bbr - qi * bbi, qr * bbi + qi * bbr], 0))
    win = jnp.stack(win, 0)
    w_in = jnp.einsum('jrgpc,gh->jgcrhp', win, eye).reshape(t * g * c, 2 * g * p)

    wout, hs = [], []
    for j in range(t + 1):
        wr = c_re * pr[j][:, None, :] - c_im * pi_[j][:, None, :]
        wi = c_re * pi_[j][:, None, :] + c_im * pr[j][:, None, :]
        if j >= 1:
            wout.append(jnp.stack([wr, -wi], 0))
        if j < t:
            hs.append(jnp.einsum('gcp,gpd->gdc', wr, bbr, precision=hi)
                      - jnp.einsum('gcp,gpd->gdc', wi, bbi, precision=hi))
    wout = jnp.stack(wout, 0)
    w_out = jnp.einsum('jrgcp,gh->rgpjhc', wout, eye).reshape(2 * g * p, t * g * c)

    hs = jnp.stack(hs, 0)
    lagi = np.arange(t)[None, :] - np.arange(t)[:, None]
    toep = hs[np.clip(lagi, 0, None)] * jnp.asarray(lagi >= 0, F32)[:, :, None, None, None]
    w_intra = jnp.einsum('abgdc,gh->agdbhc', toep, eye).reshape(t * g * c, t * g * c)

    abar_t = jnp.concatenate([pr[t].reshape(1, g * p), pi_[t].reshape(1, g * p)], axis=1)
    d_row = jnp.tile(d_skip.reshape(1, D_SSM), (1, t))
    return w_in.astype(BF16), w_intra.astype(BF16), w_out.astype(BF16), abar_t, d_row


def _ssm_kernel(u_ref, win_ref, wintra_ref, wout_ref, abar_ref, d_ref, y_ref, z_ref, xp_ref):
    n = u_ref.shape[0]
    half = N_SSM_GROUPS * SSM_STATE
    u = u_ref[...]
    ub = u.astype(BF16)
    z_ref[...] = jnp.dot(ub, win_ref[...], preferred_element_type=F32)
    ar = abar_ref[:, 0:half]
    ai = abar_ref[:, half:]

    def step(k, carry):
        xr, xi = carry
        xp_ref[pl.ds(k, 1), 0:half] = xr
        xp_ref[pl.ds(k, 1), half:] = xi
        zr = z_ref[pl.ds(k, 1), 0:half]
        zi = z_ref[pl.ds(k, 1), half:]
        return ar * xr - ai * xi + zr, ar * xi + ai * xr + zi

    zero = jnp.zeros((1, half), F32)
    lax.fori_loop(0, n, step, (zero, zero))
    y = jnp.dot(ub, wintra_ref[...], preferred_element_type=F32)
    y = y + jnp.dot(xp_ref[...].astype(BF16), wout_ref[...], preferred_element_type=F32)
    y_ref[...] = y + d_ref[...] * u


def _ssm(u, tables):
    w_in, w_intra, w_out, abar_t, d_row = tables
    bsz, s, c = u.shape
    n, width = s // SSM_CHUNK, SSM_CHUNK * c
    row = pl.BlockSpec((None, n, width), lambda b: (b, 0, 0))
    y = pl.pallas_call(
        _ssm_kernel,
        out_shape=jax.ShapeDtypeStruct((bsz, n, width), F32),
        grid=(bsz,),
        in_specs=[row, _resident(w_in.shape), _resident(w_intra.shape), _resident(w_out.shape),
                  _resident(abar_t.shape), _resident(d_row.shape)],
        out_specs=row,
        scratch_shapes=[pltpu.VMEM((n, w_in.shape[1]), F32), pltpu.VMEM((n, w_in.shape[1]), F32)],
        compiler_params=pltpu.CompilerParams(vmem_limit_bytes=VMEM_LIMIT),
        name="ssm",
    )(u.reshape(bsz, n, width), w_in, w_intra, w_out, abar_t, d_row)
    return y.reshape(bsz, s, c)


def _outproj_kernel(x_ref, mod_ref, *rest):
    nd = len(DILATIONS)
    o_refs, l_refs = rest[0:nd], rest[nd:2 * nd]
    (ys_ref, yp_ref, gw_ref, gb_ref, wa_ref, ws_ref, wp_ref, lg_ref, lb_ref, o_ref) = rest[2 * nd:2 * nd + 10]
    on_refs, ln_refs = rest[2 * nd + 10:3 * nd + 9], rest[3 * nd + 9:]
    tm = x_ref.shape[0]
    slabs = D_ATT // LANES

    o_nat, l_nat = [o_refs[0][0].astype(F32)], [l_refs[0][0]]
    for di in range(1, nd):
        d = DILATIONS[di]
        rows = tm // d
        on, ln = on_refs[di - 1], ln_refs[di - 1]
        for r in range(d):
            ln[pl.ds(r, rows, stride=d), :] = l_refs[di][r]
            for c in range(slabs):
                on[c, pl.ds(r, rows, stride=d), :] = o_refs[di][r, :, c * LANES:(c + 1) * LANES].astype(F32)
        o_nat.append(jnp.concatenate([on[c] for c in range(slabs)], axis=1))
        l_nat.append(ln[...])

    mx = functools.reduce(jnp.maximum, l_nat)
    es = [jnp.exp(lv - mx) for lv in l_nat]
    inv = 1.0 / functools.reduce(lambda a, b: a + b, es)
    head = lax.broadcasted_iota(jnp.int32, (LANES, D_ATT), 1) // HEAD_DIM
    spread = (head == lax.broadcasted_iota(jnp.int32, (LANES, D_ATT), 0)).astype(BF16)
    ya = jnp.zeros((tm, D_ATT), F32)
    for e, ov in zip(es, o_nat):
        w = e * inv
        w_hi = w.astype(BF16)
        w_lo = (w - w_hi.astype(F32)).astype(BF16)
        wx = (jnp.dot(w_hi, spread, preferred_element_type=F32)
              + jnp.dot(w_lo, spread, preferred_element_type=F32))
        ya = ya + wx * ov

    x = x_ref[...]
    gate = mod_ref[5:6, :]
    ys = ys_ref[...]
    cdf = 0.5 * (1.0 + jnp.tanh(math.sqrt(2.0 / math.pi) * (ys + 0.044715 * (ys * ys * ys))))
    t = jnp.dot((ys * cdf).astype(BF16), gw_ref[...], preferred_element_type=F32) + gb_ref[...]
    yg = (ys * _sigmoid(t)).astype(BF16)
    y = jnp.dot(ya.astype(BF16), wa_ref[...], preferred_element_type=F32)
    y = y + jnp.dot(yg, ws_ref[...], preferred_element_type=F32)
    y = y + jnp.dot(yp_ref[...], wp_ref[...], preferred_element_type=F32)
    r = ALPHA * x + gate * y
    o_ref[...] = _layernorm(r) * lg_ref[...] + lb_ref[...]


def _outproj(x, mod_all, att, y_ssm, y_pool, glu_w, glu_b, w_out, ln_g, ln_b, l):
    bsz, s, d = x.shape
    tm = TOKEN_TILE
    tok = lambda width: pl.BlockSpec((None, tm, width), lambda b, i: (b, i, 0))
    res = lambda dil, width: pl.BlockSpec((None, dil, tm // dil, width), lambda b, i: (b, 0, i, 0))
    rows_of = lambda r0, n: pl.BlockSpec((None, n, d), lambda *_: (l, r0 // n, 0), pipeline_mode=pl.Buffered(1))
    assert D_ATT % D_SSM == 0 and D_SSM == D_POOL
    scratch = []
    for dil in DILATIONS[1:]:
        scratch.append(pltpu.VMEM((D_ATT // LANES, tm, LANES), F32))
    for dil in DILATIONS[1:]:
        scratch.append(pltpu.VMEM((tm, LANES), F32))
    return pl.pallas_call(
        _outproj_kernel,
        out_shape=jax.ShapeDtypeStruct(x.shape, F32),
        grid=(bsz, s // tm),
        in_specs=[tok(d), _mod_spec(l, d)]
        + [res(dil, D_ATT) for dil in DILATIONS]
        + [res(dil, LANES) for dil in DILATIONS]
        + [tok(D_SSM), tok(D_POOL),
           _stacked(glu_w.shape[1:], l), _stacked((1, D_SSM), l),
           rows_of(0, D_ATT), rows_of(D_ATT, D_SSM), rows_of(D_ATT + D_SSM, D_POOL),
           _stacked((1, d), l, 1), _stacked((1, d), l, 1)],
        out_specs=tok(d),
        scratch_shapes=scratch,
        compiler_params=pltpu.CompilerParams(vmem_limit_bytes=VMEM_LIMIT),
        name="outproj",
    )(x, mod_all, *[o for o, _ in att], *[lse for _, lse in att], y_ssm, y_pool,
      glu_w, glu_b, w_out, w_out, w_out, ln_g, ln_b)


def _block_diag(w):
    g, n, _ = w.shape
    return jnp.einsum('gab,gh->gahb', w, jnp.eye(g, dtype=w.dtype)).reshape(g * n, g * n)


def kernel(x, c, rel_bias, ada_w, ada_b, ln_g, ln_b, ffn_w_gate, ffn_w_up, ffn_w_down, w_in, w_out,
           ssm_a_re, ssm_a_im, ssm_log_dt, ssm_b_re, ssm_b_im, ssm_c_re, ssm_c_im, ssm_d, glu_w, glu_b,
           pool_w, pool_scale):
    bsz = x.shape[0]
    mod_all = _adaln(c, ada_w, ada_b).reshape(DEPTH, bsz, 9, D_MODEL)
    wg, wu, wd = ffn_w_gate.astype(BF16), ffn_w_up.astype(BF16), ffn_w_down.astype(BF16)
    w_in_b, w_out_b, glu_w_b = w_in.astype(BF16), w_out.astype(BF16), glu_w.astype(BF16)
    ln_g4, ln_b4 = ln_g.reshape(DEPTH, 3, 1, D_MODEL), ln_b.reshape(DEPTH, 3, 1, D_MODEL)
    glu_b3 = glu_b.reshape(DEPTH, 1, D_SSM)
    pool_scale3 = pool_scale.reshape(DEPTH, 1, D_POOL)
    for l in range(DEPTH):
        x = _ffn(x, mod_all, wg, wu, wd, ln_g4, ln_b4, l, 0)
        qkv, u_ssm, y_pool = _inproj(x, mod_all, w_in_b, _block_diag(pool_w[l]).astype(BF16), pool_scale3, l)
        att = _dilated_attention(qkv, rel_bias)
        tables = _ssm_tables(ssm_a_re[l], ssm_a_im[l], ssm_log_dt[l], ssm_b_re[l], ssm_b_im[l],
                             ssm_c_re[l], ssm_c_im[l], ssm_d[l])
        y_ssm = _ssm(u_ssm, tables)
        x = _outproj(x, mod_all, att, y_ssm, y_pool, glu_w_b, glu_b3, w_out_b, ln_g4, ln_b4, l)
        x = _ffn(x, mod_all, wg, wu, wd, ln_g4, ln_b4, l, 1)
    return x
```

```python
import functools
import math

import jax
import jax.numpy as jnp
import numpy as np
from jax import lax
from jax.experimental import pallas as pl
from jax.experimental.pallas import tpu as pltpu

F32 = jnp.float32
BF16 = jnp.bfloat16

D_MODEL = 1024
DEPTH = 2
HEAD_DIM = 64
N_HEADS = 8
D_ATT = N_HEADS * HEAD_DIM
DILATED_PATTERNS = ((128, 1), (512, 4), (2048, 16))
DILATIONS = tuple(d for _, d in DILATED_PATTERNS)
ATT_BLOCK = 128
SSM_GROUP = 16
D_SSM = 256
N_SSM_GROUPS = D_SSM // SSM_GROUP
SSM_STATE = 64
POOL_WINDOWS = (2, 4, 8, 16)
D_POOL = 256
POOL_GROUP = D_POOL // len(POOL_WINDOWS)
D_IN = 3 * D_ATT + D_SSM + D_POOL
D_FF = 2816
N_BUCKETS = 32
MAX_DISTANCE = 2048
ALPHA = (2 * DEPTH) ** 0.25
FFN_RES = 0.5
LN_EPS = 1e-5
NEG = -1e30

LANES = 128
TOKEN_TILE = 512
FF_CHUNK = 512
ADA_COL_TILE = 1152
SSM_CHUNK = 8
POOL_HALO = max(POOL_WINDOWS)
ATT_LOOKAHEAD = 8
VMEM_LIMIT = 56 * 1024 * 1024


def _sigmoid(x):
    return 1.0 / (1.0 + jnp.exp(-x))


def _layernorm(x):
    mu = jnp.mean(x, axis=-1, keepdims=True)
    xc = x - mu
    var = jnp.mean(xc * xc, axis=-1, keepdims=True)
    return xc * lax.rsqrt(var + LN_EPS)


def _resident(shape):
    zeros = (0,) * len(shape)
    return pl.BlockSpec(shape, lambda *_: zeros, pipeline_mode=pl.Buffered(1))


def _stacked(tail, *lead):
    idx = tuple(lead) + (0,) * len(tail)
    return pl.BlockSpec((None,) * len(lead) + tuple(tail), lambda *_: idx, pipeline_mode=pl.Buffered(1))


def _adaln_kernel(c_ref, w_ref, b_ref, o_ref):
    c = c_ref[...]
    cond = (c * _sigmoid(c)).astype(BF16)
    o_ref[...] = jnp.dot(cond, w_ref[...].astype(BF16), preferred_element_type=F32) + b_ref[...]


def _adaln(c, ada_w, ada_b):
    nl, d, n = ada_w.shape
    bsz = c.shape[0]
    return pl.pallas_call(
        _adaln_kernel,
        out_shape=jax.ShapeDtypeStruct((nl, bsz, n), F32),
        grid=(nl, n // ADA_COL_TILE),
        in_specs=[
            pl.BlockSpec((bsz, d), lambda l, j: (0, 0)),
            pl.BlockSpec((None, d, ADA_COL_TILE), lambda l, j: (l, 0, j)),
            pl.BlockSpec((None, 1, ADA_COL_TILE), lambda l, j: (l, 0, j)),
        ],
        out_specs=pl.BlockSpec((None, bsz, ADA_COL_TILE), lambda l, j: (l, 0, j)),
        compiler_params=pltpu.CompilerParams(vmem_limit_bytes=VMEM_LIMIT),
        name="adaln",
    )(c, ada_w, ada_b.reshape(nl, 1, n))


def _ff_chunks():
    chunks, c0 = [], 0
    while c0 < D_FF:
        cw = min(FF_CHUNK, D_FF - c0)
        chunks.append((c0, cw))
        c0 += cw
    return tuple(chunks)


def _ffn_kernel(x_ref, mod_ref, wg_ref, wu_ref, wd_ref, lg_ref, lb_ref, o_ref, acc_ref, *, sub):
    x = x_ref[...]
    shift = mod_ref[3 * sub + 0:3 * sub + 1, :]
    scale = mod_ref[3 * sub + 1:3 * sub + 2, :]
    gate = mod_ref[3 * sub + 2:3 * sub + 3, :]
    h = (_layernorm(x) * (1.0 + scale) + shift).astype(BF16)
    for idx, (c0, cw) in enumerate(_ff_chunks()):
        g = jnp.dot(h, wg_ref[:, c0:c0 + cw], preferred_element_type=F32)
        u = jnp.dot(h, wu_ref[:, c0:c0 + cw], preferred_element_type=F32)
        a = (g * _sigmoid(g) * u).astype(BF16)
        d = jnp.dot(a, wd_ref[c0:c0 + cw, :], preferred_element_type=F32)
        if idx == 0:
            acc_ref[...] = d
        else:
            acc_ref[...] += d
    y = ALPHA * x + (FFN_RES * gate) * acc_ref[...]
    o_ref[...] = _layernorm(y) * lg_ref[...] + lb_ref[...]


def _mod_spec(l, d):
    return pl.BlockSpec((None, None, 9, d), lambda b, i: (l, b, 0, 0))


def _ffn(x, mod_all, wg, wu, wd, ln_g, ln_b, l, which):
    bsz, s, d = x.shape
    tm = TOKEN_TILE
    sub = 2 * which
    return pl.pallas_call(
        functools.partial(_ffn_kernel, sub=sub),
        out_shape=jax.ShapeDtypeStruct(x.shape, F32),
        grid=(bsz, s // tm),
        in_specs=[
            pl.BlockSpec((None, tm, d), lambda b, i: (b, i, 0)),
            _mod_spec(l, d),
            _stacked(wg.shape[2:], l, which),
            _stacked(wu.shape[2:], l, which),
            _stacked(wd.shape[2:], l, which),
            _stacked((1, d), l, sub),
            _stacked((1, d), l, sub),
        ],
        out_specs=pl.BlockSpec((None, tm, d), lambda b, i: (b, i, 0)),
        scratch_shapes=[pltpu.VMEM((tm, d), F32)],
        compiler_params=pltpu.CompilerParams(vmem_limit_bytes=VMEM_LIMIT),
        name=f"ffn{sub}",
    )(x, mod_all, wg, wu, wd, ln_g, ln_b)


def _inproj_kernel(x_ref, mod_ref, w_ref, pw_ref, ps_ref, *rest):
    qkv_refs = rest[:3 * len(DILATIONS)]
    us_ref, yp_ref, ext_ref, zs_ref = rest[3 * len(DILATIONS):]
    i = pl.program_id(1)
    tm = x_ref.shape[0]
    x = x_ref[...]
    shift = mod_ref[3:4, :]
    scale = mod_ref[4:5, :]
    h = (_layernorm(x) * (1.0 + scale) + shift).astype(BF16)
    z = jnp.dot(h, w_ref[...], preferred_element_type=F32)
    up = z[:, 3 * D_ATT + D_SSM:]

    slabs_per = D_ATT // LANES
    n_slabs = (3 * D_ATT + D_SSM) // LANES
    for c in range(n_slabs):
        col = z[:, c * LANES:(c + 1) * LANES]
        zs_ref[c] = col * (HEAD_DIM ** -0.5) if c < slabs_per else col
    for c in range(3 * slabs_per, n_slabs):
        for j in range(SSM_CHUNK):
            lo = j * D_SSM + (c - 3 * slabs_per) * LANES
            us_ref[:, lo:lo + LANES] = zs_ref[c, pl.ds(j, tm // SSM_CHUNK, stride=SSM_CHUNK), :]
    for di, d in enumerate(DILATIONS):
        rows = tm // d
        for c in range(3 * slabs_per):
            dst = qkv_refs[3 * di + c // slabs_per]
            lanes = slice((c % slabs_per) * LANES, (c % slabs_per + 1) * LANES)
            for r in range(d):
                src = zs_ref[c] if d == 1 else zs_ref[c, pl.ds(r, rows, stride=d), :]
                dst[r, :, lanes] = src.astype(BF16)

    @pl.when(i == 0)
    def _():
        ext_ref[0:POOL_HALO, :] = jnp.zeros((POOL_HALO, D_POOL), F32)

    @pl.when(i > 0)
    def _():
        ext_ref[0:POOL_HALO, :] = ext_ref[tm:tm + POOL_HALO, :]

    ext_ref[POOL_HALO:POOL_HALO + tm, :] = up
    sums = {}
    acc = up
    for lag in range(1, POOL_HALO):
        acc = acc + ext_ref[POOL_HALO - lag:POOL_HALO - lag + tm, :]
        if lag + 1 in POOL_WINDOWS:
            sums[lag + 1] = acc
    pos = (i * tm + lax.broadcasted_iota(jnp.int32, (tm, 1), 0) + 1).astype(F32)
    group = lax.broadcasted_iota(jnp.int32, (1, D_POOL), 1) // POOL_GROUP
    mean = sums[POOL_WINDOWS[-1]] / jnp.minimum(pos, float(POOL_WINDOWS[-1]))
    for gi in range(len(POOL_WINDOWS) - 2, -1, -1):
        w = POOL_WINDOWS[gi]
        mean = jnp.where(group == gi, sums[w] / jnp.minimum(pos, float(w)), mean)
    pooled = (mean - up).astype(BF16)
    yp = jnp.dot(pooled, pw_ref[...], preferred_element_type=F32) * ps_ref[...]
    yp_ref[...] = yp.astype(BF16)


def _inproj(x, mod_all, w_in, pool_w_bd, pool_scale, l):
    bsz, s, d = x.shape
    tm = TOKEN_TILE
    tok = lambda width: pl.BlockSpec((None, tm, width), lambda b, i: (b, i, 0))
    qkv_shapes, qkv_specs = [], []
    for dil in DILATIONS:
        for _ in range(3):
            qkv_shapes.append(jax.ShapeDtypeStruct((bsz, dil, s // dil, D_ATT), BF16))
            qkv_specs.append(pl.BlockSpec((None, dil, tm // dil, D_ATT), lambda b, i: (b, 0, i, 0)))
    res = pl.pallas_call(
        _inproj_kernel,
        out_shape=tuple(qkv_shapes) + (
            jax.ShapeDtypeStruct((bsz, s // SSM_CHUNK, SSM_CHUNK * D_SSM), F32),
            jax.ShapeDtypeStruct((bsz, s, D_POOL), BF16),
        ),
        grid=(bsz, s // tm),
        in_specs=[
            tok(d),
            _mod_spec(l, d),
            _stacked(w_in.shape[1:], l),
            _resident(pool_w_bd.shape),
            _stacked((1, D_POOL), l),
        ],
        out_specs=tuple(qkv_specs) + (
            pl.BlockSpec((None, tm // SSM_CHUNK, SSM_CHUNK * D_SSM), lambda b, i: (b, i, 0)), tok(D_POOL)),
        scratch_shapes=[pltpu.VMEM((tm + POOL_HALO, D_POOL), F32),
                        pltpu.VMEM(((3 * D_ATT + D_SSM) // LANES, tm, LANES), F32)],
        compiler_params=pltpu.CompilerParams(
            dimension_semantics=("arbitrary", "arbitrary"), vmem_limit_bytes=VMEM_LIMIT),
        name="inproj",
    )(x, mod_all, w_in, pool_w_bd, pool_scale)
    qkv = [tuple(res[3 * di:3 * di + 3]) for di in range(len(DILATIONS))]
    return qkv, res[-2], res[-1]


def _t5_bucket(dist):
    max_exact = N_BUCKETS // 2
    dd = np.maximum(dist, 1).astype(np.float32)
    large = max_exact + (np.log(dd / max_exact) / math.log(MAX_DISTANCE / max_exact)
                         * (N_BUCKETS - max_exact)).astype(np.int32)
    large = np.minimum(large, N_BUCKETS - 1)
    return np.where(dist < max_exact, dist, large).astype(np.int32)


def _branch_bias(rel_bias, window, dilation, has_prev):
    qb = ATT_BLOCK
    n_keys = window // dilation
    assert n_keys == qb
    period = 3 * qb
    dist = np.arange(n_keys, -1, -1)
    row = rel_bias[_t5_bucket(dist * dilation)].T.astype(F32)
    row = jnp.concatenate([row, jnp.full((N_HEADS, period - n_keys - 1), NEG, F32)], axis=1)
    flat = jnp.tile(row, (1, qb))[:, :qb * (period - 1)]
    bias = flat.reshape(N_HEADS, qb, period - 1)[:, :, :2 * qb]
    return bias if has_prev else bias[:, :, qb:]


def _attn_kernel(*refs, has_prev):
    refs = list(refs)
    q_ref = refs.pop(0)
    kc_ref = refs.pop(0)
    kp_ref = refs.pop(0) if has_prev else None
    vc_ref = refs.pop(0)
    vp_ref = refs.pop(0) if has_prev else None
    bias_ref, o_ref, lse_ref = refs

    qb = ATT_BLOCK
    pair = 2 * HEAD_DIM
    lane = lax.broadcasted_iota(jnp.int32, (1, pair), 1)
    low = lane < HEAD_DIM
    if has_prev:
        first_block = pl.program_id(2) == 0
        kcol = lax.broadcasted_iota(jnp.int32, (1, 2 * qb), 1)
        dead = jnp.logical_and(first_block, kcol < qb)
    lse_lane = lax.broadcasted_iota(jnp.int32, (1, LANES), 1)
    lse_tile = jnp.zeros((qb, LANES), F32)

    def pair_cols(h):
        return slice((h // 2) * pair, (h // 2 + 1) * pair)

    def scores(h):
        cols = pair_cols(h)
        q2 = q_ref[:, cols]
        k2 = jnp.concatenate([kp_ref[:, cols], kc_ref[:, cols]], axis=0) if has_prev else kc_ref[:, cols]
        qh = jnp.where(low, q2, jnp.zeros_like(q2)) if h % 2 == 0 else jnp.where(low, jnp.zeros_like(q2), q2)
        s = lax.dot_general(qh, k2, (((1,), (1,)), ((), ())), preferred_element_type=F32)
        s = s + bias_ref[h]
        return jnp.where(dead, NEG, s) if has_prev else s

    pending = {h: scores(h) for h in range(min(ATT_LOOKAHEAD, N_HEADS))}
    outs = {}
    for h in range(N_HEADS):
        s = pending.pop(h)
        m = jnp.max(s, axis=-1, keepdims=True)
        p = jnp.exp(s - m)
        l = jnp.sum(p, axis=-1, keepdims=True)
        if h + ATT_LOOKAHEAD < N_HEADS:
            pending[h + ATT_LOOKAHEAD] = scores(h + ATT_LOOKAHEAD)
        cols = pair_cols(h)
        v2 = jnp.concatenate([vp_ref[:, cols], vc_ref[:, cols]], axis=0) if has_prev else vc_ref[:, cols]
        pv = jnp.dot(p.astype(BF16), v2, preferred_element_type=F32)
        outs[h] = pv * (1.0 / l)
        lse_tile = jnp.where(lse_lane == h, m + jnp.log(l), lse_tile)
        if h % 2 == 1:
            o_ref[:, cols] = jnp.where(low, outs.pop(h - 1), outs.pop(h)).astype(BF16)
    lse_ref[...] = lse_tile


def _attn_branch(q, k, v, bias):
    bsz, d, ln, _ = q.shape
    nb = ln // ATT_BLOCK
    has_prev = nb > 1
    cur = lambda width: pl.BlockSpec((None, None, ATT_BLOCK, width), lambda b, r, j: (b, r, j, 0))
    prev = lambda width: pl.BlockSpec((None, None, ATT_BLOCK, width),
                                      lambda b, r, j: (b, r, jnp.maximum(j - 1, 0), 0))
    args, specs = [q, k], [cur(D_ATT), cur(D_ATT)]
    if has_prev:
        args.append(k)
        specs.append(prev(D_ATT))
    args.append(v)
    specs.append(cur(D_ATT))
    if has_prev:
        args.append(v)
        specs.append(prev(D_ATT))
    args.append(bias)
    specs.append(_resident(bias.shape))
    return pl.pallas_call(
        functools.partial(_attn_kernel, has_prev=has_prev),
        out_shape=(jax.ShapeDtypeStruct((bsz, d, ln, D_ATT), BF16),
                   jax.ShapeDtypeStruct((bsz, d, ln, LANES), F32)),
        grid=(bsz, d, nb),
        in_specs=specs,
        out_specs=(cur(D_ATT), cur(LANES)),
        compiler_params=pltpu.CompilerParams(vmem_limit_bytes=VMEM_LIMIT),
        name=f"attn_d{d}",
    )(*args)


def _dilated_attention(qkv, rel_bias):
    outs = []
    for (window, dilation), (q, k, v) in zip(DILATED_PATTERNS, qkv):
        has_prev = q.shape[2] // ATT_BLOCK > 1
        outs.append(_attn_branch(q, k, v, _branch_bias(rel_bias, window, dilation, has_prev)))
    return outs


def _ssm_tables(a_re, a_im, log_dt, b_re, b_im, c_re, c_im, d_skip):
    hi = lax.Precision.HIGHEST
    t, g, p, c = SSM_CHUNK, N_SSM_GROUPS, SSM_STATE, SSM_GROUP
    dt = jnp.exp(log_dt)[:, None]
    mag = jnp.exp(a_re * dt)
    ar, ai = mag * jnp.cos(a_im * dt), mag * jnp.sin(a_im * dt)
    den = a_re * a_re + a_im * a_im
    fr = ((ar - 1.0) * a_re + ai * a_im) / den
    fi = (ai * a_re - (ar - 1.0) * a_im) / den
    bbr = fr[:, :, None] * b_re - fi[:, :, None] * b_im
    bbi = fr[:, :, None] * b_im + fi[:, :, None] * b_re
    pr, pi_ = [jnp.ones_like(ar)], [jnp.zeros_like(ar)]
    for _ in range(t):
        pr.append(pr[-1] * ar - pi_[-1] * ai)
        pi_.append(pr[-2] * ai + pi_[-1] * ar)
    n = t * g * c
    assert n == 2 * g * p and t * c == 2 * p

    def expand(dense, row_div, col_div, spread):
        full = jnp.dot(dense.astype(BF16), spread, preferred_element_type=F32)
        rg = (lax.broadcasted_iota(jnp.int32, (n, n), 0) // row_div) % g
        cg = (lax.broadcasted_iota(jnp.int32, (n, n), 1) // col_div) % g
        return jnp.where(rg == cg, full, 0.0).astype(BF16)

    kk = np.arange(t * c)[:, None]
    cc = np.arange(n)[None, :]
    spread_rp = jnp.asarray((cc // (g * p) == kk // p) & (cc % p == kk % p), BF16)
    spread_jc = jnp.asarray((cc // (g * c) == kk // c) & (cc % c == kk % c), BF16)

    win = []
    for j in range(t):
        qr, qi = pr[t - 1 - j][:, :, None], pi_[t - 1 - j][:, :, None]
        win.append(jnp.stack([qr * bbr - qi * bbi, qr * bbi + qi * bbr], 0))
    win = jnp.stack(win, 0)
    w_in = expand(jnp.transpose(win, (0, 2, 4, 1, 3)).reshape(n, 2 * p), c, p, spread_rp)

    wout, hs = [], []
    for j in range(t + 1):
        wr = c_re * pr[j][:, None, :] - c_im * pi_[j][:, None, :]
        wi = c_re * pi_[j][:, None, :] + c_im * pr[j][:, None, :]
        if j >= 1:
            wout.append(jnp.stack([wr, -wi], 0))
        if j < t:
            hs.append(jnp.einsum('gcp,gpd->gdc', wr, bbr, precision=hi)
                      - jnp.einsum('gcp,gpd->gdc', wi, bbi, precision=hi))
    wout = jnp.stack(wout, 0)
    w_out = expand(jnp.transpose(wout, (1, 2, 4, 0, 3)).reshape(n, t * c), p, c, spread_jc)

    zero_h = jnp.zeros_like(hs[0])
    toep = jnp.stack([jnp.stack([hs[j - jp] if j >= jp else zero_h for j in range(t)], 0)
                      for jp in range(t)], 0)
    w_intra = expand(jnp.transpose(toep, (0, 2, 3, 1, 4)).reshape(n, t * c), c, c, spread_jc)

    abar_t = jnp.concatenate([pr[t].reshape(1, g * p), pi_[t].reshape(1, g * p)], axis=1)
    d_row = jnp.tile(d_skip.reshape(1, D_SSM), (1, t))
    return w_in.astype(BF16), w_intra.astype(BF16), w_out.astype(BF16), abar_t, d_row


def _ssm_kernel(u_ref, win_ref, wintra_ref, wout_ref, abar_ref, d_ref, y_ref, z_ref, xp_ref):
    n = u_ref.shape[0]
    half = N_SSM_GROUPS * SSM_STATE
    u = u_ref[...]
    ub = u.astype(BF16)
    z_ref[...] = jnp.dot(ub, win_ref[...], preferred_element_type=F32)
    ar = abar_ref[:, 0:half]
    ai = abar_ref[:, half:]

    def step(k, carry):
        xr, xi = carry
        xp_ref[pl.ds(k, 1), 0:half] = xr
        xp_ref[pl.ds(k, 1), half:] = xi
        zr = z_ref[pl.ds(k, 1), 0:half]
        zi = z_ref[pl.ds(k, 1), half:]
        return ar * xr - ai * xi + zr, ar * xi + ai * xr + zi

    zero = jnp.zeros((1, half), F32)
    lax.fori_loop(0, n, step, (zero, zero))
    y = jnp.dot(ub, wintra_ref[...], preferred_element_type=F32)
    y = y + jnp.dot(xp_ref[...].astype(BF16), wout_ref[...], preferred_element_type=F32)
    y_ref[...] = y + d_ref[...] * u


def _ssm(u, tables):
    w_in, w_intra, w_out, abar_t, d_row = tables
    bsz, n, width = u.shape
    row = pl.BlockSpec((None, n, width), lambda b: (b, 0, 0))
    return pl.pallas_call(
        _ssm_kernel,
        out_shape=jax.ShapeDtypeStruct((bsz, n, width), F32),
        grid=(bsz,),
        in_specs=[row, _resident(w_in.shape), _resident(w_intra.shape), _resident(w_out.shape),
                  _resident(abar_t.shape), _resident(d_row.shape)],
        out_specs=row,
        scratch_shapes=[pltpu.VMEM((n, w_in.shape[1]), F32), pltpu.VMEM((n, w_in.shape[1]), F32)],
        compiler_params=pltpu.CompilerParams(vmem_limit_bytes=VMEM_LIMIT),
        name="ssm",
    )(u, w_in, w_intra, w_out, abar_t, d_row)


def _outproj_kernel(x_ref, mod_ref, *rest):
    nd = len(DILATIONS)
    o_refs, l_refs = rest[0:nd], rest[nd:2 * nd]
    (ys_ref, yp_ref, gw_ref, gb_ref, wa_ref, ws_ref, wp_ref, lg_ref, lb_ref, o_ref) = rest[2 * nd:2 * nd + 10]
    on_refs, ln_refs, ysn_ref = rest[2 * nd + 10:3 * nd + 9], rest[3 * nd + 9:4 * nd + 8], rest[4 * nd + 8]
    tm = x_ref.shape[0]
    slabs = D_ATT // LANES

    for j in range(SSM_CHUNK):
        for c in range(D_SSM // LANES):
            lo = j * D_SSM + c * LANES
            ysn_ref[c, pl.ds(j, tm // SSM_CHUNK, stride=SSM_CHUNK), :] = ys_ref[:, lo:lo + LANES]

    o_nat, l_nat = [o_refs[0][0].astype(F32)], [l_refs[0][0]]
    for di in range(1, nd):
        d = DILATIONS[di]
        rows = tm // d
        on, ln = on_refs[di - 1], ln_refs[di - 1]
        for r in range(d):
            ln[pl.ds(r, rows, stride=d), :] = l_refs[di][r]
            for c in range(slabs):
                on[c, pl.ds(r, rows, stride=d), :] = o_refs[di][r, :, c * LANES:(c + 1) * LANES].astype(F32)
        o_nat.append(jnp.concatenate([on[c] for c in range(slabs)], axis=1))
        l_nat.append(ln[...])

    mx = functools.reduce(jnp.maximum, l_nat)
    es = [jnp.exp(lv - mx) for lv in l_nat]
    inv = 1.0 / functools.reduce(lambda a, b: a + b, es)
    head = lax.broadcasted_iota(jnp.int32, (LANES, D_ATT), 1) // HEAD_DIM
    spread = (head == lax.broadcasted_iota(jnp.int32, (LANES, D_ATT), 0)).astype(BF16)
    ya = jnp.zeros((tm, D_ATT), F32)
    for e, ov in zip(es, o_nat):
        w = e * inv
        w_hi = w.astype(BF16)
        w_lo = (w - w_hi.astype(F32)).astype(BF16)
        wx = (jnp.dot(w_hi, spread, preferred_element_type=F32)
              + jnp.dot(w_lo, spread, preferred_element_type=F32))
        ya = ya + wx * ov

    x = x_ref[...]
    gate = mod_ref[5:6, :]
    ys = jnp.concatenate([ysn_ref[c] for c in range(D_SSM // LANES)], axis=1)
    cdf = 0.5 * (1.0 + jnp.tanh(math.sqrt(2.0 / math.pi) * (ys + 0.044715 * (ys * ys * ys))))
    t = jnp.dot((ys * cdf).astype(BF16), gw_ref[...], preferred_element_type=F32) + gb_ref[...]
    yg = (ys * _sigmoid(t)).astype(BF16)
    y = jnp.dot(ya.astype(BF16), wa_ref[...], preferred_element_type=F32)
    y = y + jnp.dot(yg, ws_ref[...], preferred_element_type=F32)
    y = y + jnp.dot(yp_ref[...], wp_ref[...], preferred_element_type=F32)
    r = ALPHA * x + gate * y
    o_ref[...] = _layernorm(r) * lg_ref[...] + lb_ref[...]


def _outproj(x, mod_all, att, y_ssm, y_pool, glu_w, glu_b, w_out, ln_g, ln_b, l):
    bsz, s, d = x.shape
    tm = TOKEN_TILE
    tok = lambda width: pl.BlockSpec((None, tm, width), lambda b, i: (b, i, 0))
    res = lambda dil, width: pl.BlockSpec((None, dil, tm // dil, width), lambda b, i: (b, 0, i, 0))
    rows_of = lambda r0, n: pl.BlockSpec((None, n, d), lambda *_: (l, r0 // n, 0), pipeline_mode=pl.Buffered(1))
    assert D_ATT % D_SSM == 0 and D_SSM == D_POOL
    scratch = []
    for dil in DILATIONS[1:]:
        scratch.append(pltpu.VMEM((D_ATT // LANES, tm, LANES), F32))
    for dil in DILATIONS[1:]:
        scratch.append(pltpu.VMEM((tm, LANES), F32))
    scratch.append(pltpu.VMEM((D_SSM // LANES, tm, LANES), F32))
    return pl.pallas_call(
        _outproj_kernel,
        out_shape=jax.ShapeDtypeStruct(x.shape, F32),
        grid=(bsz, s // tm),
        in_specs=[tok(d), _mod_spec(l, d)]
        + [res(dil, D_ATT) for dil in DILATIONS]
        + [res(dil, LANES) for dil in DILATIONS]
        + [pl.BlockSpec((None, tm // SSM_CHUNK, SSM_CHUNK * D_SSM), lambda b, i: (b, i, 0)), tok(D_POOL),
           _stacked(glu_w.shape[1:], l), _stacked((1, D_SSM), l),
           rows_of(0, D_ATT), rows_of(D_ATT, D_SSM), rows_of(D_ATT + D_SSM, D_POOL),
           _stacked((1, d), l, 1), _stacked((1, d), l, 1)],
        out_specs=tok(d),
        scratch_shapes=scratch,
        compiler_params=pltpu.CompilerParams(vmem_limit_bytes=VMEM_LIMIT),
        name="outproj",
    )(x, mod_all, *[o for o, _ in att], *[lse for _, lse in att], y_ssm, y_pool,
      glu_w, glu_b, w_out, w_out, w_out, ln_g, ln_b)


def _block_diag(w):
    g, n, _ = w.shape
    return jnp.einsum('gab,gh->gahb', w, jnp.eye(g, dtype=w.dtype)).reshape(g * n, g * n)


def kernel(x, c, rel_bias, ada_w, ada_b, ln_g, ln_b, ffn_w_gate, ffn_w_up, ffn_w_down, w_in, w_out,
           ssm_a_re, ssm_a_im, ssm_log_dt, ssm_b_re, ssm_b_im, ssm_c_re, ssm_c_im, ssm_d, glu_w, glu_b,
           pool_w, pool_scale):
    bsz = x.shape[0]
    mod_all = _adaln(c, ada_w, ada_b).reshape(DEPTH, bsz, 9, D_MODEL)
    wg, wu, wd = ffn_w_gate.astype(BF16), ffn_w_up.astype(BF16), ffn_w_down.astype(BF16)
    w_in_b, w_out_b, glu_w_b = w_in.astype(BF16), w_out.astype(BF16), glu_w.astype(BF16)
    ln_g4, ln_b4 = ln_g.reshape(DEPTH, 3, 1, D_MODEL), ln_b.reshape(DEPTH, 3, 1, D_MODEL)
    glu_b3 = glu_b.reshape(DEPTH, 1, D_SSM)
    pool_scale3 = pool_scale.reshape(DEPTH, 1, D_POOL)
    for l in range(DEPTH):
        x = _ffn(x, mod_all, wg, wu, wd, ln_g4, ln_b4, l, 0)
        qkv, u_ssm, y_pool = _inproj(x, mod_all, w_in_b, _block_diag(pool_w[l]).astype(BF16), pool_scale3, l)
        att = _dilated_attention(qkv, rel_bias)
        tables = _ssm_tables(ssm_a_re[l], ssm_a_im[l], ssm_log_dt[l], ssm_b_re[l], ssm_b_im[l],
                             ssm_c_re[l], ssm_c_im[l], ssm_d[l])
        y_ssm = _ssm(u_ssm, tables)
        x = _outproj(x, mod_all, att, y_ssm, y_pool, glu_w_b, glu_b3, w_out_b, ln_g4, ln_b4, l)
        x = _ffn(x, mod_all, wg, wu, wd, ln_g4, ln_b4, l, 1)
    return x
```

```python
import functools
import math

import jax
import jax.numpy as jnp
import numpy as np
from jax import lax
from jax.experimental import pallas as pl
from jax.experimental.pallas import tpu as pltpu

F32 = jnp.float32
BF16 = jnp.bfloat16

D_MODEL = 1024
DEPTH = 2
HEAD_DIM = 64
N_HEADS = 8
D_ATT = N_HEADS * HEAD_DIM
DILATED_PATTERNS = ((128, 1), (512, 4), (2048, 16))
DILATIONS = tuple(d for _, d in DILATED_PATTERNS)
ATT_BLOCK = 128
SSM_GROUP = 16
D_SSM = 256
N_SSM_GROUPS = D_SSM // SSM_GROUP
SSM_STATE = 64
POOL_WINDOWS = (2, 4, 8, 16)
D_POOL = 256
POOL_GROUP = D_POOL // len(POOL_WINDOWS)
D_IN = 3 * D_ATT + D_SSM + D_POOL
D_FF = 2816
N_BUCKETS = 32
MAX_DISTANCE = 2048
ALPHA = (2 * DEPTH) ** 0.25
FFN_RES = 0.5
LN_EPS = 1e-5
NEG = -1e30

LANES = 128
TOKEN_TILE = 512
FF_CHUNK = 512
FFN_TOKEN_TILE = 1024
FFN_ROW_SPLIT = 2
FFN_SIDE_PIECES = 4
ADA_COL_TILE = 1152
SSM_CHUNK = 8
POOL_HALO = 2 * max(POOL_WINDOWS)
ATT_LOOKAHEAD = 8
ATT_STEP_BLOCKS = 4
MERGE_ROWS = 128
LOG2E = math.log2(math.e)
LN2 = math.log(2.0)
VMEM_LIMIT = 56 * 1024 * 1024


def _sigmoid(x):
    return 1.0 / (1.0 + jnp.exp(-x))


def _layernorm(x):
    mu = jnp.mean(x, axis=-1, keepdims=True)
    xc = x - mu
    var = jnp.mean(xc * xc, axis=-1, keepdims=True)
    return xc * lax.rsqrt(var + LN_EPS)


def _resident(shape):
    zeros = (0,) * len(shape)
    return pl.BlockSpec(shape, lambda *_: zeros, pipeline_mode=pl.Buffered(1))


def _stacked(tail, *lead):
    idx = tuple(lead) + (0,) * len(tail)
    return pl.BlockSpec((None,) * len(lead) + tuple(tail), lambda *_: idx, pipeline_mode=pl.Buffered(1))


def _adaln_kernel(c_ref, w_ref, b_ref, o_ref):
    c = c_ref[...]
    cond = (c * _sigmoid(c)).astype(BF16)
    o_ref[...] = jnp.dot(cond, w_ref[...].astype(BF16), preferred_element_type=F32) + b_ref[...]


def _adaln(c, ada_w, ada_b):
    nl, d, n = ada_w.shape
    bsz = c.shape[0]
    return pl.pallas_call(
        _adaln_kernel,
        out_shape=jax.ShapeDtypeStruct((nl, bsz, n), F32),
        grid=(nl, n // ADA_COL_TILE),
        in_specs=[
            pl.BlockSpec((bsz, d), lambda l, j: (0, 0)),
            pl.BlockSpec((None, d, ADA_COL_TILE), lambda l, j: (l, 0, j)),
            pl.BlockSpec((None, 1, ADA_COL_TILE), lambda l, j: (l, 0, j)),
        ],
        out_specs=pl.BlockSpec((None, bsz, ADA_COL_TILE), lambda l, j: (l, 0, j)),
        compiler_params=pltpu.CompilerParams(vmem_limit_bytes=VMEM_LIMIT),
        name="adaln",
    )(c, ada_w, ada_b.reshape(nl, 1, n))


def _ff_chunks():
    chunks, c0 = [], 0
    while c0 < D_FF:
        cw = min(FF_CHUNK, D_FF - c0)
        chunks.append((c0, cw))
        c0 += cw
    return tuple(chunks)


def _ffn_kernel(x_ref, mod_ref, wg_ref, wu_ref, wd_ref, lg_ref, lb_ref, o_ref, acc_ref, h_ref, *, sub):
    shift = mod_ref[3 * sub + 0:3 * sub + 1, :]
    scale = mod_ref[3 * sub + 1:3 * sub + 2, :]
    gate = mod_ref[3 * sub + 2:3 * sub + 3, :]
    rows = x_ref.shape[0] // FFN_ROW_SPLIT
    piece = rows // FFN_SIDE_PIECES

    def prologue(r0, n):
        h_ref[r0:r0 + n, :] = (_layernorm(x_ref[r0:r0 + n, :]) * (1.0 + scale) + shift).astype(BF16)

    def epilogue(r0, n):
        y = ALPHA * x_ref[r0:r0 + n, :] + (FFN_RES * gate) * acc_ref[r0:r0 + n, :]
        o_ref[r0:r0 + n, :] = _layernorm(y) * lg_ref[...] + lb_ref[...]

    chunks = _ff_chunks()
    assert len(chunks) >= FFN_SIDE_PIECES
    prologue(0, rows)
    for k in range(FFN_ROW_SPLIT):
        p0 = k * rows
        for idx, (c0, cw) in enumerate(chunks):
            h = h_ref[p0:p0 + rows, :]
            g = jnp.dot(h, wg_ref[:, c0:c0 + cw], preferred_element_type=F32)
            u = jnp.dot(h, wu_ref[:, c0:c0 + cw], preferred_element_type=F32)
            a = (g * _sigmoid(g) * u).astype(BF16)
            d = jnp.dot(a, wd_ref[c0:c0 + cw, :], preferred_element_type=F32)
            if idx == 0:
                acc_ref[p0:p0 + rows, :] = d
            else:
                acc_ref[p0:p0 + rows, :] += d
            if idx < FFN_SIDE_PIECES:
                if k + 1 < FFN_ROW_SPLIT:
                    prologue(p0 + rows + idx * piece, piece)
                if k >= 1:
                    epilogue(p0 - rows + idx * piece, piece)
    epilogue((FFN_ROW_SPLIT - 1) * rows, rows)


def _mod_spec(l, d):
    return pl.BlockSpec((None, None, 9, d), lambda b, i: (l, b, 0, 0))


def _ffn(x, mod_all, wg, wu, wd, ln_g, ln_b, l, which):
    bsz, s, d = x.shape
    tm = FFN_TOKEN_TILE
    sub = 2 * which
    return pl.pallas_call(
        functools.partial(_ffn_kernel, sub=sub),
        out_shape=jax.ShapeDtypeStruct(x.shape, F32),
        grid=(bsz, s // tm),
        in_specs=[
            pl.BlockSpec((None, tm, d), lambda b, i: (b, i, 0)),
            _mod_spec(l, d),
            _stacked(wg.shape[2:], l, which),
            _stacked(wu.shape[2:], l, which),
            _stacked(wd.shape[2:], l, which),
            _stacked((1, d), l, sub),
            _stacked((1, d), l, sub),
        ],
        out_specs=pl.BlockSpec((None, tm, d), lambda b, i: (b, i, 0)),
        scratch_shapes=[pltpu.VMEM((tm, d), F32), pltpu.VMEM((tm, d), BF16)],
        compiler_params=pltpu.CompilerParams(vmem_limit_bytes=VMEM_LIMIT),
        name=f"ffn{sub}",
    )(x, mod_all, wg, wu, wd, ln_g, ln_b)


def _inproj_kernel(x_ref, mod_ref, w_ref, pw_ref, ps_ref, *rest):
    qkv_refs = rest[:3 * len(DILATIONS)]
    us_ref, yp_ref, zs_ref, z4_ref = rest[3 * len(DILATIONS):3 * len(DILATIONS) + 4]
    e_refs = rest[3 * len(DILATIONS) + 4:]
    i = pl.program_id(1)
    tm = x_ref.shape[0]
    x = x_ref[...]
    shift = mod_ref[3:4, :]
    scale = mod_ref[4:5, :]
    h = (_layernorm(x) * (1.0 + scale) + shift).astype(BF16)
    z = jnp.dot(h, w_ref[...], preferred_element_type=F32)
    up = z[:, 3 * D_ATT + D_SSM:]

    slabs_per = D_ATT // LANES
    n_qkv = 3 * slabs_per
    n_slabs = (3 * D_ATT + D_SSM) // LANES
    for c in range(n_slabs):
        col = z[:, c * LANES:(c + 1) * LANES]
        zs_ref[c] = col * (HEAD_DIM ** -0.5 * LOG2E) if c < slabs_per else col
    assert DILATIONS == (1, 4, 16)
    q4rows, q16rows = tm // 4, tm // 16
    for c in range(n_qkv):
        which, lanes = c // slabs_per, slice((c % slabs_per) * LANES, (c % slabs_per + 1) * LANES)
        qkv_refs[which][0, :, lanes] = zs_ref[c].astype(BF16)
        for r4 in range(4):
            blk = zs_ref[c, pl.ds(r4, q4rows, stride=4), :]
            z4_ref[c, r4 * q4rows:(r4 + 1) * q4rows, :] = blk
            qkv_refs[3 + which][r4, :, lanes] = blk.astype(BF16)
        for r4 in range(4):
            for c4 in range(4):
                blk = z4_ref[c, pl.ds(r4 * q4rows + c4, q16rows, stride=4), :]
                qkv_refs[6 + which][r4 + 4 * c4, :, lanes] = blk.astype(BF16)
    for c in range(n_qkv, n_slabs):
        for j in range(SSM_CHUNK):
            lo = j * D_SSM + (c - n_qkv) * LANES
            us_ref[:, lo:lo + LANES] = zs_ref[c, pl.ds(j, tm // SSM_CHUNK, stride=SSM_CHUNK), :]

    e1, e2, e4, e8 = e_refs
    hl = POOL_HALO
    assert POOL_WINDOWS == (2, 4, 8, 16) and hl == 32

    @pl.when(i == 0)
    def _():
        e1[0:hl, :] = jnp.zeros((hl, D_POOL), F32)

    @pl.when(i > 0)
    def _():
        e1[0:hl, :] = e1[tm:tm + hl, :]

    e1[hl:hl + tm, :] = up
    e2[8:, :] = e1[8:, :] + e1[7:tm + hl - 1, :]
    e4[16:, :] = e2[16:, :] + e2[14:tm + hl - 2, :]
    e8[24:, :] = e4[24:, :] + e4[20:tm + hl - 4, :]
    sums = {2: e2[hl:, :], 4: e4[hl:, :], 8: e8[hl:, :], 16: e8[hl:, :] + e8[hl - 8:tm + hl - 8, :]}
    pos = (i * tm + lax.broadcasted_iota(jnp.int32, (tm, 1), 0) + 1).astype(F32)
    group = lax.broadcasted_iota(jnp.int32, (1, D_POOL), 1) // POOL_GROUP
    mean = sums[POOL_WINDOWS[-1]] / jnp.minimum(pos, float(POOL_WINDOWS[-1]))
    for gi in range(len(POOL_WINDOWS) - 2, -1, -1):
        w = POOL_WINDOWS[gi]
        mean = jnp.where(group == gi, sums[w] / jnp.minimum(pos, float(w)), mean)
    pooled = (mean - up).astype(BF16)
    yp = jnp.dot(pooled, pw_ref[...], preferred_element_type=F32) * ps_ref[...]
    yp_ref[...] = yp.astype(BF16)


def _inproj(x, mod_all, w_in, pool_w_bd, pool_scale, l):
    bsz, s, d = x.shape
    tm = TOKEN_TILE
    tok = lambda width: pl.BlockSpec((None, tm, width), lambda b, i: (b, i, 0))
    qkv_shapes, qkv_specs = [], []
    for dil in DILATIONS:
        for _ in range(3):
            qkv_shapes.append(jax.ShapeDtypeStruct((bsz, dil, s // dil, D_ATT), BF16))
            qkv_specs.append(pl.BlockSpec((None, dil, tm // dil, D_ATT), lambda b, i: (b, 0, i, 0)))
    res = pl.pallas_call(
        _inproj_kernel,
        out_shape=tuple(qkv_shapes) + (
            jax.ShapeDtypeStruct((bsz, s // SSM_CHUNK, SSM_CHUNK * D_SSM), F32),
            jax.ShapeDtypeStruct((bsz, s, D_POOL), BF16),
        ),
        grid=(bsz, s // tm),
        in_specs=[
            tok(d),
            _mod_spec(l, d),
            _stacked(w_in.shape[1:], l),
            _resident(pool_w_bd.shape),
            _stacked((1, D_POOL), l),
        ],
        out_specs=tuple(qkv_specs) + (
            pl.BlockSpec((None, tm // SSM_CHUNK, SSM_CHUNK * D_SSM), lambda b, i: (b, i, 0)), tok(D_POOL)),
        scratch_shapes=[pltpu.VMEM(((3 * D_ATT + D_SSM) // LANES, tm, LANES), F32),
                        pltpu.VMEM((3 * D_ATT // LANES, tm, LANES), F32)]
        + [pltpu.VMEM((tm + POOL_HALO, D_POOL), F32) for _ in range(4)],
        compiler_params=pltpu.CompilerParams(
            dimension_semantics=("arbitrary", "arbitrary"), vmem_limit_bytes=VMEM_LIMIT),
        name="inproj",
    )(x, mod_all, w_in, pool_w_bd, pool_scale)
    qkv = [tuple(res[3 * di:3 * di + 3]) for di in range(len(DILATIONS))]
    return qkv, res[-2], res[-1]


def _t5_bucket(dist):
    max_exact = N_BUCKETS // 2
    dd = np.maximum(dist, 1).astype(np.float32)
    large = max_exact + (np.log(dd / max_exact) / math.log(MAX_DISTANCE / max_exact)
                         * (N_BUCKETS - max_exact)).astype(np.int32)
    large = np.minimum(large, N_BUCKETS - 1)
    return np.where(dist < max_exact, dist, large).astype(np.int32)


def _branch_bias(rel_bias, window, dilation, has_prev):
    qb = ATT_BLOCK
    n_keys = window // dilation
    assert n_keys == qb
    period = 3 * qb
    dist = np.arange(n_keys, -1, -1)
    row = rel_bias[_t5_bucket(dist * dilation)].T.astype(F32) * LOG2E
    row = jnp.concatenate([row, jnp.full((N_HEADS, period - n_keys - 1), NEG, F32)], axis=1)
    flat = jnp.tile(row, (1, qb))[:, :qb * (period - 1)]
    bias = flat.reshape(N_HEADS, qb, period - 1)[:, :, :2 * qb]
    if not has_prev:
        return bias[:, :, qb:]
    first = jnp.concatenate([jnp.full((N_HEADS, qb, qb), NEG, F32), bias[:, :, qb:]], axis=2)
    return jnp.stack([bias, first], 0)


def _attn_kernel(*refs, has_prev):
    qb = ATT_BLOCK
    if has_prev:
        q_ref, kc_ref, kh_ref, vc_ref, vh_ref, bias_ref, o_ref, lse_ref, kbuf, vbuf = refs
        nsub = q_ref.shape[0] // qb
        kbuf[0:qb, :] = kh_ref[...]
        kbuf[qb:, :] = kc_ref[...]
        vbuf[0:qb, :] = vh_ref[...]
        vbuf[qb:, :] = vc_ref[...]
    else:
        q_ref, kc_ref, vc_ref, bias_ref, o_ref, lse_ref = refs
        nsub = q_ref.shape[0]

    pair = 2 * HEAD_DIM
    lane = lax.broadcasted_iota(jnp.int32, (1, pair), 1)
    low = lane < HEAD_DIM
    lse_lane = lax.broadcasted_iota(jnp.int32, (1, LANES), 1)

    def block(jj, carry):
        if has_prev:
            row0 = pl.multiple_of(jj * qb, qb)
            first = jnp.logical_and(pl.program_id(2) == 0, jj == 0).astype(jnp.int32)
            q_at = lambda cols: q_ref[pl.ds(row0, qb), cols]
            k_at = lambda cols: kbuf[pl.ds(row0, 2 * qb), cols]
            v_at = lambda cols: vbuf[pl.ds(row0, 2 * qb), cols]
            bias_at = lambda h: bias_ref[first, h]
        else:
            q_at = lambda cols: q_ref[jj, :, cols]
            k_at = lambda cols: kc_ref[jj, :, cols]
            v_at = lambda cols: vc_ref[jj, :, cols]
            bias_at = lambda h: bias_ref[h]

        def pair_cols(h):
            return slice((h // 2) * pair, (h // 2 + 1) * pair)

        def scores(h):
            q2 = q_at(pair_cols(h))
            qh = jnp.where(low, q2, jnp.zeros_like(q2)) if h % 2 == 0 else jnp.where(low, jnp.zeros_like(q2), q2)
            s = lax.dot_general(qh, k_at(pair_cols(h)), (((1,), (1,)), ((), ())), preferred_element_type=F32)
            return s + bias_at(h)

        pending = {h: scores(h) for h in range(min(ATT_LOOKAHEAD, N_HEADS))}
        outs = {}
        lse_tile = jnp.zeros((qb, LANES), F32)
        for h in range(N_HEADS):
            s = pending.pop(h)
            m = jnp.max(s, axis=-1, keepdims=True)
            p = jnp.exp2(s - m)
            l = jnp.sum(p, axis=-1, keepdims=True)
            if h + ATT_LOOKAHEAD < N_HEADS:
                pending[h + ATT_LOOKAHEAD] = scores(h + ATT_LOOKAHEAD)
            cols = pair_cols(h)
            pv = jnp.dot(p.astype(BF16), v_at(cols), preferred_element_type=F32)
            outs[h] = pv * (1.0 / l)
            lse_tile = jnp.where(lse_lane == h, (m + jnp.log2(l)) * LN2, lse_tile)
            if h % 2 == 1:
                o2 = jnp.where(low, outs.pop(h - 1), outs.pop(h)).astype(BF16)
                if has_prev:
                    o_ref[pl.ds(row0, qb), cols] = o2
                else:
                    o_ref[jj, :, cols] = o2
        if has_prev:
            lse_ref[pl.ds(row0, qb), :] = lse_tile
        else:
            lse_ref[jj] = lse_tile
        return carry

    lax.fori_loop(0, nsub, block, 0)


def _attn_branch(q, k, v, bias):
    bsz, d, ln, _ = q.shape
    qb = ATT_BLOCK
    has_prev = ln > qb
    out_shape = (jax.ShapeDtypeStruct((bsz, d, ln, D_ATT), BF16),
                 jax.ShapeDtypeStruct((bsz, d, ln, LANES), F32))
    if has_prev:
        rows = min(ATT_STEP_BLOCKS * qb, ln)
        per = rows // qb
        cur = lambda width: pl.BlockSpec((None, None, rows, width), lambda b, r, j: (b, r, j, 0))
        halo = pl.BlockSpec((None, None, qb, D_ATT), lambda b, r, j: (b, r, jnp.maximum(j * per - 1, 0), 0))
        grid = (bsz, d, ln // rows)
        args = [q, k, k, v, v, bias]
        specs = [cur(D_ATT), cur(D_ATT), halo, cur(D_ATT), halo, _resident(bias.shape)]
        scratch = [pltpu.VMEM((rows + qb, D_ATT), BF16), pltpu.VMEM((rows + qb, D_ATT), BF16)]
    else:
        per = min(ATT_STEP_BLOCKS, d)
        cur = lambda width: pl.BlockSpec((None, per, qb, width), lambda b, r: (b, r, 0, 0))
        grid = (bsz, d // per)
        args = [q, k, v, bias]
        specs = [cur(D_ATT), cur(D_ATT), cur(D_ATT), _resident(bias.shape)]
        scratch = []
    return pl.pallas_call(
        functools.partial(_attn_kernel, has_prev=has_prev),
        out_shape=out_shape,
        grid=grid,
        in_specs=specs,
        out_specs=(cur(D_ATT), cur(LANES)),
        scratch_shapes=scratch,
        compiler_params=pltpu.CompilerParams(vmem_limit_bytes=VMEM_LIMIT),
        name=f"attn_d{d}",
    )(*args)


def _dilated_attention(qkv, rel_bias):
    outs = []
    for (window, dilation), (q, k, v) in zip(DILATED_PATTERNS, qkv):
        has_prev = q.shape[2] > ATT_BLOCK
        outs.append(_attn_branch(q, k, v, _branch_bias(rel_bias, window, dilation, has_prev)))
    return outs


def _ssm_tables(a_re, a_im, log_dt, b_re, b_im, c_re, c_im, d_skip):
    hi = lax.Precision.HIGHEST
    t, g, p, c = SSM_CHUNK, N_SSM_GROUPS, SSM_STATE, SSM_GROUP
    dt = jnp.exp(log_dt)[:, None]
    mag = jnp.exp(a_re * dt)
    ar, ai = mag * jnp.cos(a_im * dt), mag * jnp.sin(a_im * dt)
    den = a_re * a_re + a_im * a_im
    fr = ((ar - 1.0) * a_re + ai * a_im) / den
    fi = (ai * a_re - (ar - 1.0) * a_im) / den
    bbr = fr[:, :, None] * b_re - fi[:, :, None] * b_im
    bbi = fr[:, :, None] * b_im + fi[:, :, None] * b_re
    pr, pi_ = [jnp.ones_like(ar)], [jnp.zeros_like(ar)]
    for _ in range(t):
        pr.append(pr[-1] * ar - pi_[-1] * ai)
        pi_.append(pr[-2] * ai + pi_[-1] * ar)
    n = t * g * c
    assert n == 2 * g * p and t * c == 2 * p

    def expand(dense, row_div, col_div, spread):
        full = jnp.dot(dense.astype(BF16), spread, preferred_element_type=F32)
        rg = (lax.broadcasted_iota(jnp.int32, (n, n), 0) // row_div) % g
        cg = (lax.broadcasted_iota(jnp.int32, (n, n), 1) // col_div) % g
        return jnp.where(rg == cg, full, 0.0).astype(BF16)

    kk = np.arange(t * c)[:, None]
    cc = np.arange(n)[None, :]
    spread_rp = jnp.asarray((cc // (g * p) == kk // p) & (cc % p == kk % p), BF16)
    spread_jc = jnp.asarray((cc // (g * c) == kk // c) & (cc % c == kk % c), BF16)

    win = []
    for j in range(t):
        qr, qi = pr[t - 1 - j][:, :, None], pi_[t - 1 - j][:, :, None]
        win.append(jnp.stack([qr * bbr - qi * bbi, qr * bbi + qi * bbr], 0))
    win = jnp.stack(win, 0)
    w_in = expand(jnp.transpose(win, (0, 2, 4, 1, 3)).reshape(n, 2 * p), c, p, spread_rp)

    wout, hs = [], []
    for j in range(t + 1):
        wr = c_re * pr[j][:, None, :] - c_im * pi_[j][:, None, :]
        wi = c_re * pi_[j][:, None, :] + c_im * pr[j][:, None, :]
        if j >= 1:
            wout.append(jnp.stack([wr, -wi], 0))
        if j < t:
            hs.append(jnp.einsum('gcp,gpd->gdc', wr, bbr, precision=hi)
                      - jnp.einsum('gcp,gpd->gdc', wi, bbi, precision=hi))
    wout = jnp.stack(wout, 0)
    w_out = expand(jnp.transpose(wout, (1, 2, 4, 0, 3)).reshape(n, t * c), p, c, spread_jc)

    zero_h = jnp.zeros_like(hs[0])
    toep = jnp.stack([jnp.stack([hs[j - jp] if j >= jp else zero_h for j in range(t)], 0)
                      for jp in range(t)], 0)
    w_intra = expand(jnp.transpose(toep, (0, 2, 3, 1, 4)).reshape(n, t * c), c, c, spread_jc)

    abar_t = jnp.concatenate([pr[t].reshape(1, g * p), pi_[t].reshape(1, g * p)], axis=1)
    d_row = jnp.tile(d_skip.reshape(1, D_SSM), (1, t))
    return w_in, w_intra, w_out, abar_t, d_row


def _ssm_kernel(u_ref, win_ref, wintra_ref, wout_ref, abar_ref, d_ref, y_ref, z_ref, xp_ref):
    n = u_ref.shape[0]
    half = N_SSM_GROUPS * SSM_STATE
    u = u_ref[...]
    ub = u.astype(BF16)
    z_ref[...] = jnp.dot(ub, win_ref[...], preferred_element_type=F32)
    ar = abar_ref[:, 0:half]
    ai = abar_ref[:, half:]

    def step(k, carry):
        xr, xi = carry
        xp_ref[pl.ds(k, 1), 0:half] = xr
        xp_ref[pl.ds(k, 1), half:] = xi
        zr = z_ref[pl.ds(k, 1), 0:half]
        zi = z_ref[pl.ds(k, 1), half:]
        return ar * xr - ai * xi + zr, ar * xi + ai * xr + zi

    zero = jnp.zeros((1, half), F32)
    lax.fori_loop(0, n, step, (zero, zero))
    y = jnp.dot(ub, wintra_ref[...], preferred_element_type=F32)
    y = y + jnp.dot(xp_ref[...].astype(BF16), wout_ref[...], preferred_element_type=F32)
    y_ref[...] = y + d_ref[...] * u


def _ssm(u, tables):
    w_in, w_intra, w_out, abar_t, d_row = tables
    bsz, n, width = u.shape
    row = pl.BlockSpec((None, n, width), lambda b: (b, 0, 0))
    return pl.pallas_call(
        _ssm_kernel,
        out_shape=jax.ShapeDtypeStruct((bsz, n, width), F32),
        grid=(bsz,),
        in_specs=[row, _resident(w_in.shape), _resident(w_intra.shape), _resident(w_out.shape),
                  _resident(abar_t.shape), _resident(d_row.shape)],
        out_specs=row,
        scratch_shapes=[pltpu.VMEM((n, w_in.shape[1]), F32), pltpu.VMEM((n, w_in.shape[1]), F32)],
        compiler_params=pltpu.CompilerParams(vmem_limit_bytes=VMEM_LIMIT),
        name="ssm",
    )(u, w_in, w_intra, w_out, abar_t, d_row)


def _outproj_kernel(x_ref, mod_ref, *rest):
    nd = len(DILATIONS)
    o_refs, l_refs = rest[0:nd], rest[nd:2 * nd]
    (ys_ref, yp_ref, gw_ref, gb_ref, wa_ref, ws_ref, wp_ref, lg_ref, lb_ref, o_ref) = rest[2 * nd:2 * nd + 10]
    on4, on16, ot4, ln4, ln16, lt4, ysn_ref, ya_ref = rest[2 * nd + 10:]
    tm = x_ref.shape[0]
    slabs = D_ATT // LANES
    assert DILATIONS == (1, 4, 16)
    r4rows, r16rows = tm // 4, tm // 16

    for j in range(SSM_CHUNK):
        for c in range(D_SSM // LANES):
            lo = j * D_SSM + c * LANES
            ysn_ref[c, pl.ds(j, tm // SSM_CHUNK, stride=SSM_CHUNK), :] = ys_ref[:, lo:lo + LANES]

    for r4 in range(4):
        ln4[pl.ds(r4, r4rows, stride=4), :] = l_refs[1][r4]
        for c4 in range(4):
            lt4[pl.ds(r4 * r4rows + c4, r16rows, stride=4), :] = l_refs[2][r4 + 4 * c4]
        for c in range(slabs):
            lanes = slice(c * LANES, (c + 1) * LANES)
            on4[c, pl.ds(r4, r4rows, stride=4), :] = o_refs[1][r4, :, lanes].astype(F32)
            for c4 in range(4):
                ot4[c, pl.ds(r4 * r4rows + c4, r16rows, stride=4), :] = o_refs[2][r4 + 4 * c4, :, lanes].astype(F32)
    for r4 in range(4):
        ln16[pl.ds(r4, r4rows, stride=4), :] = lt4[r4 * r4rows:(r4 + 1) * r4rows, :]
        for c in range(slabs):
            on16[c, pl.ds(r4, r4rows, stride=4), :] = ot4[c, r4 * r4rows:(r4 + 1) * r4rows, :]

    head = lax.broadcasted_iota(jnp.int32, (LANES, D_ATT), 1) // HEAD_DIM
    spread = (head == lax.broadcasted_iota(jnp.int32, (LANES, D_ATT), 0)).astype(BF16)
    for rc in range(tm // MERGE_ROWS):
        rs = slice(rc * MERGE_ROWS, (rc + 1) * MERGE_ROWS)
        l_nat = [l_refs[0][0, rs, :], ln4[rs, :], ln16[rs, :]]
        o_nat = [o_refs[0][0, rs, :].astype(F32),
                 jnp.concatenate([on4[c, rs, :] for c in range(slabs)], axis=1),
                 jnp.concatenate([on16[c, rs, :] for c in range(slabs)], axis=1)]
        mx = functools.reduce(jnp.maximum, l_nat)
        es = [jnp.exp(lv - mx) for lv in l_nat]
        inv = 1.0 / functools.reduce(lambda a, b: a + b, es)
        ya = None
        for e, ov in zip(es, o_nat):
            w = e * inv
            w_hi = w.astype(BF16)
            w_lo = (w - w_hi.astype(F32)).astype(BF16)
            wx = (jnp.dot(w_hi, spread, preferred_element_type=F32)
                  + jnp.dot(w_lo, spread, preferred_element_type=F32))
            ya = wx * ov if ya is None else ya + wx * ov
        ya_ref[rs, :] = ya.astype(BF16)

    x = x_ref[...]
    gate = mod_ref[5:6, :]
    ys = jnp.concatenate([ysn_ref[c] for c in range(D_SSM // LANES)], axis=1)
    cdf = 0.5 * (1.0 + jnp.tanh(math.sqrt(2.0 / math.pi) * (ys + 0.044715 * (ys * ys * ys))))
    t = jnp.dot((ys * cdf).astype(BF16), gw_ref[...], preferred_element_type=F32) + gb_ref[...]
    yg = (ys * _sigmoid(t)).astype(BF16)
    y = jnp.dot(ya_ref[...], wa_ref[...], preferred_element_type=F32)
    y = y + jnp.dot(yg, ws_ref[...], preferred_element_type=F32)
    y = y + jnp.dot(yp_ref[...], wp_ref[...], preferred_element_type=F32)
    r = ALPHA * x + gate * y
    o_ref[...] = _layernorm(r) * lg_ref[...] + lb_ref[...]


def _outproj(x, mod_all, att, y_ssm, y_pool, glu_w, glu_b, w_out, ln_g, ln_b, l):
    bsz, s, d = x.shape
    tm = TOKEN_TILE
    tok = lambda width: pl.BlockSpec((None, tm, width), lambda b, i: (b, i, 0))
    res = lambda dil, width: pl.BlockSpec((None, dil, tm // dil, width), lambda b, i: (b, 0, i, 0))
    rows_of = lambda r0, n: pl.BlockSpec((None, n, d), lambda *_: (l, r0 // n, 0), pipeline_mode=pl.Buffered(1))
    assert D_ATT % D_SSM == 0 and D_SSM == D_POOL
    slab = lambda n: pltpu.VMEM((n, tm, LANES), F32)
    scratch = [slab(D_ATT // LANES)] * 3 + [pltpu.VMEM((tm, LANES), F32)] * 3
    scratch += [slab(D_SSM // LANES), pltpu.VMEM((tm, D_ATT), BF16)]
    return pl.pallas_call(
        _outproj_kernel,
        out_shape=jax.ShapeDtypeStruct(x.shape, F32),
        grid=(bsz, s // tm),
        in_specs=[tok(d), _mod_spec(l, d)]
        + [res(dil, D_ATT) for dil in DILATIONS]
        + [res(dil, LANES) for dil in DILATIONS]
        + [pl.BlockSpec((None, tm // SSM_CHUNK, SSM_CHUNK * D_SSM), lambda b, i: (b, i, 0)), tok(D_POOL),
           _stacked(glu_w.shape[1:], l), _stacked((1, D_SSM), l),
           rows_of(0, D_ATT), rows_of(D_ATT, D_SSM), rows_of(D_ATT + D_SSM, D_POOL),
           _stacked((1, d), l, 1), _stacked((1, d), l, 1)],
        out_specs=tok(d),
        scratch_shapes=scratch,
        compiler_params=pltpu.CompilerParams(vmem_limit_bytes=VMEM_LIMIT),
        name="outproj",
    )(x, mod_all, *[o for o, _ in att], *[lse for _, lse in att], y_ssm, y_pool,
      glu_w, glu_b, w_out, w_out, w_out, ln_g, ln_b)


def _block_diag(w):
    g, n, _ = w.shape
    return jnp.einsum('gab,gh->gahb', w, jnp.eye(g, dtype=w.dtype)).reshape(g * n, g * n)


def kernel(x, c, rel_bias, ada_w, ada_b, ln_g, ln_b, ffn_w_gate, ffn_w_up, ffn_w_down, w_in, w_out,
           ssm_a_re, ssm_a_im, ssm_log_dt, ssm_b_re, ssm_b_im, ssm_c_re, ssm_c_im, ssm_d, glu_w, glu_b,
           pool_w, pool_scale):
    bsz = x.shape[0]
    mod_all = _adaln(c, ada_w, ada_b).reshape(DEPTH, bsz, 9, D_MODEL)
    wg, wu, wd = ffn_w_gate.astype(BF16), ffn_w_up.astype(BF16), ffn_w_down.astype(BF16)
    w_in_b, w_out_b, glu_w_b = w_in.astype(BF16), w_out.astype(BF16), glu_w.astype(BF16)
    ln_g4, ln_b4 = ln_g.reshape(DEPTH, 3, 1, D_MODEL), ln_b.reshape(DEPTH, 3, 1, D_MODEL)
    glu_b3 = glu_b.reshape(DEPTH, 1, D_SSM)
    pool_scale3 = pool_scale.reshape(DEPTH, 1, D_POOL)
    for l in range(DEPTH):
        x = _ffn(x, mod_all, wg, wu, wd, ln_g4, ln_b4, l, 0)
        qkv, u_ssm, y_pool = _inproj(x, mod_all, w_in_b, _block_diag(pool_w[l]).astype(BF16), pool_scale3, l)
        att = _dilated_attention(qkv, rel_bias)
        tables = _ssm_tables(ssm_a_re[l], ssm_a_im[l], ssm_log_dt[l], ssm_b_re[l], ssm_b_im[l],
                             ssm_c_re[l], ssm_c_im[l], ssm_d[l])
        y_ssm = _ssm(u_ssm, tables)
        x = _outproj(x, mod_all, att, y_ssm, y_pool, glu_w_b, glu_b3, w_out_b, ln_g4, ln_b4, l)
        x = _ffn(x, mod_all, wg, wu, wd, ln_g4, ln_b4, l, 1)
    return x
```

```python
import functools
import math

import jax
import jax.numpy as jnp
import numpy as np
from jax import lax
from jax.experimental import pallas as pl
from jax.experimental.pallas import tpu as pltpu

F32 = jnp.float32
BF16 = jnp.bfloat16

D_MODEL = 1024
DEPTH = 2
HEAD_DIM = 64
N_HEADS = 8
D_ATT = N_HEADS * HEAD_DIM
DILATED_PATTERNS = ((128, 1), (512, 4), (2048, 16))
DILATIONS = tuple(d for _, d in DILATED_PATTERNS)
ATT_BLOCK = 128
SSM_GROUP = 16
D_SSM = 256
N_SSM_GROUPS = D_SSM // SSM_GROUP
SSM_STATE = 64
POOL_WINDOWS = (2, 4, 8, 16)
D_POOL = 256
POOL_GROUP = D_POOL // len(POOL_WINDOWS)
D_IN = 3 * D_ATT + D_SSM + D_POOL
D_FF = 2816
N_BUCKETS = 32
MAX_DISTANCE = 2048
ALPHA = (2 * DEPTH) ** 0.25
FFN_RES = 0.5
LN_EPS = 1e-5
NEG = -1e30

LANES = 128
TOKEN_TILE = 512
FF_CHUNK = 512
FFN_TOKEN_TILE = 512
FFN_SIDE_PIECES = 4
ADA_COL_TILE = 1152
SSM_CHUNK = 8
POOL_HALO = 2 * max(POOL_WINDOWS)
ATT_LOOKAHEAD = 8
ATT_STEP_BLOCKS = 4
MERGE_ROWS = 128
LOG2E = math.log2(math.e)
LN2 = math.log(2.0)
VMEM_LIMIT = 56 * 1024 * 1024


def _sigmoid(x):
    return 1.0 / (1.0 + jnp.exp(-x))


def _layernorm(x):
    mu = jnp.mean(x, axis=-1, keepdims=True)
    xc = x - mu
    var = jnp.mean(xc * xc, axis=-1, keepdims=True)
    return xc * lax.rsqrt(var + LN_EPS)


def _resident(shape):
    zeros = (0,) * len(shape)
    return pl.BlockSpec(shape, lambda *_: zeros, pipeline_mode=pl.Buffered(1))


def _stacked(tail, *lead):
    idx = tuple(lead) + (0,) * len(tail)
    return pl.BlockSpec((None,) * len(lead) + tuple(tail), lambda *_: idx, pipeline_mode=pl.Buffered(1))


def _adaln_kernel(c_ref, w_ref, b_ref, o_ref):
    c = c_ref[...]
    cond = (c * _sigmoid(c)).astype(BF16)
    o_ref[...] = jnp.dot(cond, w_ref[...].astype(BF16), preferred_element_type=F32) + b_ref[...]


def _adaln(c, ada_w, ada_b):
    nl, d, n = ada_w.shape
    bsz = c.shape[0]
    return pl.pallas_call(
        _adaln_kernel,
        out_shape=jax.ShapeDtypeStruct((nl, bsz, n), F32),
        grid=(nl, n // ADA_COL_TILE),
        in_specs=[
            pl.BlockSpec((bsz, d), lambda l, j: (0, 0)),
            pl.BlockSpec((None, d, ADA_COL_TILE), lambda l, j: (l, 0, j)),
            pl.BlockSpec((None, 1, ADA_COL_TILE), lambda l, j: (l, 0, j)),
        ],
        out_specs=pl.BlockSpec((None, bsz, ADA_COL_TILE), lambda l, j: (l, 0, j)),
        compiler_params=pltpu.CompilerParams(vmem_limit_bytes=VMEM_LIMIT),
        name="adaln",
    )(c, ada_w, ada_b.reshape(nl, 1, n))


def _ff_chunks():
    chunks, c0 = [], 0
    while c0 < D_FF:
        cw = min(FF_CHUNK, D_FF - c0)
        chunks.append((c0, cw))
        c0 += cw
    return tuple(chunks)


def _ffn_kernel(x_ref, modp_ref, mode_ref, wg_ref, wu_ref, wd_ref, lg_ref, lb_ref, o_ref,
                h0, h1, xs0, xs1, acc0, acc1, *, sub, n_tiles):
    s = pl.program_id(0)
    hs, xss, accs = (h0, h1), (xs0, xs1), (acc0, acc1)

    def tied_zero(v):
        bits = lax.bitcast_convert_type(v, jnp.uint32)
        bits = lax.shift_right_logical(lax.shift_right_logical(bits, jnp.uint32(16)), jnp.uint32(16))
        return lax.bitcast_convert_type(bits, F32)

    def prologue(par, r0=0, n=None):
        n = x_ref.shape[0] if n is None else n
        x = x_ref[r0:r0 + n, :]
        shift = modp_ref[3 * sub + 0:3 * sub + 1, :]
        scale = modp_ref[3 * sub + 1:3 * sub + 2, :]
        hv = _layernorm(x) * (1.0 + scale) + shift
        hs[par][r0:r0 + n, :] = hv.astype(BF16)
        xss[par][r0:r0 + n, :] = x
        return tied_zero(hv[0:8, 0:LANES])

    def epilogue(par, r0=0, n=None):
        n = x_ref.shape[0] if n is None else n
        gate = mode_ref[3 * sub + 2:3 * sub + 3, :]
        y = ALPHA * xss[par][r0:r0 + n, :] + (FFN_RES * gate) * accs[par][r0:r0 + n, :]
        out = _layernorm(y) * lg_ref[...] + lb_ref[...]
        o_ref[r0:r0 + n, :] = out
        return tied_zero(out[0:8, 0:LANES])

    def steady(par):
        chunks = _ff_chunks()
        piece = x_ref.shape[0] // FFN_SIDE_PIECES
        side = [functools.partial(epilogue, par, k * piece, piece) for k in range(FFN_SIDE_PIECES)]
        side += [functools.partial(prologue, par, k * piece, piece) for k in range(FFN_SIDE_PIECES)]
        per_chunk = -(-len(side) // len(chunks))
        h = hs[1 - par][...]
        acc = accs[1 - par]
        for idx, (c0, cw) in enumerate(chunks):
            g = jnp.dot(h, wg_ref[:, c0:c0 + cw], preferred_element_type=F32)
            u = jnp.dot(h, wu_ref[:, c0:c0 + cw], preferred_element_type=F32)
            a = (g * _sigmoid(g) * u).astype(BF16)
            d = jnp.dot(a, wd_ref[c0:c0 + cw, :], preferred_element_type=F32)
            if idx == 0:
                acc[...] = d
            else:
                acc[...] += d
            for fn in side[idx * per_chunk:(idx + 1) * per_chunk]:
                acc[0:8, 0:LANES] += fn()

    @pl.when(s == 0)
    def _():
        xs1[...] = jnp.zeros(xs1.shape, F32)
        acc1[...] = jnp.zeros(acc1.shape, F32)
        prologue(0)

    for par in range(2):
        pl.when(jnp.logical_and(jnp.logical_and(s >= 1, s <= n_tiles), s % 2 == par))(
            functools.partial(steady, par))

    @pl.when(s == n_tiles + 1)
    def _():
        epilogue((n_tiles + 1) % 2)


def _mod_spec(l, d):
    return pl.BlockSpec((None, None, 9, d), lambda b, i: (l, b, 0, 0))


def _ffn(x, mod_all, wg, wu, wd, ln_g, ln_b, l, which):
    bsz, s, d = x.shape
    tm = FFN_TOKEN_TILE
    sub = 2 * which
    per_seq = s // tm
    n_tiles = bsz * per_seq
    tile_in = lambda i: jnp.minimum(i, n_tiles - 1)
    tile_out = lambda i: jnp.clip(i - 2, 0, n_tiles - 1)
    out = pl.pallas_call(
        functools.partial(_ffn_kernel, sub=sub, n_tiles=n_tiles),
        out_shape=jax.ShapeDtypeStruct((bsz * s, d), F32),
        grid=(n_tiles + 2,),
        in_specs=[
            pl.BlockSpec((tm, d), lambda i: (tile_in(i), 0)),
            pl.BlockSpec((None, None, 9, d), lambda i: (l, tile_in(i) // per_seq, 0, 0)),
            pl.BlockSpec((None, None, 9, d), lambda i: (l, tile_out(i) // per_seq, 0, 0)),
            _stacked(wg.shape[2:], l, which),
            _stacked(wu.shape[2:], l, which),
            _stacked(wd.shape[2:], l, which),
            _stacked((1, d), l, sub),
            _stacked((1, d), l, sub),
        ],
        out_specs=pl.BlockSpec((tm, d), lambda i: (tile_out(i), 0)),
        scratch_shapes=[pltpu.VMEM((tm, d), BF16)] * 2 + [pltpu.VMEM((tm, d), F32)] * 4,
        compiler_params=pltpu.CompilerParams(dimension_semantics=("arbitrary",), vmem_limit_bytes=VMEM_LIMIT),
        name=f"ffn{sub}",
    )(x.reshape(bsz * s, d), mod_all, mod_all, wg, wu, wd, ln_g, ln_b)
    return out.reshape(bsz, s, d)


def _inproj_kernel(x_ref, mod_ref, w_ref, pw_ref, ps_ref, *rest):
    qkv_refs = rest[:3 * len(DILATIONS)]
    us_ref, yp_ref, zs_ref, z4_ref = rest[3 * len(DILATIONS):3 * len(DILATIONS) + 4]
    e_refs = rest[3 * len(DILATIONS) + 4:]
    i = pl.program_id(1)
    tm = x_ref.shape[0]
    x = x_ref[...]
    shift = mod_ref[3:4, :]
    scale = mod_ref[4:5, :]
    h = (_layernorm(x) * (1.0 + scale) + shift).astype(BF16)
    z = jnp.dot(h, w_ref[...], preferred_element_type=F32)
    up = z[:, 3 * D_ATT + D_SSM:]

    slabs_per = D_ATT // LANES
    n_qkv = 3 * slabs_per
    n_slabs = (3 * D_ATT + D_SSM) // LANES
    for c in range(n_slabs):
        col = z[:, c * LANES:(c + 1) * LANES]
        zs_ref[c] = col * (HEAD_DIM ** -0.5 * LOG2E) if c < slabs_per else col
    assert DILATIONS == (1, 4, 16)
    q4rows, q16rows = tm // 4, tm // 16
    for c in range(n_qkv):
        which, lanes = c // slabs_per, slice((c % slabs_per) * LANES, (c % slabs_per + 1) * LANES)
        qkv_refs[which][0, :, lanes] = zs_ref[c].astype(BF16)
        for r4 in range(4):
            blk = zs_ref[c, pl.ds(r4, q4rows, stride=4), :]
            z4_ref[c, r4 * q4rows:(r4 + 1) * q4rows, :] = blk
            qkv_refs[3 + which][r4, :, lanes] = blk.astype(BF16)
        for r4 in range(4):
            for c4 in range(4):
                blk = z4_ref[c, pl.ds(r4 * q4rows + c4, q16rows, stride=4), :]
                qkv_refs[6 + which][r4 + 4 * c4, :, lanes] = blk.astype(BF16)
    for c in range(n_qkv, n_slabs):
        for j in range(SSM_CHUNK):
            lo = j * D_SSM + (c - n_qkv) * LANES
            us_ref[:, lo:lo + LANES] = zs_ref[c, pl.ds(j, tm // SSM_CHUNK, stride=SSM_CHUNK), :]

    e1, e2, e4, e8 = e_refs
    hl = POOL_HALO
    assert POOL_WINDOWS == (2, 4, 8, 16) and hl == 32

    @pl.when(i == 0)
    def _():
        e1[0:hl, :] = jnp.zeros((hl, D_POOL), F32)

    @pl.when(i > 0)
    def _():
        e1[0:hl, :] = e1[tm:tm + hl, :]

    e1[hl:hl + tm, :] = up
    e2[8:, :] = e1[8:, :] + e1[7:tm + hl - 1, :]
    e4[16:, :] = e2[16:, :] + e2[14:tm + hl - 2, :]
    e8[24:, :] = e4[24:, :] + e4[20:tm + hl - 4, :]
    sums = {2: e2[hl:, :], 4: e4[hl:, :], 8: e8[hl:, :], 16: e8[hl:, :] + e8[hl - 8:tm + hl - 8, :]}
    pos = (i * tm + lax.broadcasted_iota(jnp.int32, (tm, 1), 0) + 1).astype(F32)
    group = lax.broadcasted_iota(jnp.int32, (1, D_POOL), 1) // POOL_GROUP
    mean = sums[POOL_WINDOWS[-1]] / jnp.minimum(pos, float(POOL_WINDOWS[-1]))
    for gi in range(len(POOL_WINDOWS) - 2, -1, -1):
        w = POOL_WINDOWS[gi]
        mean = jnp.where(group == gi, sums[w] / jnp.minimum(pos, float(w)), mean)
    pooled = (mean - up).astype(BF16)
    yp = jnp.dot(pooled, pw_ref[...], preferred_element_type=F32) * ps_ref[...]
    yp_ref[...] = yp.astype(BF16)


def _inproj(x, mod_all, w_in, pool_w_bd, pool_scale, l):
    bsz, s, d = x.shape
    tm = TOKEN_TILE
    tok = lambda width: pl.BlockSpec((None, tm, width), lambda b, i: (b, i, 0))
    qkv_shapes, qkv_specs = [], []
    for dil in DILATIONS:
        for _ in range(3):
            qkv_shapes.append(jax.ShapeDtypeStruct((bsz, dil, s // dil, D_ATT), BF16))
            qkv_specs.append(pl.BlockSpec((None, dil, tm // dil, D_ATT), lambda b, i: (b, 0, i, 0)))
    res = pl.pallas_call(
        _inproj_kernel,
        out_shape=tuple(qkv_shapes) + (
            jax.ShapeDtypeStruct((bsz, s // SSM_CHUNK, SSM_CHUNK * D_SSM), F32),
            jax.ShapeDtypeStruct((bsz, s, D_POOL), BF16),
        ),
        grid=(bsz, s // tm),
        in_specs=[
            tok(d),
            _mod_spec(l, d),
            _stacked(w_in.shape[1:], l),
            _resident(pool_w_bd.shape),
            _stacked((1, D_POOL), l),
        ],
        out_specs=tuple(qkv_specs) + (
            pl.BlockSpec((None, tm // SSM_CHUNK, SSM_CHUNK * D_SSM), lambda b, i: (b, i, 0)), tok(D_POOL)),
        scratch_shapes=[pltpu.VMEM(((3 * D_ATT + D_SSM) // LANES, tm, LANES), F32),
                        pltpu.VMEM((3 * D_ATT // LANES, tm, LANES), F32)]
        + [pltpu.VMEM((tm + POOL_HALO, D_POOL), F32) for _ in range(4)],
        compiler_params=pltpu.CompilerParams(
            dimension_semantics=("arbitrary", "arbitrary"), vmem_limit_bytes=VMEM_LIMIT),
        name="inproj",
    )(x, mod_all, w_in, pool_w_bd, pool_scale)
    qkv = [tuple(res[3 * di:3 * di + 3]) for di in range(len(DILATIONS))]
    return qkv, res[-2], res[-1]


def _t5_bucket(dist):
    max_exact = N_BUCKETS // 2
    dd = np.maximum(dist, 1).astype(np.float32)
    large = max_exact + (np.log(dd / max_exact) / math.log(MAX_DISTANCE / max_exact)
                         * (N_BUCKETS - max_exact)).astype(np.int32)
    large = np.minimum(large, N_BUCKETS - 1)
    return np.where(dist < max_exact, dist, large).astype(np.int32)


def _branch_bias(rel_bias, window, dilation, has_prev):
    qb = ATT_BLOCK
    n_keys = window // dilation
    assert n_keys == qb
    period = 3 * qb
    dist = np.arange(n_keys, -1, -1)
    row = rel_bias[_t5_bucket(dist * dilation)].T.astype(F32) * LOG2E
    row = jnp.concatenate([row, jnp.full((N_HEADS, period - n_keys - 1), NEG, F32)], axis=1)
    flat = jnp.tile(row, (1, qb))[:, :qb * (period - 1)]
    bias = flat.reshape(N_HEADS, qb, period - 1)[:, :, :2 * qb]
    if not has_prev:
        return bias[:, :, qb:]
    first = jnp.concatenate([jnp.full((N_HEADS, qb, qb), NEG, F32), bias[:, :, qb:]], axis=2)
    return jnp.stack([bias, first], 0)


def _attn_kernel(*refs, has_prev):
    qb = ATT_BLOCK
    if has_prev:
        q_ref, kc_ref, kh_ref, vc_ref, vh_ref, bias_ref, o_ref, lse_ref, kbuf, vbuf = refs
        nsub = q_ref.shape[0] // qb
        kbuf[0:qb, :] = kh_ref[...]
        kbuf[qb:, :] = kc_ref[...]
        vbuf[0:qb, :] = vh_ref[...]
        vbuf[qb:, :] = vc_ref[...]
    else:
        q_ref, kc_ref, vc_ref, bias_ref, o_ref, lse_ref = refs
        nsub = q_ref.shape[0]

    pair = 2 * HEAD_DIM
    lane = lax.broadcasted_iota(jnp.int32, (1, pair), 1)
    low = lane < HEAD_DIM
    lse_lane = lax.broadcasted_iota(jnp.int32, (1, LANES), 1)

    def block(jj, carry):
        if has_prev:
            row0 = pl.multiple_of(jj * qb, qb)
            first = jnp.logical_and(pl.program_id(2) == 0, jj == 0).astype(jnp.int32)
            q_at = lambda cols: q_ref[pl.ds(row0, qb), cols]
            k_at = lambda cols: kbuf[pl.ds(row0, 2 * qb), cols]
            v_at = lambda cols: vbuf[pl.ds(row0, 2 * qb), cols]
            bias_at = lambda h: bias_ref[first, h]
        else:
            q_at = lambda cols: q_ref[jj, :, cols]
            k_at = lambda cols: kc_ref[jj, :, cols]
            v_at = lambda cols: vc_ref[jj, :, cols]
            bias_at = lambda h: bias_ref[h]

        def pair_cols(h):
            return slice((h // 2) * pair, (h // 2 + 1) * pair)

        def scores(h):
            q2 = q_at(pair_cols(h))
            qh = jnp.where(low, q2, jnp.zeros_like(q2)) if h % 2 == 0 else jnp.where(low, jnp.zeros_like(q2), q2)
            s = lax.dot_general(qh, k_at(pair_cols(h)), (((1,), (1,)), ((), ())), preferred_element_type=F32)
            return s + bias_at(h)

        pending = {h: scores(h) for h in range(min(ATT_LOOKAHEAD, N_HEADS))}
        outs = {}
        lse_tile = jnp.zeros((qb, LANES), F32)
        for h in range(N_HEADS):
            s = pending.pop(h)
            m = jnp.max(s, axis=-1, keepdims=True)
            p = jnp.exp2(s - m)
            l = jnp.sum(p, axis=-1, keepdims=True)
            if h + ATT_LOOKAHEAD < N_HEADS:
                pending[h + ATT_LOOKAHEAD] = scores(h + ATT_LOOKAHEAD)
            cols = pair_cols(h)
            pv = jnp.dot(p.astype(BF16), v_at(cols), preferred_element_type=F32)
            outs[h] = pv * (1.0 / l)
            lse_tile = jnp.where(lse_lane == h, (m + jnp.log2(l)) * LN2, lse_tile)
            if h % 2 == 1:
                o2 = jnp.where(low, outs.pop(h - 1), outs.pop(h)).astype(BF16)
                if has_prev:
                    o_ref[pl.ds(row0, qb), cols] = o2
                else:
                    o_ref[jj, :, cols] = o2
        if has_prev:
            lse_ref[pl.ds(row0, qb), :] = lse_tile
        else:
            lse_ref[jj] = lse_tile
        return carry

    lax.fori_loop(0, nsub, block, 0)


def _attn_branch(q, k, v, bias):
    bsz, d, ln, _ = q.shape
    qb = ATT_BLOCK
    has_prev = ln > qb
    out_shape = (jax.ShapeDtypeStruct((bsz, d, ln, D_ATT), BF16),
                 jax.ShapeDtypeStruct((bsz, d, ln, LANES), F32))
    if has_prev:
        rows = min(ATT_STEP_BLOCKS * qb, ln)
        per = rows // qb
        cur = lambda width: pl.BlockSpec((None, None, rows, width), lambda b, r, j: (b, r, j, 0))
        halo = pl.BlockSpec((None, None, qb, D_ATT), lambda b, r, j: (b, r, jnp.maximum(j * per - 1, 0), 0))
        grid = (bsz, d, ln // rows)
        args = [q, k, k, v, v, bias]
        specs = [cur(D_ATT), cur(D_ATT), halo, cur(D_ATT), halo, _resident(bias.shape)]
        scratch = [pltpu.VMEM((rows + qb, D_ATT), BF16), pltpu.VMEM((rows + qb, D_ATT), BF16)]
    else:
        per = min(ATT_STEP_BLOCKS, d)
        cur = lambda width: pl.BlockSpec((None, per, qb, width), lambda b, r: (b, r, 0, 0))
        grid = (bsz, d // per)
        args = [q, k, v, bias]
        specs = [cur(D_ATT), cur(D_ATT), cur(D_ATT), _resident(bias.shape)]
        scratch = []
    return pl.pallas_call(
        functools.partial(_attn_kernel, has_prev=has_prev),
        out_shape=out_shape,
        grid=grid,
        in_specs=specs,
        out_specs=(cur(D_ATT), cur(LANES)),
        scratch_shapes=scratch,
        compiler_params=pltpu.CompilerParams(vmem_limit_bytes=VMEM_LIMIT),
        name=f"attn_d{d}",
    )(*args)


def _dilated_attention(qkv, rel_bias):
    outs = []
    for (window, dilation), (q, k, v) in zip(DILATED_PATTERNS, qkv):
        has_prev = q.shape[2] > ATT_BLOCK
        outs.append(_attn_branch(q, k, v, _branch_bias(rel_bias, window, dilation, has_prev)))
    return outs


def _ssm_tables(a_re, a_im, log_dt, b_re, b_im, c_re, c_im, d_skip):
    hi = lax.Precision.HIGHEST
    t, g, p, c = SSM_CHUNK, N_SSM_GROUPS, SSM_STATE, SSM_GROUP
    dt = jnp.exp(log_dt)[:, None]
    mag = jnp.exp(a_re * dt)
    ar, ai = mag * jnp.cos(a_im * dt), mag * jnp.sin(a_im * dt)
    den = a_re * a_re + a_im * a_im
    fr = ((ar - 1.0) * a_re + ai * a_im) / den
    fi = (ai * a_re - (ar - 1.0) * a_im) / den
    bbr = fr[:, :, None] * b_re - fi[:, :, None] * b_im
    bbi = fr[:, :, None] * b_im + fi[:, :, None] * b_re
    pr, pi_ = [jnp.ones_like(ar)], [jnp.zeros_like(ar)]
    for _ in range(t):
        pr.append(pr[-1] * ar - pi_[-1] * ai)
        pi_.append(pr[-2] * ai + pi_[-1] * ar)
    n = t * g * c
    assert n == 2 * g * p and t * c == 2 * p

    def expand(dense, row_div, col_div, spread):
        full = jnp.dot(dense.astype(BF16), spread, preferred_element_type=F32)
        rg = (lax.broadcasted_iota(jnp.int32, (n, n), 0) // row_div) % g
        cg = (lax.broadcasted_iota(jnp.int32, (n, n), 1) // col_div) % g
        return jnp.where(rg == cg, full, 0.0).astype(BF16)

    kk = np.arange(t * c)[:, None]
    cc = np.arange(n)[None, :]
    spread_rp = jnp.asarray((cc // (g * p) == kk // p) & (cc % p == kk % p), BF16)
    spread_jc = jnp.asarray((cc // (g * c) == kk // c) & (cc % c == kk % c), BF16)

    win = []
    for j in range(t):
        qr, qi = pr[t - 1 - j][:, :, None], pi_[t - 1 - j][:, :, None]
        win.append(jnp.stack([qr * bbr - qi * bbi, qr * bbi + qi * bbr], 0))
    win = jnp.stack(win, 0)
    w_in = expand(jnp.transpose(win, (0, 2, 4, 1, 3)).reshape(n, 2 * p), c, p, spread_rp)

    wout, hs = [], []
    for j in range(t + 1):
        wr = c_re * pr[j][:, None, :] - c_im * pi_[j][:, None, :]
        wi = c_re * pi_[j][:, None, :] + c_im * pr[j][:, None, :]
        if j >= 1:
            wout.append(jnp.stack([wr, -wi], 0))
        if j < t:
            hs.append(jnp.einsum('gcp,gpd->gdc', wr, bbr, precision=hi)
                      - jnp.einsum('gcp,gpd->gdc', wi, bbi, precision=hi))
    wout = jnp.stack(wout, 0)
    w_out = expand(jnp.transpose(wout, (1, 2, 4, 0, 3)).reshape(n, t * c), p, c, spread_jc)

    zero_h = jnp.zeros_like(hs[0])
    toep = jnp.stack([jnp.stack([hs[j - jp] if j >= jp else zero_h for j in range(t)], 0)
                      for jp in range(t)], 0)
    w_intra = expand(jnp.transpose(toep, (0, 2, 3, 1, 4)).reshape(n, t * c), c, c, spread_jc)

    abar_t = jnp.concatenate([pr[t].reshape(1, g * p), pi_[t].reshape(1, g * p)], axis=1)
    d_row = jnp.tile(d_skip.reshape(1, D_SSM), (1, t))
    return w_in, w_intra, w_out, abar_t, d_row


def _ssm_kernel(u_ref, win_ref, wintra_ref, wout_ref, abar_ref, d_ref, y_ref, z_ref, xp_ref):
    n = u_ref.shape[0]
    half = N_SSM_GROUPS * SSM_STATE
    u = u_ref[...]
    ub = u.astype(BF16)
    z_ref[...] = jnp.dot(ub, win_ref[...], preferred_element_type=F32)
    ar = abar_ref[:, 0:half]
    ai = abar_ref[:, half:]

    def step(k, carry):
        xr, xi = carry
        xp_ref[pl.ds(k, 1), 0:half] = xr
        xp_ref[pl.ds(k, 1), half:] = xi
        zr = z_ref[pl.ds(k, 1), 0:half]
        zi = z_ref[pl.ds(k, 1), half:]
        return ar * xr - ai * xi + zr, ar * xi + ai * xr + zi

    zero = jnp.zeros((1, half), F32)
    lax.fori_loop(0, n, step, (zero, zero))
    y = jnp.dot(ub, wintra_ref[...], preferred_element_type=F32)
    y = y + jnp.dot(xp_ref[...].astype(BF16), wout_ref[...], preferred_element_type=F32)
    y_ref[...] = y + d_ref[...] * u


def _ssm(u, tables):
    w_in, w_intra, w_out, abar_t, d_row = tables
    bsz, n, width = u.shape
    row = pl.BlockSpec((None, n, width), lambda b: (b, 0, 0))
    return pl.pallas_call(
        _ssm_kernel,
        out_shape=jax.ShapeDtypeStruct((bsz, n, width), F32),
        grid=(bsz,),
        in_specs=[row, _resident(w_in.shape), _resident(w_intra.shape), _resident(w_out.shape),
                  _resident(abar_t.shape), _resident(d_row.shape)],
        out_specs=row,
        scratch_shapes=[pltpu.VMEM((n, w_in.shape[1]), F32), pltpu.VMEM((n, w_in.shape[1]), F32)],
        compiler_params=pltpu.CompilerParams(vmem_limit_bytes=VMEM_LIMIT),
        name="ssm",
    )(u, w_in, w_intra, w_out, abar_t, d_row)


def _outproj_kernel(x_ref, mod_ref, *rest):
    nd = len(DILATIONS)
    o_refs, l_refs = rest[0:nd], rest[nd:2 * nd]
    (ys_ref, yp_ref, gw_ref, gb_ref, wa_ref, ws_ref, wp_ref, lg_ref, lb_ref, o_ref) = rest[2 * nd:2 * nd + 10]
    on4, on16, ot4, ln4, ln16, lt4, ysn_ref, ya_ref = rest[2 * nd + 10:]
    tm = x_ref.shape[0]
    slabs = D_ATT // LANES
    assert DILATIONS == (1, 4, 16)
    r4rows, r16rows = tm // 4, tm // 16

    for j in range(SSM_CHUNK):
        for c in range(D_SSM // LANES):
            lo = j * D_SSM + c * LANES
            ysn_ref[c, pl.ds(j, tm // SSM_CHUNK, stride=SSM_CHUNK), :] = ys_ref[:, lo:lo + LANES]

    for r4 in range(4):
        ln4[pl.ds(r4, r4rows, stride=4), :] = l_refs[1][r4]
        for c4 in range(4):
            lt4[pl.ds(r4 * r4rows + c4, r16rows, stride=4), :] = l_refs[2][r4 + 4 * c4]
        for c in range(slabs):
            lanes = slice(c * LANES, (c + 1) * LANES)
            on4[c, pl.ds(r4, r4rows, stride=4), :] = o_refs[1][r4, :, lanes].astype(F32)
            for c4 in range(4):
                ot4[c, pl.ds(r4 * r4rows + c4, r16rows, stride=4), :] = o_refs[2][r4 + 4 * c4, :, lanes].astype(F32)
    for r4 in range(4):
        ln16[pl.ds(r4, r4rows, stride=4), :] = lt4[r4 * r4rows:(r4 + 1) * r4rows, :]
        for c in range(slabs):
            on16[c, pl.ds(r4, r4rows, stride=4), :] = ot4[c, r4 * r4rows:(r4 + 1) * r4rows, :]

    head = lax.broadcasted_iota(jnp.int32, (LANES, D_ATT), 1) // HEAD_DIM
    spread = (head == lax.broadcasted_iota(jnp.int32, (LANES, D_ATT), 0)).astype(BF16)
    for rc in range(tm // MERGE_ROWS):
        rs = slice(rc * MERGE_ROWS, (rc + 1) * MERGE_ROWS)
        l_nat = [l_refs[0][0, rs, :], ln4[rs, :], ln16[rs, :]]
        o_nat = [o_refs[0][0, rs, :].astype(F32),
                 jnp.concatenate([on4[c, rs, :] for c in range(slabs)], axis=1),
                 jnp.concatenate([on16[c, rs, :] for c in range(slabs)], axis=1)]
        mx = functools.reduce(jnp.maximum, l_nat)
        es = [jnp.exp(lv - mx) for lv in l_nat]
        inv = 1.0 / functools.reduce(lambda a, b: a + b, es)
        ya = None
        for e, ov in zip(es, o_nat):
            w = e * inv
            w_hi = w.astype(BF16)
            w_lo = (w - w_hi.astype(F32)).astype(BF16)
            wx = (jnp.dot(w_hi, spread, preferred_element_type=F32)
                  + jnp.dot(w_lo, spread, preferred_element_type=F32))
            ya = wx * ov if ya is None else ya + wx * ov
        ya_ref[rs, :] = ya.astype(BF16)

    x = x_ref[...]
    gate = mod_ref[5:6, :]
    ys = jnp.concatenate([ysn_ref[c] for c in range(D_SSM // LANES)], axis=1)
    cdf = 0.5 * (1.0 + jnp.tanh(math.sqrt(2.0 / math.pi) * (ys + 0.044715 * (ys * ys * ys))))
    t = jnp.dot((ys * cdf).astype(BF16), gw_ref[...], preferred_element_type=F32) + gb_ref[...]
    yg = (ys * _sigmoid(t)).astype(BF16)
    y = jnp.dot(ya_ref[...], wa_ref[...], preferred_element_type=F32)
    y = y + jnp.dot(yg, ws_ref[...], preferred_element_type=F32)
    y = y + jnp.dot(yp_ref[...], wp_ref[...], preferred_element_type=F32)
    r = ALPHA * x + gate * y
    o_ref[...] = _layernorm(r) * lg_ref[...] + lb_ref[...]


def _outproj(x, mod_all, att, y_ssm, y_pool, glu_w, glu_b, w_out, ln_g, ln_b, l):
    bsz, s, d = x.shape
    tm = TOKEN_TILE
    tok = lambda width: pl.BlockSpec((None, tm, width), lambda b, i: (b, i, 0))
    res = lambda dil, width: pl.BlockSpec((None, dil, tm // dil, width), lambda b, i: (b, 0, i, 0))
    rows_of = lambda r0, n: pl.BlockSpec((None, n, d), lambda *_: (l, r0 // n, 0), pipeline_mode=pl.Buffered(1))
    assert D_ATT % D_SSM == 0 and D_SSM == D_POOL
    slab = lambda n: pltpu.VMEM((n, tm, LANES), F32)
    scratch = [slab(D_ATT // LANES)] * 3 + [pltpu.VMEM((tm, LANES), F32)] * 3
    scratch += [slab(D_SSM // LANES), pltpu.VMEM((tm, D_ATT), BF16)]
    return pl.pallas_call(
        _outproj_kernel,
        out_shape=jax.ShapeDtypeStruct(x.shape, F32),
        grid=(bsz, s // tm),
        in_specs=[tok(d), _mod_spec(l, d)]
        + [res(dil, D_ATT) for dil in DILATIONS]
        + [res(dil, LANES) for dil in DILATIONS]
        + [pl.BlockSpec((None, tm // SSM_CHUNK, SSM_CHUNK * D_SSM), lambda b, i: (b, i, 0)), tok(D_POOL),
           _stacked(glu_w.shape[1:], l), _stacked((1, D_SSM), l),
           rows_of(0, D_ATT), rows_of(D_ATT, D_SSM), rows_of(D_ATT + D_SSM, D_POOL),
           _stacked((1, d), l, 1), _stacked((1, d), l, 1)],
        out_specs=tok(d),
        scratch_shapes=scratch,
        compiler_params=pltpu.CompilerParams(vmem_limit_bytes=VMEM_LIMIT),
        name="outproj",
    )(x, mod_all, *[o for o, _ in att], *[lse for _, lse in att], y_ssm, y_pool,
      glu_w, glu_b, w_out, w_out, w_out, ln_g, ln_b)


def _block_diag(w):
    g, n, _ = w.shape
    return jnp.einsum('gab,gh->gahb', w, jnp.eye(g, dtype=w.dtype)).reshape(g * n, g * n)


def kernel(x, c, rel_bias, ada_w, ada_b, ln_g, ln_b, ffn_w_gate, ffn_w_up, ffn_w_down, w_in, w_out,
           ssm_a_re, ssm_a_im, ssm_log_dt, ssm_b_re, ssm_b_im, ssm_c_re, ssm_c_im, ssm_d, glu_w, glu_b,
           pool_w, pool_scale):
    bsz = x.shape[0]
    mod_all = _adaln(c, ada_w, ada_b).reshape(DEPTH, bsz, 9, D_MODEL)
    wg, wu, wd = ffn_w_gate.astype(BF16), ffn_w_up.astype(BF16), ffn_w_down.astype(BF16)
    w_in_b, w_out_b, glu_w_b = w_in.astype(BF16), w_out.astype(BF16), glu_w.astype(BF16)
    ln_g4, ln_b4 = ln_g.reshape(DEPTH, 3, 1, D_MODEL), ln_b.reshape(DEPTH, 3, 1, D_MODEL)
    glu_b3 = glu_b.reshape(DEPTH, 1, D_SSM)
    pool_scale3 = pool_scale.reshape(DEPTH, 1, D_POOL)
    for l in range(DEPTH):
        x = _ffn(x, mod_all, wg, wu, wd, ln_g4, ln_b4, l, 0)
        qkv, u_ssm, y_pool = _inproj(x, mod_all, w_in_b, _block_diag(pool_w[l]).astype(BF16), pool_scale3, l)
        att = _dilated_attention(qkv, rel_bias)
        tables = _ssm_tables(ssm_a_re[l], ssm_a_im[l], ssm_log_dt[l], ssm_b_re[l], ssm_b_im[l],
                             ssm_c_re[l], ssm_c_im[l], ssm_d[l])
        y_ssm = _ssm(u_ssm, tables)
        x = _outproj(x, mod_all, att, y_ssm, y_pool, glu_w_b, glu_b3, w_out_b, ln_g4, ln_b4, l)
        x = _ffn(x, mod_all, wg, wu, wd, ln_g4, ln_b4, l, 1)
    return x
```

```python
import functools
import math

import jax
import jax.numpy as jnp
import numpy as np
from jax import lax
from jax.experimental import pallas as pl
from jax.experimental.pallas import tpu as pltpu

F32 = jnp.float32
BF16 = jnp.bfloat16

D_MODEL = 1024
DEPTH = 2
HEAD_DIM = 64
N_HEADS = 8
D_ATT = N_HEADS * HEAD_DIM
DILATED_PATTERNS = ((128, 1), (512, 4), (2048, 16))
DILATIONS = tuple(d for _, d in DILATED_PATTERNS)
ATT_BLOCK = 128
SSM_GROUP = 16
D_SSM = 256
N_SSM_GROUPS = D_SSM // SSM_GROUP
SSM_STATE = 64
POOL_WINDOWS = (2, 4, 8, 16)
D_POOL = 256
POOL_GROUP = D_POOL // len(POOL_WINDOWS)
D_IN = 3 * D_ATT + D_SSM + D_POOL
D_FF = 2816
N_BUCKETS = 32
MAX_DISTANCE = 2048
ALPHA = (2 * DEPTH) ** 0.25
FFN_RES = 0.5
LN_EPS = 1e-5
NEG = -1e30

LANES = 128
TOKEN_TILE = 512
FF_CHUNK = 512
FFN_TOKEN_TILE = 512
ADA_COL_TILE = 1152
SSM_CHUNK = 8
POOL_HALO = 2 * max(POOL_WINDOWS)
ATT_LOOKAHEAD = 8
ATT_STEP_BLOCKS = 4
MERGE_ROWS = 128
LOG2E = math.log2(math.e)
LN2 = math.log(2.0)
VMEM_LIMIT = 56 * 1024 * 1024


def _sigmoid(x):
    return 1.0 / (1.0 + jnp.exp(-x))


def _layernorm(x):
    mu = jnp.mean(x, axis=-1, keepdims=True)
    xc = x - mu
    var = jnp.mean(xc * xc, axis=-1, keepdims=True)
    return xc * lax.rsqrt(var + LN_EPS)


def _resident(shape):
    zeros = (0,) * len(shape)
    return pl.BlockSpec(shape, lambda *_: zeros, pipeline_mode=pl.Buffered(1))


def _stacked(tail, *lead):
    idx = tuple(lead) + (0,) * len(tail)
    return pl.BlockSpec((None,) * len(lead) + tuple(tail), lambda *_: idx, pipeline_mode=pl.Buffered(1))


def _adaln_kernel(c_ref, w_ref, b_ref, o_ref):
    c = c_ref[...]
    cond = (c * _sigmoid(c)).astype(BF16)
    o_ref[...] = jnp.dot(cond, w_ref[...].astype(BF16), preferred_element_type=F32) + b_ref[...]


def _adaln(c, ada_w, ada_b):
    nl, d, n = ada_w.shape
    bsz = c.shape[0]
    return pl.pallas_call(
        _adaln_kernel,
        out_shape=jax.ShapeDtypeStruct((nl, bsz, n), F32),
        grid=(nl, n // ADA_COL_TILE),
        in_specs=[
            pl.BlockSpec((bsz, d), lambda l, j: (0, 0)),
            pl.BlockSpec((None, d, ADA_COL_TILE), lambda l, j: (l, 0, j)),
            pl.BlockSpec((None, 1, ADA_COL_TILE), lambda l, j: (l, 0, j)),
        ],
        out_specs=pl.BlockSpec((None, bsz, ADA_COL_TILE), lambda l, j: (l, 0, j)),
        compiler_params=pltpu.CompilerParams(vmem_limit_bytes=VMEM_LIMIT),
        name="adaln",
    )(c, ada_w, ada_b.reshape(nl, 1, n))


def _ff_chunks():
    chunks, c0 = [], 0
    while c0 < D_FF:
        cw = min(FF_CHUNK, D_FF - c0)
        chunks.append((c0, cw))
        c0 += cw
    return tuple(chunks)


def _ffn_kernel(x_ref, mod_ref, wg_ref, wu_ref, wd_ref, lg_ref, lb_ref, o_ref, acc_ref, *, sub):
    x = x_ref[...]
    shift = mod_ref[3 * sub + 0:3 * sub + 1, :]
    scale = mod_ref[3 * sub + 1:3 * sub + 2, :]
    gate = mod_ref[3 * sub + 2:3 * sub + 3, :]
    h = (_layernorm(x) * (1.0 + scale) + shift).astype(BF16)
    for idx, (c0, cw) in enumerate(_ff_chunks()):
        g = jnp.dot(h, wg_ref[:, c0:c0 + cw], preferred_element_type=F32)
        u = jnp.dot(h, wu_ref[:, c0:c0 + cw], preferred_element_type=F32)
        a = (g * _sigmoid(g) * u).astype(BF16)
        d = jnp.dot(a, wd_ref[c0:c0 + cw, :], preferred_element_type=F32)
        if idx == 0:
            acc_ref[...] = d
        else:
            acc_ref[...] += d
    y = ALPHA * x + (FFN_RES * gate) * acc_ref[...]
    o_ref[...] = _layernorm(y) * lg_ref[...] + lb_ref[...]


def _mod_spec(l, d):
    return pl.BlockSpec((None, None, 9, d), lambda b, i: (l, b, 0, 0))


def _ffn(x, mod_all, wg, wu, wd, ln_g, ln_b, l, which):
    bsz, s, d = x.shape
    tm = FFN_TOKEN_TILE
    sub = 2 * which
    return pl.pallas_call(
        functools.partial(_ffn_kernel, sub=sub),
        out_shape=jax.ShapeDtypeStruct(x.shape, F32),
        grid=(bsz, s // tm),
        in_specs=[
            pl.BlockSpec((None, tm, d), lambda b, i: (b, i, 0)),
            _mod_spec(l, d),
            _stacked(wg.shape[2:], l, which),
            _stacked(wu.shape[2:], l, which),
            _stacked(wd.shape[2:], l, which),
            _stacked((1, d), l, sub),
            _stacked((1, d), l, sub),
        ],
        out_specs=pl.BlockSpec((None, tm, d), lambda b, i: (b, i, 0)),
        scratch_shapes=[pltpu.VMEM((tm, d), F32)],
        compiler_params=pltpu.CompilerParams(vmem_limit_bytes=VMEM_LIMIT),
        name=f"ffn{sub}",
    )(x, mod_all, wg, wu, wd, ln_g, ln_b)


def _inproj_kernel(x_ref, mod_ref, w_ref, pw_ref, ps_ref, *rest):
    qkv_refs = rest[:3 * len(DILATIONS)]
    us_ref, yp_ref, zs_ref, z4_ref = rest[3 * len(DILATIONS):3 * len(DILATIONS) + 4]
    e_refs = rest[3 * len(DILATIONS) + 4:]
    i = pl.program_id(1)
    tm = x_ref.shape[0]
    x = x_ref[...]
    shift = mod_ref[3:4, :]
    scale = mod_ref[4:5, :]
    h = (_layernorm(x) * (1.0 + scale) + shift).astype(BF16)
    z = jnp.dot(h, w_ref[...], preferred_element_type=F32)
    up = z[:, 3 * D_ATT + D_SSM:]

    slabs_per = D_ATT // LANES
    n_qkv = 3 * slabs_per
    n_slabs = (3 * D_ATT + D_SSM) // LANES
    for c in range(n_slabs):
        col = z[:, c * LANES:(c + 1) * LANES]
        zs_ref[c] = col * (HEAD_DIM ** -0.5 * LOG2E) if c < slabs_per else col
    assert DILATIONS == (1, 4, 16)
    q4rows, q16rows = tm // 4, tm // 16
    for c in range(n_qkv):
        which, lanes = c // slabs_per, slice((c % slabs_per) * LANES, (c % slabs_per + 1) * LANES)
        qkv_refs[which][0, :, lanes] = zs_ref[c].astype(BF16)
        for r4 in range(4):
            blk = zs_ref[c, pl.ds(r4, q4rows, stride=4), :]
            z4_ref[c, r4 * q4rows:(r4 + 1) * q4rows, :] = blk
            qkv_refs[3 + which][r4, :, lanes] = blk.astype(BF16)
        for r4 in range(4):
            for c4 in range(4):
                blk = z4_ref[c, pl.ds(r4 * q4rows + c4, q16rows, stride=4), :]
                qkv_refs[6 + which][r4 + 4 * c4, :, lanes] = blk.astype(BF16)
    for c in range(n_qkv, n_slabs):
        for j in range(SSM_CHUNK):
            lo = j * D_SSM + (c - n_qkv) * LANES
            us_ref[:, lo:lo + LANES] = zs_ref[c, pl.ds(j, tm // SSM_CHUNK, stride=SSM_CHUNK), :]

    e1, e2, e4, e8 = e_refs
    hl = POOL_HALO
    assert POOL_WINDOWS == (2, 4, 8, 16) and hl == 32

    @pl.when(i == 0)
    def _():
        e1[0:hl, :] = jnp.zeros((hl, D_POOL), F32)

    @pl.when(i > 0)
    def _():
        e1[0:hl, :] = e1[tm:tm + hl, :]

    e1[hl:hl + tm, :] = up
    e2[8:, :] = e1[8:, :] + e1[7:tm + hl - 1, :]
    e4[16:, :] = e2[16:, :] + e2[14:tm + hl - 2, :]
    e8[24:, :] = e4[24:, :] + e4[20:tm + hl - 4, :]
    sums = {2: e2[hl:, :], 4: e4[hl:, :], 8: e8[hl:, :], 16: e8[hl:, :] + e8[hl - 8:tm + hl - 8, :]}
    pos = (i * tm + lax.broadcasted_iota(jnp.int32, (tm, 1), 0) + 1).astype(F32)
    group = lax.broadcasted_iota(jnp.int32, (1, D_POOL), 1) // POOL_GROUP
    mean = sums[POOL_WINDOWS[-1]] / jnp.minimum(pos, float(POOL_WINDOWS[-1]))
    for gi in range(len(POOL_WINDOWS) - 2, -1, -1):
        w = POOL_WINDOWS[gi]
        mean = jnp.where(group == gi, sums[w] / jnp.minimum(pos, float(w)), mean)
    pooled = (mean - up).astype(BF16)
    yp = jnp.dot(pooled, pw_ref[...], preferred_element_type=F32) * ps_ref[...]
    yp_ref[...] = yp.astype(BF16)


def _inproj(x, mod_all, w_in, pool_w_bd, pool_scale, l):
    bsz, s, d = x.shape
    tm = TOKEN_TILE
    tok = lambda width: pl.BlockSpec((None, tm, width), lambda b, i: (b, i, 0))
    qkv_shapes, qkv_specs = [], []
    for dil in DILATIONS:
        for _ in range(3):
            qkv_shapes.append(jax.ShapeDtypeStruct((bsz, dil, s // dil, D_ATT), BF16))
            qkv_specs.append(pl.BlockSpec((None, dil, tm // dil, D_ATT), lambda b, i: (b, 0, i, 0)))
    res = pl.pallas_call(
        _inproj_kernel,
        out_shape=tuple(qkv_shapes) + (
            jax.ShapeDtypeStruct((bsz, s // SSM_CHUNK, SSM_CHUNK * D_SSM), F32),
            jax.ShapeDtypeStruct((bsz, s, D_POOL), BF16),
        ),
        grid=(bsz, s // tm),
        in_specs=[
            tok(d),
            _mod_spec(l, d),
            _stacked(w_in.shape[1:], l),
            _resident(pool_w_bd.shape),
            _stacked((1, D_POOL), l),
        ],
        out_specs=tuple(qkv_specs) + (
            pl.BlockSpec((None, tm // SSM_CHUNK, SSM_CHUNK * D_SSM), lambda b, i: (b, i, 0)), tok(D_POOL)),
        scratch_shapes=[pltpu.VMEM(((3 * D_ATT + D_SSM) // LANES, tm, LANES), F32),
                        pltpu.VMEM((3 * D_ATT // LANES, tm, LANES), F32)]
        + [pltpu.VMEM((tm + POOL_HALO, D_POOL), F32) for _ in range(4)],
        compiler_params=pltpu.CompilerParams(
            dimension_semantics=("arbitrary", "arbitrary"), vmem_limit_bytes=VMEM_LIMIT),
        name="inproj",
    )(x, mod_all, w_in, pool_w_bd, pool_scale)
    qkv = [tuple(res[3 * di:3 * di + 3]) for di in range(len(DILATIONS))]
    return qkv, res[-2], res[-1]


def _t5_bucket(dist):
    max_exact = N_BUCKETS // 2
    dd = np.maximum(dist, 1).astype(np.float32)
    large = max_exact + (np.log(dd / max_exact) / math.log(MAX_DISTANCE / max_exact)
                         * (N_BUCKETS - max_exact)).astype(np.int32)
    large = np.minimum(large, N_BUCKETS - 1)
    return np.where(dist < max_exact, dist, large).astype(np.int32)


def _branch_bias(rel_bias, window, dilation, has_prev):
    qb = ATT_BLOCK
    n_keys = window // dilation
    assert n_keys == qb
    period = 3 * qb
    dist = np.arange(n_keys, -1, -1)
    row = rel_bias[_t5_bucket(dist * dilation)].T.astype(F32) * LOG2E
    row = jnp.concatenate([row, jnp.full((N_HEADS, period - n_keys - 1), NEG, F32)], axis=1)
    flat = jnp.tile(row, (1, qb))[:, :qb * (period - 1)]
    bias = flat.reshape(N_HEADS, qb, period - 1)[:, :, :2 * qb]
    if not has_prev:
        return bias[:, :, qb:]
    first = jnp.concatenate([jnp.full((N_HEADS, qb, qb), NEG, F32), bias[:, :, qb:]], axis=2)
    return jnp.stack([bias, first], 0)


def _attn_kernel(*refs, has_prev):
    qb = ATT_BLOCK
    if has_prev:
        q_ref, kc_ref, kh_ref, vc_ref, vh_ref, bias_ref, o_ref, lse_ref, kbuf, vbuf = refs
        nsub = q_ref.shape[0] // qb
        kbuf[0:qb, :] = kh_ref[...]
        kbuf[qb:, :] = kc_ref[...]
        vbuf[0:qb, :] = vh_ref[...]
        vbuf[qb:, :] = vc_ref[...]
    else:
        q_ref, kc_ref, vc_ref, bias_ref, o_ref, lse_ref = refs
        nsub = q_ref.shape[0]

    pair = 2 * HEAD_DIM
    lane = lax.broadcasted_iota(jnp.int32, (1, pair), 1)
    low = lane < HEAD_DIM
    lse_lane = lax.broadcasted_iota(jnp.int32, (1, LANES), 1)

    def block(jj, carry):
        if has_prev:
            row0 = pl.multiple_of(jj * qb, qb)
            first = jnp.logical_and(pl.program_id(2) == 0, jj == 0).astype(jnp.int32)
            q_at = lambda cols: q_ref[pl.ds(row0, qb), cols]
            k_at = lambda cols: kbuf[pl.ds(row0, 2 * qb), cols]
            v_at = lambda cols: vbuf[pl.ds(row0, 2 * qb), cols]
            bias_at = lambda h: bias_ref[first, h]
        else:
            q_at = lambda cols: q_ref[jj, :, cols]
            k_at = lambda cols: kc_ref[jj, :, cols]
            v_at = lambda cols: vc_ref[jj, :, cols]
            bias_at = lambda h: bias_ref[h]

        def pair_cols(h):
            return slice((h // 2) * pair, (h // 2 + 1) * pair)

        def scores(h):
            q2 = q_at(pair_cols(h))
            qh = jnp.where(low, q2, jnp.zeros_like(q2)) if h % 2 == 0 else jnp.where(low, jnp.zeros_like(q2), q2)
            s = lax.dot_general(qh, k_at(pair_cols(h)), (((1,), (1,)), ((), ())), preferred_element_type=F32)
            return s + bias_at(h)

        pending = {h: scores(h) for h in range(min(ATT_LOOKAHEAD, N_HEADS))}
        outs = {}
        m_tile = jnp.zeros((qb, LANES), F32)
        l_tile = jnp.ones((qb, LANES), F32)
        for h in range(N_HEADS):
            s = pending.pop(h)
            m = jnp.max(s, axis=-1, keepdims=True)
            p = jnp.exp2(s - m)
            l = jnp.sum(p, axis=-1, keepdims=True)
            if h + ATT_LOOKAHEAD < N_HEADS:
                pending[h + ATT_LOOKAHEAD] = scores(h + ATT_LOOKAHEAD)
            cols = pair_cols(h)
            pv = jnp.dot(p.astype(BF16), v_at(cols), preferred_element_type=F32)
            outs[h] = pv * (1.0 / l)
            m_tile = jnp.where(lse_lane == h, m, m_tile)
            l_tile = jnp.where(lse_lane == h, l, l_tile)
            if h % 2 == 1:
                o2 = jnp.where(low, outs.pop(h - 1), outs.pop(h)).astype(BF16)
                if has_prev:
                    o_ref[pl.ds(row0, qb), cols] = o2
                else:
                    o_ref[jj, :, cols] = o2
        lse_tile = (m_tile + jnp.log2(l_tile)) * LN2
        if has_prev:
            lse_ref[pl.ds(row0, qb), :] = lse_tile
        else:
            lse_ref[jj] = lse_tile
        return carry

    lax.fori_loop(0, nsub, block, 0, unroll=True)


def _attn_branch(q, k, v, bias):
    bsz, d, ln, _ = q.shape
    qb = ATT_BLOCK
    has_prev = ln > qb
    out_shape = (jax.ShapeDtypeStruct((bsz, d, ln, D_ATT), BF16),
                 jax.ShapeDtypeStruct((bsz, d, ln, LANES), F32))
    if has_prev:
        rows = min(ATT_STEP_BLOCKS * qb, ln)
        per = rows // qb
        cur = lambda width: pl.BlockSpec((None, None, rows, width), lambda b, r, j: (b, r, j, 0))
        halo = pl.BlockSpec((None, None, qb, D_ATT), lambda b, r, j: (b, r, jnp.maximum(j * per - 1, 0), 0))
        grid = (bsz, d, ln // rows)
        args = [q, k, k, v, v, bias]
        specs = [cur(D_ATT), cur(D_ATT), halo, cur(D_ATT), halo, _resident(bias.shape)]
        scratch = [pltpu.VMEM((rows + qb, D_ATT), BF16), pltpu.VMEM((rows + qb, D_ATT), BF16)]
    else:
        per = min(ATT_STEP_BLOCKS, d)
        cur = lambda width: pl.BlockSpec((None, per, qb, width), lambda b, r: (b, r, 0, 0))
        grid = (bsz, d // per)
        args = [q, k, v, bias]
        specs = [cur(D_ATT), cur(D_ATT), cur(D_ATT), _resident(bias.shape)]
        scratch = []
    return pl.pallas_call(
        functools.partial(_attn_kernel, has_prev=has_prev),
        out_shape=out_shape,
        grid=grid,
        in_specs=specs,
        out_specs=(cur(D_ATT), cur(LANES)),
        scratch_shapes=scratch,
        compiler_params=pltpu.CompilerParams(vmem_limit_bytes=VMEM_LIMIT),
        name=f"attn_d{d}",
    )(*args)


def _dilated_attention(qkv, rel_bias):
    outs = []
    for (window, dilation), (q, k, v) in zip(DILATED_PATTERNS, qkv):
        has_prev = q.shape[2] > ATT_BLOCK
        outs.append(_attn_branch(q, k, v, _branch_bias(rel_bias, window, dilation, has_prev)))
    return outs


def _ssm_tables(a_re, a_im, log_dt, b_re, b_im, c_re, c_im, d_skip):
    hi = lax.Precision.HIGHEST
    t, g, p, c = SSM_CHUNK, N_SSM_GROUPS, SSM_STATE, SSM_GROUP
    dt = jnp.exp(log_dt)[:, None]
    mag = jnp.exp(a_re * dt)
    ar, ai = mag * jnp.cos(a_im * dt), mag * jnp.sin(a_im * dt)
    den = a_re * a_re + a_im * a_im
    fr = ((ar - 1.0) * a_re + ai * a_im) / den
    fi = (ai * a_re - (ar - 1.0) * a_im) / den
    bbr = fr[:, :, None] * b_re - fi[:, :, None] * b_im
    bbi = fr[:, :, None] * b_im + fi[:, :, None] * b_re
    pr, pi_ = [jnp.ones_like(ar)], [jnp.zeros_like(ar)]
    for _ in range(t):
        pr.append(pr[-1] * ar - pi_[-1] * ai)
        pi_.append(pr[-2] * ai + pi_[-1] * ar)
    n = t * g * c
    assert n == 2 * g * p and t * c == 2 * p

    def expand(dense, row_div, col_div, spread):
        full = jnp.dot(dense.astype(BF16), spread, preferred_element_type=F32)
        rg = (lax.broadcasted_iota(jnp.int32, (n, n), 0) // row_div) % g
        cg = (lax.broadcasted_iota(jnp.int32, (n, n), 1) // col_div) % g
        return jnp.where(rg == cg, full, 0.0).astype(BF16)

    kk = np.arange(t * c)[:, None]
    cc = np.arange(n)[None, :]
    spread_rp = jnp.asarray((cc // (g * p) == kk // p) & (cc % p == kk % p), BF16)
    spread_jc = jnp.asarray((cc // (g * c) == kk // c) & (cc % c == kk % c), BF16)

    win = []
    for j in range(t):
        qr, qi = pr[t - 1 - j][:, :, None], pi_[t - 1 - j][:, :, None]
        win.append(jnp.stack([qr * bbr - qi * bbi, qr * bbi + qi * bbr], 0))
    win = jnp.stack(win, 0)
    w_in = expand(jnp.transpose(win, (0, 2, 4, 1, 3)).reshape(n, 2 * p), c, p, spread_rp)

    wout, hs = [], []
    for j in range(t + 1):
        wr = c_re * pr[j][:, None, :] - c_im * pi_[j][:, None, :]
        wi = c_re * pi_[j][:, None, :] + c_im * pr[j][:, None, :]
        if j >= 1:
            wout.append(jnp.stack([wr, -wi], 0))
        if j < t:
            hs.append(jnp.einsum('gcp,gpd->gdc', wr, bbr, precision=hi)
                      - jnp.einsum('gcp,gpd->gdc', wi, bbi, precision=hi))
    wout = jnp.stack(wout, 0)
    w_out = expand(jnp.transpose(wout, (1, 2, 4, 0, 3)).reshape(n, t * c), p, c, spread_jc)

    zero_h = jnp.zeros_like(hs[0])
    toep = jnp.stack([jnp.stack([hs[j - jp] if j >= jp else zero_h for j in range(t)], 0)
                      for jp in range(t)], 0)
    w_intra = expand(jnp.transpose(toep, (0, 2, 3, 1, 4)).reshape(n, t * c), c, c, spread_jc)

    abar_t = jnp.concatenate([pr[t].reshape(1, g * p), pi_[t].reshape(1, g * p)], axis=1)
    d_row = jnp.tile(d_skip.reshape(1, D_SSM), (1, t))
    return w_in, w_intra, w_out, abar_t, d_row


def _ssm_kernel(u_ref, win_ref, wintra_ref, wout_ref, abar_ref, d_ref, y_ref, z_ref, xp_ref):
    n = u_ref.shape[0]
    half = N_SSM_GROUPS * SSM_STATE
    u = u_ref[...]
    ub = u.astype(BF16)
    z_ref[...] = jnp.dot(ub, win_ref[...], preferred_element_type=F32)
    ar = abar_ref[:, 0:half]
    ai = abar_ref[:, half:]

    def step(k, carry):
        xr, xi = carry
        xp_ref[pl.ds(k, 1), 0:half] = xr
        xp_ref[pl.ds(k, 1), half:] = xi
        zr = z_ref[pl.ds(k, 1), 0:half]
        zi = z_ref[pl.ds(k, 1), half:]
        return ar * xr - ai * xi + zr, ar * xi + ai * xr + zi

    zero = jnp.zeros((1, half), F32)
    lax.fori_loop(0, n, step, (zero, zero))
    y = jnp.dot(ub, wintra_ref[...], preferred_element_type=F32)
    y = y + jnp.dot(xp_ref[...].astype(BF16), wout_ref[...], preferred_element_type=F32)
    y_ref[...] = y + d_ref[...] * u


def _ssm(u, tables):
    w_in, w_intra, w_out, abar_t, d_row = tables
    bsz, n, width = u.shape
    row = pl.BlockSpec((None, n, width), lambda b: (b, 0, 0))
    return pl.pallas_call(
        _ssm_kernel,
        out_shape=jax.ShapeDtypeStruct((bsz, n, width), F32),
        grid=(bsz,),
        in_specs=[row, _resident(w_in.shape), _resident(w_intra.shape), _resident(w_out.shape),
                  _resident(abar_t.shape), _resident(d_row.shape)],
        out_specs=row,
        scratch_shapes=[pltpu.VMEM((n, w_in.shape[1]), F32), pltpu.VMEM((n, w_in.shape[1]), F32)],
        compiler_params=pltpu.CompilerParams(vmem_limit_bytes=VMEM_LIMIT),
        name="ssm",
    )(u, w_in, w_intra, w_out, abar_t, d_row)


def _outproj_kernel(x_ref, mod_ref, *rest):
    nd = len(DILATIONS)
    o_refs, l_refs = rest[0:nd], rest[nd:2 * nd]
    (ys_ref, yp_ref, gw_ref, gb_ref, wa_ref, ws_ref, wp_ref, lg_ref, lb_ref, o_ref) = rest[2 * nd:2 * nd + 10]
    on4, on16, ot4, ln4, ln16, lt4, ysn_ref, ya_ref = rest[2 * nd + 10:]
    tm = x_ref.shape[0]
    slabs = D_ATT // LANES
    assert DILATIONS == (1, 4, 16)
    r4rows, r16rows = tm // 4, tm // 16

    for j in range(SSM_CHUNK):
        for c in range(D_SSM // LANES):
            lo = j * D_SSM + c * LANES
            ysn_ref[c, pl.ds(j, tm // SSM_CHUNK, stride=SSM_CHUNK), :] = ys_ref[:, lo:lo + LANES]

    for r4 in range(4):
        ln4[pl.ds(r4, r4rows, stride=4), :] = l_refs[1][r4]
        for c4 in range(4):
            lt4[pl.ds(r4 * r4rows + c4, r16rows, stride=4), :] = l_refs[2][r4 + 4 * c4]
        for c in range(slabs):
            lanes = slice(c * LANES, (c + 1) * LANES)
            on4[c, pl.ds(r4, r4rows, stride=4), :] = o_refs[1][r4, :, lanes].astype(F32)
            for c4 in range(4):
                ot4[c, pl.ds(r4 * r4rows + c4, r16rows, stride=4), :] = o_refs[2][r4 + 4 * c4, :, lanes].astype(F32)
    for r4 in range(4):
        ln16[pl.ds(r4, r4rows, stride=4), :] = lt4[r4 * r4rows:(r4 + 1) * r4rows, :]
        for c in range(slabs):
            on16[c, pl.ds(r4, r4rows, stride=4), :] = ot4[c, r4 * r4rows:(r4 + 1) * r4rows, :]

    head = lax.broadcasted_iota(jnp.int32, (LANES, D_ATT), 1) // HEAD_DIM
    spread = (head == lax.broadcasted_iota(jnp.int32, (LANES, D_ATT), 0)).astype(BF16)
    for rc in range(tm // MERGE_ROWS):
        rs = slice(rc * MERGE_ROWS, (rc + 1) * MERGE_ROWS)
        l_nat = [l_refs[0][0, rs, :], ln4[rs, :], ln16[rs, :]]
        o_nat = [o_refs[0][0, rs, :].astype(F32),
                 jnp.concatenate([on4[c, rs, :] for c in range(slabs)], axis=1),
                 jnp.concatenate([on16[c, rs, :] for c in range(slabs)], axis=1)]
        mx = functools.reduce(jnp.maximum, l_nat)
        es = [jnp.exp(lv - mx) for lv in l_nat]
        inv = 1.0 / functools.reduce(lambda a, b: a + b, es)
        ya = o_nat[-1]
        for e, ov in zip(es[:-1], o_nat[:-1]):
            wx = jnp.dot((e * inv).astype(BF16), spread, preferred_element_type=F32)
            ya = ya + wx * (ov - o_nat[-1])
        ya_ref[rs, :] = ya.astype(BF16)

    x = x_ref[...]
    gate = mod_ref[5:6, :]
    ys = jnp.concatenate([ysn_ref[c] for c in range(D_SSM // LANES)], axis=1)
    cdf = 0.5 * (1.0 + jnp.tanh(math.sqrt(2.0 / math.pi) * (ys + 0.044715 * (ys * ys * ys))))
    t = jnp.dot((ys * cdf).astype(BF16), gw_ref[...], preferred_element_type=F32) + gb_ref[...]
    yg = (ys * _sigmoid(t)).astype(BF16)
    y = jnp.dot(ya_ref[...], wa_ref[...], preferred_element_type=F32)
    y = y + jnp.dot(yg, ws_ref[...], preferred_element_type=F32)
    y = y + jnp.dot(yp_ref[...], wp_ref[...], preferred_element_type=F32)
    r = ALPHA * x + gate * y
    o_ref[...] = _layernorm(r) * lg_ref[...] + lb_ref[...]


def _outproj(x, mod_all, att, y_ssm, y_pool, glu_w, glu_b, w_out, ln_g, ln_b, l):
    bsz, s, d = x.shape
    tm = TOKEN_TILE
    tok = lambda width: pl.BlockSpec((None, tm, width), lambda b, i: (b, i, 0))
    res = lambda dil, width: pl.BlockSpec((None, dil, tm // dil, width), lambda b, i: (b, 0, i, 0))
    rows_of = lambda r0, n: pl.BlockSpec((None, n, d), lambda *_: (l, r0 // n, 0), pipeline_mode=pl.Buffered(1))
    assert D_ATT % D_SSM == 0 and D_SSM == D_POOL
    slab = lambda n: pltpu.VMEM((n, tm, LANES), F32)
    scratch = [slab(D_ATT // LANES)] * 3 + [pltpu.VMEM((tm, LANES), F32)] * 3
    scratch += [slab(D_SSM // LANES), pltpu.VMEM((tm, D_ATT), BF16)]
    return pl.pallas_call(
        _outproj_kernel,
        out_shape=jax.ShapeDtypeStruct(x.shape, F32),
        grid=(bsz, s // tm),
        in_specs=[tok(d), _mod_spec(l, d)]
        + [res(dil, D_ATT) for dil in DILATIONS]
        + [res(dil, LANES) for dil in DILATIONS]
        + [pl.BlockSpec((None, tm // SSM_CHUNK, SSM_CHUNK * D_SSM), lambda b, i: (b, i, 0)), tok(D_POOL),
           _stacked(glu_w.shape[1:], l), _stacked((1, D_SSM), l),
           rows_of(0, D_ATT), rows_of(D_ATT, D_SSM), rows_of(D_ATT + D_SSM, D_POOL),
           _stacked((1, d), l, 1), _stacked((1, d), l, 1)],
        out_specs=tok(d),
        scratch_shapes=scratch,
        compiler_params=pltpu.CompilerParams(vmem_limit_bytes=VMEM_LIMIT),
        name="outproj",
    )(x, mod_all, *[o for o, _ in att], *[lse for _, lse in att], y_ssm, y_pool,
      glu_w, glu_b, w_out, w_out, w_out, ln_g, ln_b)


def _block_diag(w):
    g, n, _ = w.shape
    return jnp.einsum('gab,gh->gahb', w, jnp.eye(g, dtype=w.dtype)).reshape(g * n, g * n)


def kernel(x, c, rel_bias, ada_w, ada_b, ln_g, ln_b, ffn_w_gate, ffn_w_up, ffn_w_down, w_in, w_out,
           ssm_a_re, ssm_a_im, ssm_log_dt, ssm_b_re, ssm_b_im, ssm_c_re, ssm_c_im, ssm_d, glu_w, glu_b,
           pool_w, pool_scale):
    bsz = x.shape[0]
    mod_all = _adaln(c, ada_w, ada_b).reshape(DEPTH, bsz, 9, D_MODEL)
    wg, wu, wd = ffn_w_gate.astype(BF16), ffn_w_up.astype(BF16), ffn_w_down.astype(BF16)
    w_in_b, w_out_b, glu_w_b = w_in.astype(BF16), w_out.astype(BF16), glu_w.astype(BF16)
    ln_g4, ln_b4 = ln_g.reshape(DEPTH, 3, 1, D_MODEL), ln_b.reshape(DEPTH, 3, 1, D_MODEL)
    glu_b3 = glu_b.reshape(DEPTH, 1, D_SSM)
    pool_scale3 = pool_scale.reshape(DEPTH, 1, D_POOL)
    for l in range(DEPTH):
        x = _ffn(x, mod_all, wg, wu, wd, ln_g4, ln_b4, l, 0)
        qkv, u_ssm, y_pool = _inproj(x, mod_all, w_in_b, _block_diag(pool_w[l]).astype(BF16), pool_scale3, l)
        att = _dilated_attention(qkv, rel_bias)
        tables = _ssm_tables(ssm_a_re[l], ssm_a_im[l], ssm_log_dt[l], ssm_b_re[l], ssm_b_im[l],
                             ssm_c_re[l], ssm_c_im[l], ssm_d[l])
        y_ssm = _ssm(u_ssm, tables)
        x = _outproj(x, mod_all, att, y_ssm, y_pool, glu_w_b, glu_b3, w_out_b, ln_g4, ln_b4, l)
        x = _ffn(x, mod_all, wg, wu, wd, ln_g4, ln_b4, l, 1)
    return x
```

```python
import functools
import math

import jax
import jax.numpy as jnp
import numpy as np
from jax import lax
from jax.experimental import pallas as pl
from jax.experimental.pallas import tpu as pltpu

F32 = jnp.float32
BF16 = jnp.bfloat16

D_MODEL = 1024
DEPTH = 2
HEAD_DIM = 64
N_HEADS = 8
D_ATT = N_HEADS * HEAD_DIM
DILATED_PATTERNS = ((128, 1), (512, 4), (2048, 16))
DILATIONS = tuple(d for _, d in DILATED_PATTERNS)
ATT_BLOCK = 128
SSM_GROUP = 16
D_SSM = 256
N_SSM_GROUPS = D_SSM // SSM_GROUP
SSM_STATE = 64
POOL_WINDOWS = (2, 4, 8, 16)
D_POOL = 256
POOL_GROUP = D_POOL // len(POOL_WINDOWS)
D_IN = 3 * D_ATT + D_SSM + D_POOL
D_FF = 2816
N_BUCKETS = 32
MAX_DISTANCE = 2048
ALPHA = (2 * DEPTH) ** 0.25
FFN_RES = 0.5
LN_EPS = 1e-5
NEG = -1e30

LANES = 128
TOKEN_TILE = 512
FF_CHUNK = 512
FFN_TOKEN_TILE = 512
ADA_COL_TILE = 1152
SSM_CHUNK = 8
POOL_HALO = 2 * max(POOL_WINDOWS)
ATT_LOOKAHEAD = 8
ATT_STEP_BLOCKS = 8
MERGE_ROWS = 128
LOG2E = math.log2(math.e)
LN2 = math.log(2.0)
VMEM_LIMIT = 56 * 1024 * 1024


def _sigmoid(x):
    return 1.0 / (1.0 + jnp.exp(-x))


def _layernorm(x):
    mu = jnp.mean(x, axis=-1, keepdims=True)
    xc = x - mu
    var = jnp.mean(xc * xc, axis=-1, keepdims=True)
    return xc * lax.rsqrt(var + LN_EPS)


def _resident(shape):
    zeros = (0,) * len(shape)
    return pl.BlockSpec(shape, lambda *_: zeros, pipeline_mode=pl.Buffered(1))


def _stacked(tail, *lead):
    idx = tuple(lead) + (0,) * len(tail)
    return pl.BlockSpec((None,) * len(lead) + tuple(tail), lambda *_: idx, pipeline_mode=pl.Buffered(1))


def _adaln_kernel(c_ref, w_ref, b_ref, o_ref):
    c = c_ref[...]
    cond = (c * _sigmoid(c)).astype(BF16)
    o_ref[...] = jnp.dot(cond, w_ref[...].astype(BF16), preferred_element_type=F32) + b_ref[...]


def _adaln(c, ada_w, ada_b):
    nl, d, n = ada_w.shape
    bsz = c.shape[0]
    return pl.pallas_call(
        _adaln_kernel,
        out_shape=jax.ShapeDtypeStruct((nl, bsz, n), F32),
        grid=(nl, n // ADA_COL_TILE),
        in_specs=[
            pl.BlockSpec((bsz, d), lambda l, j: (0, 0)),
            pl.BlockSpec((None, d, ADA_COL_TILE), lambda l, j: (l, 0, j)),
            pl.BlockSpec((None, 1, ADA_COL_TILE), lambda l, j: (l, 0, j)),
        ],
        out_specs=pl.BlockSpec((None, bsz, ADA_COL_TILE), lambda l, j: (l, 0, j)),
        compiler_params=pltpu.CompilerParams(vmem_limit_bytes=VMEM_LIMIT),
        name="adaln",
    )(c, ada_w, ada_b.reshape(nl, 1, n))


def _ff_chunks():
    chunks, c0 = [], 0
    while c0 < D_FF:
        cw = min(FF_CHUNK, D_FF - c0)
        chunks.append((c0, cw))
        c0 += cw
    return tuple(chunks)


def _ffn_kernel(x_ref, mod_ref, wg_ref, wu_ref, wd_ref, lg_ref, lb_ref, o_ref, acc_ref, *, sub):
    x = x_ref[...]
    shift = mod_ref[3 * sub + 0:3 * sub + 1, :]
    scale = mod_ref[3 * sub + 1:3 * sub + 2, :]
    gate = mod_ref[3 * sub + 2:3 * sub + 3, :]
    h = (_layernorm(x) * (1.0 + scale) + shift).astype(BF16)
    for idx, (c0, cw) in enumerate(_ff_chunks()):
        g = jnp.dot(h, wg_ref[:, c0:c0 + cw].astype(BF16), preferred_element_type=F32)
        u = jnp.dot(h, wu_ref[:, c0:c0 + cw].astype(BF16), preferred_element_type=F32)
        a = (g * _sigmoid(g) * u).astype(BF16)
        d = jnp.dot(a, wd_ref[c0:c0 + cw, :].astype(BF16), preferred_element_type=F32)
        if idx == 0:
            acc_ref[...] = d
        else:
            acc_ref[...] += d
    y = ALPHA * x + (FFN_RES * gate) * acc_ref[...]
    o_ref[...] = _layernorm(y) * lg_ref[...] + lb_ref[...]


def _mod_spec(l, d):
    return pl.BlockSpec((None, None, 9, d), lambda b, i: (l, b, 0, 0))


def _ffn(x, mod_all, wg, wu, wd, ln_g, ln_b, l, which):
    bsz, s, d = x.shape
    tm = FFN_TOKEN_TILE
    sub = 2 * which
    return pl.pallas_call(
        functools.partial(_ffn_kernel, sub=sub),
        out_shape=jax.ShapeDtypeStruct(x.shape, F32),
        grid=(bsz, s // tm),
        in_specs=[
            pl.BlockSpec((None, tm, d), lambda b, i: (b, i, 0)),
            _mod_spec(l, d),
            _stacked(wg.shape[2:], l, which),
            _stacked(wu.shape[2:], l, which),
            _stacked(wd.shape[2:], l, which),
            _stacked((1, d), l, sub),
            _stacked((1, d), l, sub),
        ],
        out_specs=pl.BlockSpec((None, tm, d), lambda b, i: (b, i, 0)),
        scratch_shapes=[pltpu.VMEM((tm, d), F32)],
        compiler_params=pltpu.CompilerParams(vmem_limit_bytes=VMEM_LIMIT),
        name=f"ffn{sub}",
    )(x, mod_all, wg, wu, wd, ln_g, ln_b)


def _inproj_kernel(x_ref, mod_ref, w_ref, pw_ref, ps_ref, *rest):
    qkv_refs = rest[:3 * len(DILATIONS)]
    us_ref, yp_ref, zs_ref, z4_ref = rest[3 * len(DILATIONS):3 * len(DILATIONS) + 4]
    e_refs = rest[3 * len(DILATIONS) + 4:]
    i = pl.program_id(1)
    tm = x_ref.shape[0]
    x = x_ref[...]
    shift = mod_ref[3:4, :]
    scale = mod_ref[4:5, :]
    h = (_layernorm(x) * (1.0 + scale) + shift).astype(BF16)
    z = jnp.dot(h, w_ref[...], preferred_element_type=F32)
    up = z[:, 3 * D_ATT + D_SSM:]

    slabs_per = D_ATT // LANES
    n_qkv = 3 * slabs_per
    n_slabs = (3 * D_ATT + D_SSM) // LANES
    for c in range(n_slabs):
        col = z[:, c * LANES:(c + 1) * LANES]
        zs_ref[c] = col * (HEAD_DIM ** -0.5 * LOG2E) if c < slabs_per else col
    assert DILATIONS == (1, 4, 16)
    q4rows, q16rows = tm // 4, tm // 16
    for c in range(n_qkv):
        which, lanes = c // slabs_per, slice((c % slabs_per) * LANES, (c % slabs_per + 1) * LANES)
        qkv_refs[which][0, :, lanes] = zs_ref[c].astype(BF16)
        for r4 in range(4):
            blk = zs_ref[c, pl.ds(r4, q4rows, stride=4), :]
            z4_ref[c, r4 * q4rows:(r4 + 1) * q4rows, :] = blk
            qkv_refs[3 + which][r4, :, lanes] = blk.astype(BF16)
        for r4 in range(4):
            for c4 in range(4):
                blk = z4_ref[c, pl.ds(r4 * q4rows + c4, q16rows, stride=4), :]
                qkv_refs[6 + which][r4 + 4 * c4, :, lanes] = blk.astype(BF16)
    for c in range(n_qkv, n_slabs):
        for j in range(SSM_CHUNK):
            lo = j * D_SSM + (c - n_qkv) * LANES
            us_ref[:, lo:lo + LANES] = zs_ref[c, pl.ds(j, tm // SSM_CHUNK, stride=SSM_CHUNK), :]

    e1, e2, e4, e8 = e_refs
    hl = POOL_HALO
    assert POOL_WINDOWS == (2, 4, 8, 16) and hl == 32

    @pl.when(i == 0)
    def _():
        e1[0:hl, :] = jnp.zeros((hl, D_POOL), F32)

    @pl.when(i > 0)
    def _():
        e1[0:hl, :] = e1[tm:tm + hl, :]

    e1[hl:hl + tm, :] = up
    e2[8:, :] = e1[8:, :] + e1[7:tm + hl - 1, :]
    e4[16:, :] = e2[16:, :] + e2[14:tm + hl - 2, :]
    e8[24:, :] = e4[24:, :] + e4[20:tm + hl - 4, :]
    sums = {2: e2[hl:, :], 4: e4[hl:, :], 8: e8[hl:, :], 16: e8[hl:, :] + e8[hl - 8:tm + hl - 8, :]}
    pos = (i * tm + lax.broadcasted_iota(jnp.int32, (tm, 1), 0) + 1).astype(F32)
    group = lax.broadcasted_iota(jnp.int32, (1, D_POOL), 1) // POOL_GROUP
    mean = sums[POOL_WINDOWS[-1]] / jnp.minimum(pos, float(POOL_WINDOWS[-1]))
    for gi in range(len(POOL_WINDOWS) - 2, -1, -1):
        w = POOL_WINDOWS[gi]
        mean = jnp.where(group == gi, sums[w] / jnp.minimum(pos, float(w)), mean)
    pooled = (mean - up).astype(BF16)
    yp = jnp.dot(pooled, pw_ref[...], preferred_element_type=F32) * ps_ref[...]
    yp_ref[...] = yp.astype(BF16)


def _inproj(x, mod_all, w_in, pool_w_bd, pool_scale, l):
    bsz, s, d = x.shape
    tm = TOKEN_TILE
    tok = lambda width: pl.BlockSpec((None, tm, width), lambda b, i: (b, i, 0))
    qkv_shapes, qkv_specs = [], []
    for dil in DILATIONS:
        for _ in range(3):
            qkv_shapes.append(jax.ShapeDtypeStruct((bsz, dil, s // dil, D_ATT), BF16))
            qkv_specs.append(pl.BlockSpec((None, dil, tm // dil, D_ATT), lambda b, i: (b, 0, i, 0)))
    res = pl.pallas_call(
        _inproj_kernel,
        out_shape=tuple(qkv_shapes) + (
            jax.ShapeDtypeStruct((bsz, s // SSM_CHUNK, SSM_CHUNK * D_SSM), F32),
            jax.ShapeDtypeStruct((bsz, s, D_POOL), BF16),
        ),
        grid=(bsz, s // tm),
        in_specs=[
            tok(d),
            _mod_spec(l, d),
            _stacked(w_in.shape[1:], l),
            _resident(pool_w_bd.shape),
            _stacked((1, D_POOL), l),
        ],
        out_specs=tuple(qkv_specs) + (
            pl.BlockSpec((None, tm // SSM_CHUNK, SSM_CHUNK * D_SSM), lambda b, i: (b, i, 0)), tok(D_POOL)),
        scratch_shapes=[pltpu.VMEM(((3 * D_ATT + D_SSM) // LANES, tm, LANES), F32),
                        pltpu.VMEM((3 * D_ATT // LANES, tm, LANES), F32)]
        + [pltpu.VMEM((tm + POOL_HALO, D_POOL), F32) for _ in range(4)],
        compiler_params=pltpu.CompilerParams(
            dimension_semantics=("arbitrary", "arbitrary"), vmem_limit_bytes=VMEM_LIMIT),
        name="inproj",
    )(x, mod_all, w_in, pool_w_bd, pool_scale)
    qkv = [tuple(res[3 * di:3 * di + 3]) for di in range(len(DILATIONS))]
    return qkv, res[-2], res[-1]


def _t5_bucket(dist):
    max_exact = N_BUCKETS // 2
    dd = np.maximum(dist, 1).astype(np.float32)
    large = max_exact + (np.log(dd / max_exact) / math.log(MAX_DISTANCE / max_exact)
                         * (N_BUCKETS - max_exact)).astype(np.int32)
    large = np.minimum(large, N_BUCKETS - 1)
    return np.where(dist < max_exact, dist, large).astype(np.int32)


def _branch_bias(rel_bias, window, dilation, has_prev):
    qb = ATT_BLOCK
    n_keys = window // dilation
    assert n_keys == qb
    period = 3 * qb
    dist = np.arange(n_keys, -1, -1)
    row = rel_bias[_t5_bucket(dist * dilation)].T.astype(F32) * LOG2E
    row = jnp.concatenate([row, jnp.full((N_HEADS, period - n_keys - 1), NEG, F32)], axis=1)
    flat = jnp.tile(row, (1, qb))[:, :qb * (period - 1)]
    bias = flat.reshape(N_HEADS, qb, period - 1)[:, :, :2 * qb]
    if not has_prev:
        return bias[:, :, qb:]
    first = jnp.concatenate([jnp.full((N_HEADS, qb, qb), NEG, F32), bias[:, :, qb:]], axis=2)
    return jnp.stack([bias, first], 0)


def _attn_kernel(*refs, has_prev):
    qb = ATT_BLOCK
    if has_prev:
        q_ref, kc_ref, kh_ref, vc_ref, vh_ref, bias_ref, o_ref, lse_ref, kbuf, vbuf = refs
        nsub = q_ref.shape[0] // qb
        kbuf[0:qb, :] = kh_ref[...]
        kbuf[qb:, :] = kc_ref[...]
        vbuf[0:qb, :] = vh_ref[...]
        vbuf[qb:, :] = vc_ref[...]
    else:
        q_ref, kc_ref, vc_ref, bias_ref, o_ref, lse_ref = refs
        nsub = q_ref.shape[0]

    pair = 2 * HEAD_DIM
    lane = lax.broadcasted_iota(jnp.int32, (1, pair), 1)
    low = lane < HEAD_DIM
    lse_lane = lax.broadcasted_iota(jnp.int32, (1, LANES), 1)

    def block(jj, carry):
        if has_prev:
            row0 = pl.multiple_of(jj * qb, qb)
            first = jnp.logical_and(pl.program_id(2) == 0, jj == 0).astype(jnp.int32)
            q_at = lambda cols: q_ref[pl.ds(row0, qb), cols]
            k_at = lambda cols: kbuf[pl.ds(row0, 2 * qb), cols]
            v_at = lambda cols: vbuf[pl.ds(row0, 2 * qb), cols]
            bias_at = lambda h: bias_ref[first, h]
        else:
            q_at = lambda cols: q_ref[jj, :, cols]
            k_at = lambda cols: kc_ref[jj, :, cols]
            v_at = lambda cols: vc_ref[jj, :, cols]
            bias_at = lambda h: bias_ref[h]

        def pair_cols(h):
            return slice((h // 2) * pair, (h // 2 + 1) * pair)

        def scores(h):
            q2 = q_at(pair_cols(h))
            qh = jnp.where(low, q2, jnp.zeros_like(q2)) if h % 2 == 0 else jnp.where(low, jnp.zeros_like(q2), q2)
            s = lax.dot_general(qh, k_at(pair_cols(h)), (((1,), (1,)), ((), ())), preferred_element_type=F32)
            return s + bias_at(h)

        pending = {h: scores(h) for h in range(min(ATT_LOOKAHEAD, N_HEADS))}
        outs = {}
        m_tile = jnp.zeros((qb, LANES), F32)
        l_tile = jnp.ones((qb, LANES), F32)
        for h in range(N_HEADS):
            s = pending.pop(h)
            m = jnp.max(s, axis=-1, keepdims=True)
            p = jnp.exp2(s - m)
            l = jnp.sum(p, axis=-1, keepdims=True)
            if h + ATT_LOOKAHEAD < N_HEADS:
                pending[h + ATT_LOOKAHEAD] = scores(h + ATT_LOOKAHEAD)
            cols = pair_cols(h)
            pv = jnp.dot(p.astype(BF16), v_at(cols), preferred_element_type=F32)
            outs[h] = pv * (1.0 / l)
            m_tile = jnp.where(lse_lane == h, m, m_tile)
            l_tile = jnp.where(lse_lane == h, l, l_tile)
            if h % 2 == 1:
                o2 = jnp.where(low, outs.pop(h - 1), outs.pop(h)).astype(BF16)
                if has_prev:
                    o_ref[pl.ds(row0, qb), cols] = o2
                else:
                    o_ref[jj, :, cols] = o2
        lse_tile = (m_tile + jnp.log2(l_tile)) * LN2
        if has_prev:
            lse_ref[pl.ds(row0, qb), :] = lse_tile
        else:
            lse_ref[jj] = lse_tile
        return carry

    lax.fori_loop(0, nsub, block, 0, unroll=True)


def _attn_branch(q, k, v, bias):
    bsz, d, ln, _ = q.shape
    qb = ATT_BLOCK
    has_prev = ln > qb
    out_shape = (jax.ShapeDtypeStruct((bsz, d, ln, D_ATT), BF16),
                 jax.ShapeDtypeStruct((bsz, d, ln, LANES), F32))
    if has_prev:
        rows = min(ATT_STEP_BLOCKS * qb, ln)
        per = rows // qb
        cur = lambda width: pl.BlockSpec((None, None, rows, width), lambda b, r, j: (b, r, j, 0))
        halo = pl.BlockSpec((None, None, qb, D_ATT), lambda b, r, j: (b, r, jnp.maximum(j * per - 1, 0), 0))
        grid = (bsz, d, ln // rows)
        args = [q, k, k, v, v, bias]
        specs = [cur(D_ATT), cur(D_ATT), halo, cur(D_ATT), halo, _resident(bias.shape)]
        scratch = [pltpu.VMEM((rows + qb, D_ATT), BF16), pltpu.VMEM((rows + qb, D_ATT), BF16)]
    else:
        per = min(ATT_STEP_BLOCKS, d)
        cur = lambda width: pl.BlockSpec((None, per, qb, width), lambda b, r: (b, r, 0, 0))
        grid = (bsz, d // per)
        args = [q, k, v, bias]
        specs = [cur(D_ATT), cur(D_ATT), cur(D_ATT), _resident(bias.shape)]
        scratch = []
    return pl.pallas_call(
        functools.partial(_attn_kernel, has_prev=has_prev),
        out_shape=out_shape,
        grid=grid,
        in_specs=specs,
        out_specs=(cur(D_ATT), cur(LANES)),
        scratch_shapes=scratch,
        compiler_params=pltpu.CompilerParams(vmem_limit_bytes=VMEM_LIMIT),
        name=f"attn_d{d}",
    )(*args)


def _dilated_attention(qkv, rel_bias):
    outs = []
    for (window, dilation), (q, k, v) in zip(DILATED_PATTERNS, qkv):
        has_prev = q.shape[2] > ATT_BLOCK
        outs.append(_attn_branch(q, k, v, _branch_bias(rel_bias, window, dilation, has_prev)))
    return outs


def _ssm_tables(a_re, a_im, log_dt, b_re, b_im, c_re, c_im, d_skip):
    hi = lax.Precision.HIGHEST
    t, g, p, c = SSM_CHUNK, N_SSM_GROUPS, SSM_STATE, SSM_GROUP
    dt = jnp.exp(log_dt)[:, None]
    mag = jnp.exp(a_re * dt)
    ar, ai = mag * jnp.cos(a_im * dt), mag * jnp.sin(a_im * dt)
    den = a_re * a_re + a_im * a_im
    fr = ((ar - 1.0) * a_re + ai * a_im) / den
    fi = (ai * a_re - (ar - 1.0) * a_im) / den
    bbr = fr[:, :, None] * b_re - fi[:, :, None] * b_im
    bbi = fr[:, :, None] * b_im + fi[:, :, None] * b_re
    pr, pi_ = [jnp.ones_like(ar)], [jnp.zeros_like(ar)]
    for _ in range(t):
        pr.append(pr[-1] * ar - pi_[-1] * ai)
        pi_.append(pr[-2] * ai + pi_[-1] * ar)
    n = t * g * c
    assert n == 2 * g * p and t * c == 2 * p

    def expand(dense, row_div, col_div, spread):
        full = jnp.dot(dense.astype(BF16), spread, preferred_element_type=F32)
        rg = (lax.broadcasted_iota(jnp.int32, (n, n), 0) // row_div) % g
        cg = (lax.broadcasted_iota(jnp.int32, (n, n), 1) // col_div) % g
        return jnp.where(rg == cg, full, 0.0).astype(BF16)

    kk = np.arange(t * c)[:, None]
    cc = np.arange(n)[None, :]
    spread_rp = jnp.asarray((cc // (g * p) == kk // p) & (cc % p == kk % p), BF16)
    spread_jc = jnp.asarray((cc // (g * c) == kk // c) & (cc % c == kk % c), BF16)

    win = []
    for j in range(t):
        qr, qi = pr[t - 1 - j][:, :, None], pi_[t - 1 - j][:, :, None]
        win.append(jnp.stack([qr * bbr - qi * bbi, qr * bbi + qi * bbr], 0))
    win = jnp.stack(win, 0)
    w_in = expand(jnp.transpose(win, (0, 2, 4, 1, 3)).reshape(n, 2 * p), c, p, spread_rp)

    wout, hs = [], []
    for j in range(t + 1):
        wr = c_re * pr[j][:, None, :] - c_im * pi_[j][:, None, :]
        wi = c_re * pi_[j][:, None, :] + c_im * pr[j][:, None, :]
        if j >= 1:
            wout.append(jnp.stack([wr, -wi], 0))
        if j < t:
            hs.append(jnp.einsum('gcp,gpd->gdc', wr, bbr, precision=hi)
                      - jnp.einsum('gcp,gpd->gdc', wi, bbi, precision=hi))
    wout = jnp.stack(wout, 0)
    w_out = expand(jnp.transpose(wout, (1, 2, 4, 0, 3)).reshape(n, t * c), p, c, spread_jc)

    zero_h = jnp.zeros_like(hs[0])
    toep = jnp.stack([jnp.stack([hs[j - jp] if j >= jp else zero_h for j in range(t)], 0)
                      for jp in range(t)], 0)
    w_intra = expand(jnp.transpose(toep, (0, 2, 3, 1, 4)).reshape(n, t * c), c, c, spread_jc)

    abar_t = jnp.concatenate([pr[t].reshape(1, g * p), pi_[t].reshape(1, g * p)], axis=1)
    d_row = jnp.tile(d_skip.reshape(1, D_SSM), (1, t))
    return w_in, w_intra, w_out, abar_t, d_row


def _ssm_kernel(u_ref, win_ref, wintra_ref, wout_ref, abar_ref, d_ref, y_ref, z_ref, xp_ref):
    n = u_ref.shape[0]
    half = N_SSM_GROUPS * SSM_STATE
    u = u_ref[...]
    ub = u.astype(BF16)
    z_ref[...] = jnp.dot(ub, win_ref[...], preferred_element_type=F32)
    ar = abar_ref[:, 0:half]
    ai = abar_ref[:, half:]

    def step(k, carry):
        xr, xi = carry
        xp_ref[pl.ds(k, 1), 0:half] = xr
        xp_ref[pl.ds(k, 1), half:] = xi
        zr = z_ref[pl.ds(k, 1), 0:half]
        zi = z_ref[pl.ds(k, 1), half:]
        return ar * xr - ai * xi + zr, ar * xi + ai * xr + zi

    zero = jnp.zeros((1, half), F32)
    lax.fori_loop(0, n, step, (zero, zero))
    y = jnp.dot(ub, wintra_ref[...], preferred_element_type=F32)
    y = y + jnp.dot(xp_ref[...].astype(BF16), wout_ref[...], preferred_element_type=F32)
    y_ref[...] = y + d_ref[...] * u


def _ssm(u, tables):
    w_in, w_intra, w_out, abar_t, d_row = tables
    bsz, n, width = u.shape
    row = pl.BlockSpec((None, n, width), lambda b: (b, 0, 0))
    return pl.pallas_call(
        _ssm_kernel,
        out_shape=jax.ShapeDtypeStruct((bsz, n, width), F32),
        grid=(bsz,),
        in_specs=[row, _resident(w_in.shape), _resident(w_intra.shape), _resident(w_out.shape),
                  _resident(abar_t.shape), _resident(d_row.shape)],
        out_specs=row,
        scratch_shapes=[pltpu.VMEM((n, w_in.shape[1]), F32), pltpu.VMEM((n, w_in.shape[1]), F32)],
        compiler_params=pltpu.CompilerParams(vmem_limit_bytes=VMEM_LIMIT),
        name="ssm",
    )(u, w_in, w_intra, w_out, abar_t, d_row)


def _outproj_kernel(x_ref, mod_ref, *rest):
    nd = len(DILATIONS)
    o_refs, l_refs = rest[0:nd], rest[nd:2 * nd]
    (ys_ref, yp_ref, gw_ref, gb_ref, wa_ref, ws_ref, wp_ref, lg_ref, lb_ref, o_ref) = rest[2 * nd:2 * nd + 10]
    on4, on16, ot4, ln4, ln16, lt4, ysn_ref, ya_ref = rest[2 * nd + 10:]
    tm = x_ref.shape[0]
    slabs = D_ATT // LANES
    assert DILATIONS == (1, 4, 16)
    r4rows, r16rows = tm // 4, tm // 16

    for j in range(SSM_CHUNK):
        for c in range(D_SSM // LANES):
            lo = j * D_SSM + c * LANES
            ysn_ref[c, pl.ds(j, tm // SSM_CHUNK, stride=SSM_CHUNK), :] = ys_ref[:, lo:lo + LANES]

    for r4 in range(4):
        ln4[pl.ds(r4, r4rows, stride=4), :] = l_refs[1][r4]
        for c4 in range(4):
            lt4[pl.ds(r4 * r4rows + c4, r16rows, stride=4), :] = l_refs[2][r4 + 4 * c4]
        for c in range(slabs):
            lanes = slice(c * LANES, (c + 1) * LANES)
            on4[c, pl.ds(r4, r4rows, stride=4), :] = o_refs[1][r4, :, lanes].astype(F32)
            for c4 in range(4):
                ot4[c, pl.ds(r4 * r4rows + c4, r16rows, stride=4), :] = o_refs[2][r4 + 4 * c4, :, lanes].astype(F32)
    for r4 in range(4):
        ln16[pl.ds(r4, r4rows, stride=4), :] = lt4[r4 * r4rows:(r4 + 1) * r4rows, :]
        for c in range(slabs):
            on16[c, pl.ds(r4, r4rows, stride=4), :] = ot4[c, r4 * r4rows:(r4 + 1) * r4rows, :]

    head = lax.broadcasted_iota(jnp.int32, (LANES, D_ATT), 1) // HEAD_DIM
    spread = (head == lax.broadcasted_iota(jnp.int32, (LANES, D_ATT), 0)).astype(BF16)
    for rc in range(tm // MERGE_ROWS):
        rs = slice(rc * MERGE_ROWS, (rc + 1) * MERGE_ROWS)
        l_nat = [l_refs[0][0, rs, :], ln4[rs, :], ln16[rs, :]]
        o_nat = [o_refs[0][0, rs, :].astype(F32),
                 jnp.concatenate([on4[c, rs, :] for c in range(slabs)], axis=1),
                 jnp.concatenate([on16[c, rs, :] for c in range(slabs)], axis=1)]
        mx = functools.reduce(jnp.maximum, l_nat)
        es = [jnp.exp(lv - mx) for lv in l_nat]
        inv = 1.0 / functools.reduce(lambda a, b: a + b, es)
        ya = o_nat[-1]
        for e, ov in zip(es[:-1], o_nat[:-1]):
            wx = jnp.dot((e * inv).astype(BF16), spread, preferred_element_type=F32)
            ya = ya + wx * (ov - o_nat[-1])
        ya_ref[rs, :] = ya.astype(BF16)

    x = x_ref[...]
    gate = mod_ref[5:6, :]
    ys = jnp.concatenate([ysn_ref[c] for c in range(D_SSM // LANES)], axis=1)
    cdf = 0.5 * (1.0 + jnp.tanh(math.sqrt(2.0 / math.pi) * (ys + 0.044715 * (ys * ys * ys))))
    t = jnp.dot((ys * cdf).astype(BF16), gw_ref[...], preferred_element_type=F32) + gb_ref[...]
    yg = (ys * _sigmoid(t)).astype(BF16)
    y = jnp.dot(ya_ref[...], wa_ref[...], preferred_element_type=F32)
    y = y + jnp.dot(yg, ws_ref[...], preferred_element_type=F32)
    y = y + jnp.dot(yp_ref[...], wp_ref[...], preferred_element_type=F32)
    r = ALPHA * x + gate * y
    o_ref[...] = _layernorm(r) * lg_ref[...] + lb_ref[...]


def _outproj(x, mod_all, att, y_ssm, y_pool, glu_w, glu_b, w_out, ln_g, ln_b, l):
    bsz, s, d = x.shape
    tm = TOKEN_TILE
    tok = lambda width: pl.BlockSpec((None, tm, width), lambda b, i: (b, i, 0))
    res = lambda dil, width: pl.BlockSpec((None, dil, tm // dil, width), lambda b, i: (b, 0, i, 0))
    rows_of = lambda r0, n: pl.BlockSpec((None, n, d), lambda *_: (l, r0 // n, 0), pipeline_mode=pl.Buffered(1))
    assert D_ATT % D_SSM == 0 and D_SSM == D_POOL
    slab = lambda n: pltpu.VMEM((n, tm, LANES), F32)
    scratch = [slab(D_ATT // LANES)] * 3 + [pltpu.VMEM((tm, LANES), F32)] * 3
    scratch += [slab(D_SSM // LANES), pltpu.VMEM((tm, D_ATT), BF16)]
    return pl.pallas_call(
        _outproj_kernel,
        out_shape=jax.ShapeDtypeStruct(x.shape, F32),
        grid=(bsz, s // tm),
        in_specs=[tok(d), _mod_spec(l, d)]
        + [res(dil, D_ATT) for dil in DILATIONS]
        + [res(dil, LANES) for dil in DILATIONS]
        + [pl.BlockSpec((None, tm // SSM_CHUNK, SSM_CHUNK * D_SSM), lambda b, i: (b, i, 0)), tok(D_POOL),
           _stacked(glu_w.shape[1:], l), _stacked((1, D_SSM), l),
           rows_of(0, D_ATT), rows_of(D_ATT, D_SSM), rows_of(D_ATT + D_SSM, D_POOL),
           _stacked((1, d), l, 1), _stacked((1, d), l, 1)],
        out_specs=tok(d),
        scratch_shapes=scratch,
        compiler_params=pltpu.CompilerParams(vmem_limit_bytes=VMEM_LIMIT),
        name="outproj",
    )(x, mod_all, *[o for o, _ in att], *[lse for _, lse in att], y_ssm, y_pool,
      glu_w, glu_b, w_out, w_out, w_out, ln_g, ln_b)


def _block_diag(w):
    g, n, _ = w.shape
    return jnp.einsum('gab,gh->gahb', w, jnp.eye(g, dtype=w.dtype)).reshape(g * n, g * n)


def kernel(x, c, rel_bias, ada_w, ada_b, ln_g, ln_b, ffn_w_gate, ffn_w_up, ffn_w_down, w_in, w_out,
           ssm_a_re, ssm_a_im, ssm_log_dt, ssm_b_re, ssm_b_im, ssm_c_re, ssm_c_im, ssm_d, glu_w, glu_b,
           pool_w, pool_scale):
    bsz = x.shape[0]
    mod_all = _adaln(c, ada_w, ada_b).reshape(DEPTH, bsz, 9, D_MODEL)
    wg, wu, wd = ffn_w_gate, ffn_w_up, ffn_w_down
    w_in_b, w_out_b, glu_w_b = w_in.astype(BF16), w_out.astype(BF16), glu_w.astype(BF16)
    ln_g4, ln_b4 = ln_g.reshape(DEPTH, 3, 1, D_MODEL), ln_b.reshape(DEPTH, 3, 1, D_MODEL)
    glu_b3 = glu_b.reshape(DEPTH, 1, D_SSM)
    pool_scale3 = pool_scale.reshape(DEPTH, 1, D_POOL)
    for l in range(DEPTH):
        x = _ffn(x, mod_all, wg, wu, wd, ln_g4, ln_b4, l, 0)
        qkv, u_ssm, y_pool = _inproj(x, mod_all, w_in_b, _block_diag(pool_w[l]).astype(BF16), pool_scale3, l)
        att = _dilated_attention(qkv, rel_bias)
        tables = _ssm_tables(ssm_a_re[l], ssm_a_im[l], ssm_log_dt[l], ssm_b_re[l], ssm_b_im[l],
                             ssm_c_re[l], ssm_c_im[l], ssm_d[l])
        y_ssm = _ssm(u_ssm, tables)
        x = _outproj(x, mod_all, att, y_ssm, y_pool, glu_w_b, glu_b3, w_out_b, ln_g4, ln_b4, l)
        x = _ffn(x, mod_all, wg, wu, wd, ln_g4, ln_b4, l, 1)
    return x
```

```python
import functools
import math

import jax
import jax.numpy as jnp
import numpy as np
from jax import lax
from jax.experimental import pallas as pl
from jax.experimental.pallas import tpu as pltpu

F32 = jnp.float32
BF16 = jnp.bfloat16

D_MODEL = 1024
DEPTH = 2
HEAD_DIM = 64
N_HEADS = 8
D_ATT = N_HEADS * HEAD_DIM
DILATED_PATTERNS = ((128, 1), (512, 4), (2048, 16))
DILATIONS = tuple(d for _, d in DILATED_PATTERNS)
ATT_BLOCK = 128
SSM_GROUP = 16
D_SSM = 256
N_SSM_GROUPS = D_SSM // SSM_GROUP
SSM_STATE = 64
POOL_WINDOWS = (2, 4, 8, 16)
D_POOL = 256
POOL_GROUP = D_POOL // len(POOL_WINDOWS)
D_IN = 3 * D_ATT + D_SSM + D_POOL
D_FF = 2816
N_BUCKETS = 32
MAX_DISTANCE = 2048
ALPHA = (2 * DEPTH) ** 0.25
FFN_RES = 0.5
LN_EPS = 1e-5
NEG = -1e30

LANES = 128
TOKEN_TILE = 512
FF_CHUNK = 512
FFN_TOKEN_TILE = 512
ADA_COL_TILE = 1152
SSM_CHUNK = 8
POOL_HALO = 2 * max(POOL_WINDOWS)
ATT_LOOKAHEAD = 8
ATT_STEP_BLOCKS = 8
MERGE_ROWS = 128
LOG2E = math.log2(math.e)
STAT_SUM_OFFSET = 16
VMEM_LIMIT = 56 * 1024 * 1024


def _sigmoid(x):
    return 1.0 / (1.0 + jnp.exp(-x))


def _layernorm(x):
    mu = jnp.mean(x, axis=-1, keepdims=True)
    xc = x - mu
    var = jnp.mean(xc * xc, axis=-1, keepdims=True)
    return xc * lax.rsqrt(var + LN_EPS)


def _resident(shape):
    zeros = (0,) * len(shape)
    return pl.BlockSpec(shape, lambda *_: zeros, pipeline_mode=pl.Buffered(1))


def _stacked(tail, *lead):
    idx = tuple(lead) + (0,) * len(tail)
    return pl.BlockSpec((None,) * len(lead) + tuple(tail), lambda *_: idx, pipeline_mode=pl.Buffered(1))


def _adaln_kernel(c_ref, w_ref, b_ref, o_ref):
    c = c_ref[...]
    cond = (c * _sigmoid(c)).astype(BF16)
    o_ref[...] = jnp.dot(cond, w_ref[...].astype(BF16), preferred_element_type=F32) + b_ref[...]


def _adaln(c, ada_w, ada_b):
    nl, d, n = ada_w.shape
    bsz = c.shape[0]
    return pl.pallas_call(
        _adaln_kernel,
        out_shape=jax.ShapeDtypeStruct((nl, bsz, n), F32),
        grid=(nl, n // ADA_COL_TILE),
        in_specs=[
            pl.BlockSpec((bsz, d), lambda l, j: (0, 0)),
            pl.BlockSpec((None, d, ADA_COL_TILE), lambda l, j: (l, 0, j)),
            pl.BlockSpec((None, 1, ADA_COL_TILE), lambda l, j: (l, 0, j)),
        ],
        out_specs=pl.BlockSpec((None, bsz, ADA_COL_TILE), lambda l, j: (l, 0, j)),
        compiler_params=pltpu.CompilerParams(vmem_limit_bytes=VMEM_LIMIT),
        name="adaln",
    )(c, ada_w, ada_b.reshape(nl, 1, n))


def _ff_chunks():
    chunks, c0 = [], 0
    while c0 < D_FF:
        cw = min(FF_CHUNK, D_FF - c0)
        chunks.append((c0, cw))
        c0 += cw
    return tuple(chunks)


def _ffn_kernel(x_ref, mod_ref, wg_ref, wu_ref, wd_ref, lg_ref, lb_ref, o_ref, acc_ref, *, sub):
    x = x_ref[...]
    shift = mod_ref[3 * sub + 0:3 * sub + 1, :]
    scale = mod_ref[3 * sub + 1:3 * sub + 2, :]
    gate = mod_ref[3 * sub + 2:3 * sub + 3, :]
    h = (_layernorm(x) * (1.0 + scale) + shift).astype(BF16)
    for idx, (c0, cw) in enumerate(_ff_chunks()):
        g = jnp.dot(h, wg_ref[:, c0:c0 + cw].astype(BF16), preferred_element_type=F32)
        u = jnp.dot(h, wu_ref[:, c0:c0 + cw].astype(BF16), preferred_element_type=F32)
        a = (g * _sigmoid(g) * u).astype(BF16)
        d = jnp.dot(a, wd_ref[c0:c0 + cw, :].astype(BF16), preferred_element_type=F32)
        if idx == 0:
            acc_ref[...] = d
        else:
            acc_ref[...] += d
    y = ALPHA * x + (FFN_RES * gate) * acc_ref[...]
    o_ref[...] = _layernorm(y) * lg_ref[...] + lb_ref[...]


def _mod_spec(l, d):
    return pl.BlockSpec((None, None, 9, d), lambda b, i: (l, b, 0, 0))


def _ffn(x, mod_all, wg, wu, wd, ln_g, ln_b, l, which):
    bsz, s, d = x.shape
    tm = FFN_TOKEN_TILE
    sub = 2 * which
    return pl.pallas_call(
        functools.partial(_ffn_kernel, sub=sub),
        out_shape=jax.ShapeDtypeStruct(x.shape, F32),
        grid=(bsz, s // tm),
        in_specs=[
            pl.BlockSpec((None, tm, d), lambda b, i: (b, i, 0)),
            _mod_spec(l, d),
            _stacked(wg.shape[2:], l, which),
            _stacked(wu.shape[2:], l, which),
            _stacked(wd.shape[2:], l, which),
            _stacked((1, d), l, sub),
            _stacked((1, d), l, sub),
        ],
        out_specs=pl.BlockSpec((None, tm, d), lambda b, i: (b, i, 0)),
        scratch_shapes=[pltpu.VMEM((tm, d), F32)],
        compiler_params=pltpu.CompilerParams(vmem_limit_bytes=VMEM_LIMIT),
        name=f"ffn{sub}",
    )(x, mod_all, wg, wu, wd, ln_g, ln_b)


def _inproj_kernel(x_ref, mod_ref, w_ref, pw_ref, ps_ref, *rest):
    qkv_refs = rest[:3 * len(DILATIONS)]
    us_ref, yp_ref, zs_ref, z4_ref = rest[3 * len(DILATIONS):3 * len(DILATIONS) + 4]
    e_refs = rest[3 * len(DILATIONS) + 4:]
    i = pl.program_id(1)
    tm = x_ref.shape[0]
    x = x_ref[...]
    shift = mod_ref[3:4, :]
    scale = mod_ref[4:5, :]
    h = (_layernorm(x) * (1.0 + scale) + shift).astype(BF16)
    z = jnp.dot(h, w_ref[...], preferred_element_type=F32)
    up = z[:, 3 * D_ATT + D_SSM:]

    slabs_per = D_ATT // LANES
    n_qkv = 3 * slabs_per
    n_slabs = (3 * D_ATT + D_SSM) // LANES
    for c in range(n_slabs):
        col = z[:, c * LANES:(c + 1) * LANES]
        zs_ref[c] = col * (HEAD_DIM ** -0.5 * LOG2E) if c < slabs_per else col
    assert DILATIONS == (1, 4, 16)
    q4rows, q16rows = tm // 4, tm // 16
    for c in range(n_qkv):
        which, lanes = c // slabs_per, slice((c % slabs_per) * LANES, (c % slabs_per + 1) * LANES)
        qkv_refs[which][0, :, lanes] = zs_ref[c].astype(BF16)
        for r4 in range(4):
            blk = zs_ref[c, pl.ds(r4, q4rows, stride=4), :]
            z4_ref[c, r4 * q4rows:(r4 + 1) * q4rows, :] = blk
            qkv_refs[3 + which][r4, :, lanes] = blk.astype(BF16)
        for r4 in range(4):
            for c4 in range(4):
                blk = z4_ref[c, pl.ds(r4 * q4rows + c4, q16rows, stride=4), :]
                qkv_refs[6 + which][r4 + 4 * c4, :, lanes] = blk.astype(BF16)
    for c in range(n_qkv, n_slabs):
        for j in range(SSM_CHUNK):
            lo = j * D_SSM + (c - n_qkv) * LANES
            us_ref[:, lo:lo + LANES] = zs_ref[c, pl.ds(j, tm // SSM_CHUNK, stride=SSM_CHUNK), :]

    e1, e2, e4, e8 = e_refs
    hl = POOL_HALO
    assert POOL_WINDOWS == (2, 4, 8, 16) and hl == 32

    @pl.when(i == 0)
    def _():
        e1[0:hl, :] = jnp.zeros((hl, D_POOL), F32)

    @pl.when(i > 0)
    def _():
        e1[0:hl, :] = e1[tm:tm + hl, :]

    e1[hl:hl + tm, :] = up
    e2[8:, :] = e1[8:, :] + e1[7:tm + hl - 1, :]
    e4[16:, :] = e2[16:, :] + e2[14:tm + hl - 2, :]
    e8[24:, :] = e4[24:, :] + e4[20:tm + hl - 4, :]
    sums = {2: e2[hl:, :], 4: e4[hl:, :], 8: e8[hl:, :], 16: e8[hl:, :] + e8[hl - 8:tm + hl - 8, :]}
    pos = (i * tm + lax.broadcasted_iota(jnp.int32, (tm, 1), 0) + 1).astype(F32)
    group = lax.broadcasted_iota(jnp.int32, (1, D_POOL), 1) // POOL_GROUP
    mean = sums[POOL_WINDOWS[-1]] / jnp.minimum(pos, float(POOL_WINDOWS[-1]))
    for gi in range(len(POOL_WINDOWS) - 2, -1, -1):
        w = POOL_WINDOWS[gi]
        mean = jnp.where(group == gi, sums[w] / jnp.minimum(pos, float(w)), mean)
    pooled = (mean - up).astype(BF16)
    yp = jnp.dot(pooled, pw_ref[...], preferred_element_type=F32) * ps_ref[...]
    yp_ref[...] = yp.astype(BF16)


def _inproj(x, mod_all, w_in, pool_w_bd, pool_scale, l):
    bsz, s, d = x.shape
    tm = TOKEN_TILE
    tok = lambda width: pl.BlockSpec((None, tm, width), lambda b, i: (b, i, 0))
    qkv_shapes, qkv_specs = [], []
    for dil in DILATIONS:
        for _ in range(3):
            qkv_shapes.append(jax.ShapeDtypeStruct((bsz, dil, s // dil, D_ATT), BF16))
            qkv_specs.append(pl.BlockSpec((None, dil, tm // dil, D_ATT), lambda b, i: (b, 0, i, 0)))
    res = pl.pallas_call(
        _inproj_kernel,
        out_shape=tuple(qkv_shapes) + (
            jax.ShapeDtypeStruct((bsz, s // SSM_CHUNK, SSM_CHUNK * D_SSM), F32),
            jax.ShapeDtypeStruct((bsz, s, D_POOL), BF16),
        ),
        grid=(bsz, s // tm),
        in_specs=[
            tok(d),
            _mod_spec(l, d),
            _stacked(w_in.shape[1:], l),
            _stacked(pool_w_bd.shape[1:], l),
            _stacked((1, D_POOL), l),
        ],
        out_specs=tuple(qkv_specs) + (
            pl.BlockSpec((None, tm // SSM_CHUNK, SSM_CHUNK * D_SSM), lambda b, i: (b, i, 0)), tok(D_POOL)),
        scratch_shapes=[pltpu.VMEM(((3 * D_ATT + D_SSM) // LANES, tm, LANES), F32),
                        pltpu.VMEM((3 * D_ATT // LANES, tm, LANES), F32)]
        + [pltpu.VMEM((tm + POOL_HALO, D_POOL), F32) for _ in range(4)],
        compiler_params=pltpu.CompilerParams(
            dimension_semantics=("arbitrary", "arbitrary"), vmem_limit_bytes=VMEM_LIMIT),
        name="inproj",
    )(x, mod_all, w_in, pool_w_bd, pool_scale)
    qkv = [tuple(res[3 * di:3 * di + 3]) for di in range(len(DILATIONS))]
    return qkv, res[-2], res[-1]


def _t5_bucket(dist):
    max_exact = N_BUCKETS // 2
    dd = np.maximum(dist, 1).astype(np.float32)
    large = max_exact + (np.log(dd / max_exact) / math.log(MAX_DISTANCE / max_exact)
                         * (N_BUCKETS - max_exact)).astype(np.int32)
    large = np.minimum(large, N_BUCKETS - 1)
    return np.where(dist < max_exact, dist, large).astype(np.int32)


def _branch_bias(rel_bias, window, dilation, has_prev):
    qb = ATT_BLOCK
    n_keys = window // dilation
    assert n_keys == qb
    period = 3 * qb
    dist = np.arange(n_keys, -1, -1)
    row = rel_bias[_t5_bucket(dist * dilation)].T.astype(F32) * LOG2E
    row = jnp.concatenate([row, jnp.full((N_HEADS, period - n_keys - 1), NEG, F32)], axis=1)
    flat = jnp.tile(row, (1, qb))[:, :qb * (period - 1)]
    bias = flat.reshape(N_HEADS, qb, period - 1)[:, :, :2 * qb]
    if not has_prev:
        return bias[:, :, qb:]
    first = jnp.concatenate([jnp.full((N_HEADS, qb, qb), NEG, F32), bias[:, :, qb:]], axis=2)
    return jnp.stack([bias, first], 0)


def _stat_lane(h):
    return h if h % 2 == 1 else HEAD_DIM + h


def _attn_kernel(*refs, has_prev):
    qb = ATT_BLOCK
    if has_prev:
        q_ref, kc_ref, kh_ref, vc_ref, vh_ref, bias_ref, o_ref, st_ref, kbuf, vbuf = refs
        nsub = q_ref.shape[0] // qb
        kbuf[0:qb, :] = kh_ref[...]
        kbuf[qb:, :] = kc_ref[...]
        vbuf[0:qb, :] = vh_ref[...]
        vbuf[qb:, :] = vc_ref[...]
    else:
        q_ref, kc_ref, vc_ref, bias_ref, o_ref, st_ref = refs
        nsub = q_ref.shape[0]

    pair = 2 * HEAD_DIM
    assert pair == LANES
    lane = lax.broadcasted_iota(jnp.int32, (1, pair), 1)
    low = lane < HEAD_DIM
    stat_lane = lane

    def block(jj, carry):
        if has_prev:
            row0 = pl.multiple_of(jj * qb, qb)
            first = jnp.logical_and(pl.program_id(2) == 0, jj == 0).astype(jnp.int32)
            q_at = lambda cols: q_ref[pl.ds(row0, qb), cols]
            k_at = lambda cols: kbuf[pl.ds(row0, 2 * qb), cols]
            v_at = lambda cols: vbuf[pl.ds(row0, 2 * qb), cols]
            bias_at = lambda h: bias_ref[first, h]
        else:
            q_at = lambda cols: q_ref[jj, :, cols]
            k_at = lambda cols: kc_ref[jj, :, cols]
            v_at = lambda cols: vc_ref[jj, :, cols]
            bias_at = lambda h: bias_ref[h]

        def pair_cols(h):
            return slice((h // 2) * pair, (h // 2 + 1) * pair)

        def scores(h):
            q2 = q_at(pair_cols(h))
            qh = jnp.where(low, q2, jnp.zeros_like(q2)) if h % 2 == 0 else jnp.where(low, jnp.zeros_like(q2), q2)
            s = lax.dot_general(qh, k_at(pair_cols(h)), (((1,), (1,)), ((), ())), preferred_element_type=F32)
            return s + bias_at(h)

        pending = {h: scores(h) for h in range(min(ATT_LOOKAHEAD, N_HEADS))}
        outs = {}
        stats = jnp.ones((qb, LANES), F32)
        for h in range(N_HEADS):
            s = pending.pop(h)
            m = jnp.max(s, axis=-1, keepdims=True)
            p = jnp.exp2(s - m)
            if h + ATT_LOOKAHEAD < N_HEADS:
                pending[h + ATT_LOOKAHEAD] = scores(h + ATT_LOOKAHEAD)
            cols = pair_cols(h)
            v2 = v_at(cols)
            v2 = jnp.where(low, v2, jnp.ones_like(v2)) if h % 2 == 0 else jnp.where(low, jnp.ones_like(v2), v2)
            pv = jnp.dot(p.astype(BF16), v2, preferred_element_type=F32)
            outs[h] = pv
            stats = jnp.where(stat_lane == _stat_lane(h), m, stats)
            stats = jnp.where(stat_lane == _stat_lane(h) + STAT_SUM_OFFSET, pv, stats)
            if h % 2 == 1:
                o2 = jnp.where(low, outs.pop(h - 1), outs.pop(h)).astype(BF16)
                if has_prev:
                    o_ref[pl.ds(row0, qb), cols] = o2
                else:
                    o_ref[jj, :, cols] = o2
        if has_prev:
            st_ref[pl.ds(row0, qb), :] = stats
        else:
            st_ref[jj] = stats
        return carry

    lax.fori_loop(0, nsub, block, 0, unroll=True)


def _attn_branch(q, k, v, bias):
    bsz, d, ln, _ = q.shape
    qb = ATT_BLOCK
    has_prev = ln > qb
    out_shape = (jax.ShapeDtypeStruct((bsz, d, ln, D_ATT), BF16),
                 jax.ShapeDtypeStruct((bsz, d, ln, LANES), F32))
    if has_prev:
        rows = min(ATT_STEP_BLOCKS * qb, ln)
        per = rows // qb
        cur = lambda width: pl.BlockSpec((None, None, rows, width), lambda b, r, j: (b, r, j, 0))
        halo = pl.BlockSpec((None, None, qb, D_ATT), lambda b, r, j: (b, r, jnp.maximum(j * per - 1, 0), 0))
        grid = (bsz, d, ln // rows)
        args = [q, k, k, v, v, bias]
        specs = [cur(D_ATT), cur(D_ATT), halo, cur(D_ATT), halo, _resident(bias.shape)]
        scratch = [pltpu.VMEM((rows + qb, D_ATT), BF16), pltpu.VMEM((rows + qb, D_ATT), BF16)]
    else:
        per = min(ATT_STEP_BLOCKS, d)
        cur = lambda width: pl.BlockSpec((None, per, qb, width), lambda b, r: (b, r, 0, 0))
        grid = (bsz, d // per)
        args = [q, k, v, bias]
        specs = [cur(D_ATT), cur(D_ATT), cur(D_ATT), _resident(bias.shape)]
        scratch = []
    return pl.pallas_call(
        functools.partial(_attn_kernel, has_prev=has_prev),
        out_shape=out_shape,
        grid=grid,
        in_specs=specs,
        out_specs=(cur(D_ATT), cur(LANES)),
        scratch_shapes=scratch,
        compiler_params=pltpu.CompilerParams(vmem_limit_bytes=VMEM_LIMIT),
        name=f"attn_d{d}",
    )(*args)


def _dilated_attention(qkv, biases):
    return [_attn_branch(q, k, v, bias) for (q, k, v), bias in zip(qkv, biases)]


def _ssm_tables(a_re, a_im, log_dt, b_re, b_im, c_re, c_im, d_skip):
    hi = lax.Precision.HIGHEST
    t, g, p, c = SSM_CHUNK, N_SSM_GROUPS, SSM_STATE, SSM_GROUP
    dt = jnp.exp(log_dt)[:, None]
    mag = jnp.exp(a_re * dt)
    ar, ai = mag * jnp.cos(a_im * dt), mag * jnp.sin(a_im * dt)
    den = a_re * a_re + a_im * a_im
    fr = ((ar - 1.0) * a_re + ai * a_im) / den
    fi = (ai * a_re - (ar - 1.0) * a_im) / den
    bbr = fr[:, :, None] * b_re - fi[:, :, None] * b_im
    bbi = fr[:, :, None] * b_im + fi[:, :, None] * b_re
    pr, pi_ = [jnp.ones_like(ar)], [jnp.zeros_like(ar)]
    for _ in range(t):
        pr.append(pr[-1] * ar - pi_[-1] * ai)
        pi_.append(pr[-2] * ai + pi_[-1] * ar)
    n = t * g * c
    assert n == 2 * g * p and t * c == 2 * p

    def expand(dense, row_div, col_div, spread):
        full = jnp.dot(dense.astype(BF16), spread, preferred_element_type=F32)
        rg = (lax.broadcasted_iota(jnp.int32, (n, n), 0) // row_div) % g
        cg = (lax.broadcasted_iota(jnp.int32, (n, n), 1) // col_div) % g
        return jnp.where(rg == cg, full, 0.0).astype(BF16)

    kk = np.arange(t * c)[:, None]
    cc = np.arange(n)[None, :]
    spread_rp = jnp.asarray((cc // (g * p) == kk // p) & (cc % p == kk % p), BF16)
    spread_jc = jnp.asarray((cc // (g * c) == kk // c) & (cc % c == kk % c), BF16)

    win = []
    for j in range(t):
        qr, qi = pr[t - 1 - j][:, :, None], pi_[t - 1 - j][:, :, None]
        win.append(jnp.stack([qr * bbr - qi * bbi, qr * bbi + qi * bbr], 0))
    win = jnp.stack(win, 0)
    w_in = expand(jnp.transpose(win, (0, 2, 4, 1, 3)).reshape(n, 2 * p), c, p, spread_rp)

    wout, hs = [], []
    for j in range(t + 1):
        wr = c_re * pr[j][:, None, :] - c_im * pi_[j][:, None, :]
        wi = c_re * pi_[j][:, None, :] + c_im * pr[j][:, None, :]
        if j >= 1:
            wout.append(jnp.stack([wr, -wi], 0))
        if j < t:
            hs.append(jnp.einsum('gcp,gpd->gdc', wr, bbr, precision=hi)
                      - jnp.einsum('gcp,gpd->gdc', wi, bbi, precision=hi))
    wout = jnp.stack(wout, 0)
    w_out = expand(jnp.transpose(wout, (1, 2, 4, 0, 3)).reshape(n, t * c), p, c, spread_jc)

    zero_h = jnp.zeros_like(hs[0])
    toep = jnp.stack([jnp.stack([hs[j - jp] if j >= jp else zero_h for j in range(t)], 0)
                      for jp in range(t)], 0)
    w_intra = expand(jnp.transpose(toep, (0, 2, 3, 1, 4)).reshape(n, t * c), c, c, spread_jc)

    abar_t = jnp.concatenate([pr[t].reshape(1, g * p), pi_[t].reshape(1, g * p)], axis=1)
    d_row = jnp.tile(d_skip.reshape(1, D_SSM), (1, t))
    return w_in, w_intra, w_out, abar_t, d_row


def _ssm_kernel(u_ref, win_ref, wintra_ref, wout_ref, abar_ref, d_ref, y_ref, z_ref, xp_ref):
    n = u_ref.shape[0]
    half = N_SSM_GROUPS * SSM_STATE
    u = u_ref[...]
    ub = u.astype(BF16)
    z_ref[...] = jnp.dot(ub, win_ref[...], preferred_element_type=F32)
    ar = abar_ref[:, 0:half]
    ai = abar_ref[:, half:]

    def step(k, carry):
        xr, xi = carry
        xp_ref[pl.ds(k, 1), 0:half] = xr
        xp_ref[pl.ds(k, 1), half:] = xi
        zr = z_ref[pl.ds(k, 1), 0:half]
        zi = z_ref[pl.ds(k, 1), half:]
        return ar * xr - ai * xi + zr, ar * xi + ai * xr + zi

    zero = jnp.zeros((1, half), F32)
    lax.fori_loop(0, n, step, (zero, zero))
    y = jnp.dot(ub, wintra_ref[...], preferred_element_type=F32)
    y = y + jnp.dot(xp_ref[...].astype(BF16), wout_ref[...], preferred_element_type=F32)
    y_ref[...] = y + d_ref[...] * u


def _ssm(u, tables, l):
    w_in, w_intra, w_out, abar_t, d_row = tables
    bsz, n, width = u.shape
    row = pl.BlockSpec((None, n, width), lambda b: (b, 0, 0))
    return pl.pallas_call(
        _ssm_kernel,
        out_shape=jax.ShapeDtypeStruct((bsz, n, width), F32),
        grid=(bsz,),
        in_specs=[row] + [_stacked(t.shape[1:], l) for t in tables],
        out_specs=row,
        scratch_shapes=[pltpu.VMEM((n, w_in.shape[2]), F32), pltpu.VMEM((n, w_in.shape[2]), F32)],
        compiler_params=pltpu.CompilerParams(vmem_limit_bytes=VMEM_LIMIT),
        name="ssm",
    )(u, w_in, w_intra, w_out, abar_t, d_row)


def _outproj_kernel(x_ref, mod_ref, *rest):
    nd = len(DILATIONS)
    o_refs, l_refs = rest[0:nd], rest[nd:2 * nd]
    (ys_ref, yp_ref, gw_ref, gb_ref, wa_ref, ws_ref, wp_ref, lg_ref, lb_ref, o_ref) = rest[2 * nd:2 * nd + 10]
    on4, on16, ot4, ln4, ln16, lt4, ysn_ref, ya_ref = rest[2 * nd + 10:]
    tm = x_ref.shape[0]
    slabs = D_ATT // LANES
    assert DILATIONS == (1, 4, 16)
    r4rows, r16rows = tm // 4, tm // 16

    for j in range(SSM_CHUNK):
        for c in range(D_SSM // LANES):
            lo = j * D_SSM + c * LANES
            ysn_ref[c, pl.ds(j, tm // SSM_CHUNK, stride=SSM_CHUNK), :] = ys_ref[:, lo:lo + LANES]

    for r4 in range(4):
        ln4[pl.ds(r4, r4rows, stride=4), :] = l_refs[1][r4]
        for c4 in range(4):
            lt4[pl.ds(r4 * r4rows + c4, r16rows, stride=4), :] = l_refs[2][r4 + 4 * c4]
        for c in range(slabs):
            lanes = slice(c * LANES, (c + 1) * LANES)
            on4[c, pl.ds(r4, r4rows, stride=4), :] = o_refs[1][r4, :, lanes].astype(F32)
            for c4 in range(4):
                ot4[c, pl.ds(r4 * r4rows + c4, r16rows, stride=4), :] = o_refs[2][r4 + 4 * c4, :, lanes].astype(F32)
    for r4 in range(4):
        ln16[pl.ds(r4, r4rows, stride=4), :] = lt4[r4 * r4rows:(r4 + 1) * r4rows, :]
        for c in range(slabs):
            on16[c, pl.ds(r4, r4rows, stride=4), :] = ot4[c, r4 * r4rows:(r4 + 1) * r4rows, :]

    head = lax.broadcasted_iota(jnp.int32, (LANES, D_ATT), 1) // HEAD_DIM
    stat_of_head = jnp.where(head % 2 == 1, head, HEAD_DIM + head)
    spread = (stat_of_head == lax.broadcasted_iota(jnp.int32, (LANES, D_ATT), 0)).astype(BF16)
    st_lane = lax.broadcasted_iota(jnp.int32, (1, LANES), 1)
    is_max_lane = functools.reduce(jnp.logical_or, [st_lane == _stat_lane(h) for h in range(N_HEADS)])
    for rc in range(tm // MERGE_ROWS):
        rs = slice(rc * MERGE_ROWS, (rc + 1) * MERGE_ROWS)
        st_nat = [l_refs[0][0, rs, :], ln4[rs, :], ln16[rs, :]]
        o_nat = [o_refs[0][0, rs, :].astype(F32),
                 jnp.concatenate([on4[c, rs, :] for c in range(slabs)], axis=1),
                 jnp.concatenate([on16[c, rs, :] for c in range(slabs)], axis=1)]
        mx = functools.reduce(jnp.maximum, st_nat)
        es = [jnp.exp2(st - mx) for st in st_nat]
        sums = [pltpu.roll(st, LANES - STAT_SUM_OFFSET, axis=1) for st in st_nat]
        inv = 1.0 / functools.reduce(lambda a, b: a + b, [e * l for e, l in zip(es, sums)])
        ya = None
        for e, ov in zip(es, o_nat):
            w = jnp.where(is_max_lane, e * inv, 0.0)
            wx = jnp.dot(w.astype(BF16), spread, preferred_element_type=F32)
            ya = wx * ov if ya is None else ya + wx * ov
        ya_ref[rs, :] = ya.astype(BF16)

    x = x_ref[...]
    gate = mod_ref[5:6, :]
    ys = jnp.concatenate([ysn_ref[c] for c in range(D_SSM // LANES)], axis=1)
    cdf = 0.5 * (1.0 + jnp.tanh(math.sqrt(2.0 / math.pi) * (ys + 0.044715 * (ys * ys * ys))))
    t = jnp.dot((ys * cdf).astype(BF16), gw_ref[...], preferred_element_type=F32) + gb_ref[...]
    yg = (ys * _sigmoid(t)).astype(BF16)
    y = jnp.dot(ya_ref[...], wa_ref[...], preferred_element_type=F32)
    y = y + jnp.dot(yg, ws_ref[...], preferred_element_type=F32)
    y = y + jnp.dot(yp_ref[...], wp_ref[...], preferred_element_type=F32)
    r = ALPHA * x + gate * y
    o_ref[...] = _layernorm(r) * lg_ref[...] + lb_ref[...]


def _outproj(x, mod_all, att, y_ssm, y_pool, glu_w, glu_b, w_out, ln_g, ln_b, l):
    bsz, s, d = x.shape
    tm = TOKEN_TILE
    tok = lambda width: pl.BlockSpec((None, tm, width), lambda b, i: (b, i, 0))
    res = lambda dil, width: pl.BlockSpec((None, dil, tm // dil, width), lambda b, i: (b, 0, i, 0))
    rows_of = lambda r0, n: pl.BlockSpec((None, n, d), lambda *_: (l, r0 // n, 0), pipeline_mode=pl.Buffered(1))
    assert D_ATT % D_SSM == 0 and D_SSM == D_POOL
    slab = lambda n: pltpu.VMEM((n, tm, LANES), F32)
    scratch = [slab(D_ATT // LANES)] * 3 + [pltpu.VMEM((tm, LANES), F32)] * 3
    scratch += [slab(D_SSM // LANES), pltpu.VMEM((tm, D_ATT), BF16)]
    return pl.pallas_call(
        _outproj_kernel,
        out_shape=jax.ShapeDtypeStruct(x.shape, F32),
        grid=(bsz, s // tm),
        in_specs=[tok(d), _mod_spec(l, d)]
        + [res(dil, D_ATT) for dil in DILATIONS]
        + [res(dil, LANES) for dil in DILATIONS]
        + [pl.BlockSpec((None, tm // SSM_CHUNK, SSM_CHUNK * D_SSM), lambda b, i: (b, i, 0)), tok(D_POOL),
           _stacked(glu_w.shape[1:], l), _stacked((1, D_SSM), l),
           rows_of(0, D_ATT), rows_of(D_ATT, D_SSM), rows_of(D_ATT + D_SSM, D_POOL),
           _stacked((1, d), l, 1), _stacked((1, d), l, 1)],
        out_specs=tok(d),
        scratch_shapes=scratch,
        compiler_params=pltpu.CompilerParams(vmem_limit_bytes=VMEM_LIMIT),
        name="outproj",
    )(x, mod_all, *[o for o, _ in att], *[st for _, st in att], y_ssm, y_pool,
      glu_w, glu_b, w_out, w_out, w_out, ln_g, ln_b)


def _block_diag(w):
    g, n, _ = w.shape
    return jnp.einsum('gab,gh->gahb', w, jnp.eye(g, dtype=w.dtype)).reshape(g * n, g * n)


def kernel(x, c, rel_bias, ada_w, ada_b, ln_g, ln_b, ffn_w_gate, ffn_w_up, ffn_w_down, w_in, w_out,
           ssm_a_re, ssm_a_im, ssm_log_dt, ssm_b_re, ssm_b_im, ssm_c_re, ssm_c_im, ssm_d, glu_w, glu_b,
           pool_w, pool_scale):
    bsz = x.shape[0]
    mod_all = _adaln(c, ada_w, ada_b).reshape(DEPTH, bsz, 9, D_MODEL)
    wg, wu, wd = ffn_w_gate, ffn_w_up, ffn_w_down
    w_in_b, w_out_b, glu_w_b = w_in.astype(BF16), w_out.astype(BF16), glu_w.astype(BF16)
    ln_g4, ln_b4 = ln_g.reshape(DEPTH, 3, 1, D_MODEL), ln_b.reshape(DEPTH, 3, 1, D_MODEL)
    glu_b3 = glu_b.reshape(DEPTH, 1, D_SSM)
    pool_scale3 = pool_scale.reshape(DEPTH, 1, D_POOL)
    pool_w_bd = jax.vmap(_block_diag)(pool_w).astype(BF16)
    biases = [_branch_bias(rel_bias, window, dilation, x.shape[1] // dilation > ATT_BLOCK)
              for window, dilation in DILATED_PATTERNS]
    tables = jax.vmap(_ssm_tables)(ssm_a_re, ssm_a_im, ssm_log_dt, ssm_b_re, ssm_b_im,
                                   ssm_c_re, ssm_c_im, ssm_d)
    for l in range(DEPTH):
        x = _ffn(x, mod_all, wg, wu, wd, ln_g4, ln_b4, l, 0)
        qkv, u_ssm, y_pool = _inproj(x, mod_all, w_in_b, pool_w_bd, pool_scale3, l)
        att = _dilated_attention(qkv, biases)
        y_ssm = _ssm(u_ssm, tables, l)
        x = _outproj(x, mod_all, att, y_ssm, y_pool, glu_w_b, glu_b3, w_out_b, ln_g4, ln_b4, l)
        x = _ffn(x, mod_all, wg, wu, wd, ln_g4, ln_b4, l, 1)
    return x
```

```python
import functools
import math

import jax
import jax.numpy as jnp
import numpy as np
from jax import lax
from jax.experimental import pallas as pl
from jax.experimental.pallas import tpu as pltpu

F32 = jnp.float32
BF16 = jnp.bfloat16

D_MODEL = 1024
DEPTH = 2
HEAD_DIM = 64
N_HEADS = 8
D_ATT = N_HEADS * HEAD_DIM
DILATED_PATTERNS = ((128, 1), (512, 4), (2048, 16))
DILATIONS = tuple(d for _, d in DILATED_PATTERNS)
ATT_BLOCK = 128
SSM_GROUP = 16
D_SSM = 256
N_SSM_GROUPS = D_SSM // SSM_GROUP
SSM_STATE = 64
POOL_WINDOWS = (2, 4, 8, 16)
D_POOL = 256
POOL_GROUP = D_POOL // len(POOL_WINDOWS)
D_IN = 3 * D_ATT + D_SSM + D_POOL
D_FF = 2816
N_BUCKETS = 32
MAX_DISTANCE = 2048
ALPHA = (2 * DEPTH) ** 0.25
FFN_RES = 0.5
LN_EPS = 1e-5
NEG = -1e30

LANES = 128
TOKEN_TILE = 512
FF_CHUNK = 512
FFN_TOKEN_TILE = 512
ADA_COL_TILE = 1152
SSM_CHUNK = 8
POOL_HALO = 2 * max(POOL_WINDOWS)
ATT_LOOKAHEAD = 8
ATT_STEP_BLOCKS = 8
MERGE_ROWS = 128
LOG2E = math.log2(math.e)
STAT_SUM_OFFSET = 16
VMEM_LIMIT = 56 * 1024 * 1024


def _sigmoid(x):
    return 1.0 / (1.0 + jnp.exp(-x))


def _layernorm(x):
    mu = jnp.mean(x, axis=-1, keepdims=True)
    xc = x - mu
    var = jnp.mean(xc * xc, axis=-1, keepdims=True)
    return xc * lax.rsqrt(var + LN_EPS)


def _resident(shape):
    zeros = (0,) * len(shape)
    return pl.BlockSpec(shape, lambda *_: zeros, pipeline_mode=pl.Buffered(1))


def _stacked(tail, *lead):
    idx = tuple(lead) + (0,) * len(tail)
    return pl.BlockSpec((None,) * len(lead) + tuple(tail), lambda *_: idx, pipeline_mode=pl.Buffered(1))


def _adaln_kernel(c_ref, w_ref, b_ref, o_ref):
    c = c_ref[...]
    cond = (c * _sigmoid(c)).astype(BF16)
    o_ref[...] = jnp.dot(cond, w_ref[...].astype(BF16), preferred_element_type=F32) + b_ref[...]


def _adaln(c, ada_w, ada_b):
    nl, d, n = ada_w.shape
    bsz = c.shape[0]
    return pl.pallas_call(
        _adaln_kernel,
        out_shape=jax.ShapeDtypeStruct((nl, bsz, n), F32),
        grid=(nl, n // ADA_COL_TILE),
        in_specs=[
            pl.BlockSpec((bsz, d), lambda l, j: (0, 0)),
            pl.BlockSpec((None, d, ADA_COL_TILE), lambda l, j: (l, 0, j)),
            pl.BlockSpec((None, 1, ADA_COL_TILE), lambda l, j: (l, 0, j)),
        ],
        out_specs=pl.BlockSpec((None, bsz, ADA_COL_TILE), lambda l, j: (l, 0, j)),
        compiler_params=pltpu.CompilerParams(vmem_limit_bytes=VMEM_LIMIT),
        name="adaln",
    )(c, ada_w, ada_b.reshape(nl, 1, n))


def _ff_chunks():
    chunks, c0 = [], 0
    while c0 < D_FF:
        cw = min(FF_CHUNK, D_FF - c0)
        chunks.append((c0, cw))
        c0 += cw
    return tuple(chunks)


def _ffn_kernel(x_ref, mod_ref, wg_ref, wu_ref, wd_ref, lg_ref, lb_ref, o_ref, acc_ref, *, sub):
    x = x_ref[...]
    shift = mod_ref[3 * sub + 0:3 * sub + 1, :]
    scale = mod_ref[3 * sub + 1:3 * sub + 2, :]
    gate = mod_ref[3 * sub + 2:3 * sub + 3, :]
    h = (_layernorm(x) * (1.0 + scale) + shift).astype(BF16)
    for idx, (c0, cw) in enumerate(_ff_chunks()):
        g = jnp.dot(h, wg_ref[:, c0:c0 + cw].astype(BF16), preferred_element_type=F32)
        u = jnp.dot(h, wu_ref[:, c0:c0 + cw].astype(BF16), preferred_element_type=F32)
        a = (g * _sigmoid(g) * u).astype(BF16)
        d = jnp.dot(a, wd_ref[c0:c0 + cw, :].astype(BF16), preferred_element_type=F32)
        if idx == 0:
            acc_ref[...] = d
        else:
            acc_ref[...] += d
    y = ALPHA * x + (FFN_RES * gate) * acc_ref[...]
    o_ref[...] = _layernorm(y) * lg_ref[...] + lb_ref[...]


def _mod_spec(l, d):
    return pl.BlockSpec((None, None, 9, d), lambda b, i: (l, b, 0, 0))


def _ffn(x, mod_all, wg, wu, wd, ln_g, ln_b, l, which):
    bsz, s, d = x.shape
    tm = FFN_TOKEN_TILE
    sub = 2 * which
    return pl.pallas_call(
        functools.partial(_ffn_kernel, sub=sub),
        out_shape=jax.ShapeDtypeStruct(x.shape, F32),
        grid=(bsz, s // tm),
        in_specs=[
            pl.BlockSpec((None, tm, d), lambda b, i: (b, i, 0)),
            _mod_spec(l, d),
            _stacked(wg.shape[2:], l, which),
            _stacked(wu.shape[2:], l, which),
            _stacked(wd.shape[2:], l, which),
            _stacked((1, d), l, sub),
            _stacked((1, d), l, sub),
        ],
        out_specs=pl.BlockSpec((None, tm, d), lambda b, i: (b, i, 0)),
        scratch_shapes=[pltpu.VMEM((tm, d), F32)],
        compiler_params=pltpu.CompilerParams(vmem_limit_bytes=VMEM_LIMIT),
        name=f"ffn{sub}",
    )(x, mod_all, wg, wu, wd, ln_g, ln_b)


def _inproj_kernel(x_ref, mod_ref, w_ref, pw_ref, ps_ref, *rest):
    qkv_refs = rest[:3 * len(DILATIONS)]
    us_ref, yp_ref, zs_ref, z4_ref = rest[3 * len(DILATIONS):3 * len(DILATIONS) + 4]
    e_refs = rest[3 * len(DILATIONS) + 4:]
    i = pl.program_id(1)
    tm = x_ref.shape[0]
    x = x_ref[...]
    shift = mod_ref[3:4, :]
    scale = mod_ref[4:5, :]
    h = (_layernorm(x) * (1.0 + scale) + shift).astype(BF16)
    z = jnp.dot(h, w_ref[...], preferred_element_type=F32)
    up = z[:, 3 * D_ATT + D_SSM:]

    slabs_per = D_ATT // LANES
    n_qkv = 3 * slabs_per
    n_slabs = (3 * D_ATT + D_SSM) // LANES
    for c in range(n_slabs):
        col = z[:, c * LANES:(c + 1) * LANES]
        zs_ref[c] = col * (HEAD_DIM ** -0.5 * LOG2E) if c < slabs_per else col
    assert DILATIONS == (1, 4, 16)
    q4rows, q16rows = tm // 4, tm // 16
    for c in range(n_qkv):
        which, lanes = c // slabs_per, slice((c % slabs_per) * LANES, (c % slabs_per + 1) * LANES)
        qkv_refs[which][0, :, lanes] = zs_ref[c].astype(BF16)
        for r4 in range(4):
            blk = zs_ref[c, pl.ds(r4, q4rows, stride=4), :]
            z4_ref[c, r4 * q4rows:(r4 + 1) * q4rows, :] = blk
            qkv_refs[3 + which][r4, :, lanes] = blk.astype(BF16)
        for r4 in range(4):
            for c4 in range(4):
                blk = z4_ref[c, pl.ds(r4 * q4rows + c4, q16rows, stride=4), :]
                qkv_refs[6 + which][r4 + 4 * c4, :, lanes] = blk.astype(BF16)
    for c in range(n_qkv, n_slabs):
        for j in range(SSM_CHUNK):
            lo = j * D_SSM + (c - n_qkv) * LANES
            us_ref[:, lo:lo + LANES] = zs_ref[c, pl.ds(j, tm // SSM_CHUNK, stride=SSM_CHUNK), :]

    e1, e2, e4, e8 = e_refs
    hl = POOL_HALO
    assert POOL_WINDOWS == (2, 4, 8, 16) and hl == 32

    @pl.when(i == 0)
    def _():
        e1[0:hl, :] = jnp.zeros((hl, D_POOL), F32)

    @pl.when(i > 0)
    def _():
        e1[0:hl, :] = e1[tm:tm + hl, :]

    e1[hl:hl + tm, :] = up
    e2[8:, :] = e1[8:, :] + e1[7:tm + hl - 1, :]
    e4[16:, :] = e2[16:, :] + e2[14:tm + hl - 2, :]
    e8[24:, :] = e4[24:, :] + e4[20:tm + hl - 4, :]
    sums = {2: e2[hl:, :], 4: e4[hl:, :], 8: e8[hl:, :], 16: e8[hl:, :] + e8[hl - 8:tm + hl - 8, :]}
    pos = (i * tm + lax.broadcasted_iota(jnp.int32, (tm, 1), 0) + 1).astype(F32)
    group = lax.broadcasted_iota(jnp.int32, (1, D_POOL), 1) // POOL_GROUP
    mean = sums[POOL_WINDOWS[-1]] / jnp.minimum(pos, float(POOL_WINDOWS[-1]))
    for gi in range(len(POOL_WINDOWS) - 2, -1, -1):
        w = POOL_WINDOWS[gi]
        mean = jnp.where(group == gi, sums[w] / jnp.minimum(pos, float(w)), mean)
    pooled = (mean - up).astype(BF16)
    yp = jnp.dot(pooled, pw_ref[...], preferred_element_type=F32) * ps_ref[...]
    yp_ref[...] = yp.astype(BF16)


def _inproj(x, mod_all, w_in, pool_w_bd, pool_scale, l):
    bsz, s, d = x.shape
    tm = TOKEN_TILE
    tok = lambda width: pl.BlockSpec((None, tm, width), lambda b, i: (b, i, 0))
    qkv_shapes, qkv_specs = [], []
    for dil in DILATIONS:
        for _ in range(3):
            qkv_shapes.append(jax.ShapeDtypeStruct((bsz, dil, s // dil, D_ATT), BF16))
            qkv_specs.append(pl.BlockSpec((None, dil, tm // dil, D_ATT), lambda b, i: (b, 0, i, 0)))
    res = pl.pallas_call(
        _inproj_kernel,
        out_shape=tuple(qkv_shapes) + (
            jax.ShapeDtypeStruct((bsz, s // SSM_CHUNK, SSM_CHUNK * D_SSM), F32),
            jax.ShapeDtypeStruct((bsz, s, D_POOL), BF16),
        ),
        grid=(bsz, s // tm),
        in_specs=[
            tok(d),
            _mod_spec(l, d),
            _stacked(w_in.shape[1:], l),
            _stacked(pool_w_bd.shape[1:], l),
            _stacked((1, D_POOL), l),
        ],
        out_specs=tuple(qkv_specs) + (
            pl.BlockSpec((None, tm // SSM_CHUNK, SSM_CHUNK * D_SSM), lambda b, i: (b, i, 0)), tok(D_POOL)),
        scratch_shapes=[pltpu.VMEM(((3 * D_ATT + D_SSM) // LANES, tm, LANES), F32),
                        pltpu.VMEM((3 * D_ATT // LANES, tm, LANES), F32)]
        + [pltpu.VMEM((tm + POOL_HALO, D_POOL), F32) for _ in range(4)],
        compiler_params=pltpu.CompilerParams(
            dimension_semantics=("arbitrary", "arbitrary"), vmem_limit_bytes=VMEM_LIMIT),
        name="inproj",
    )(x, mod_all, w_in, pool_w_bd, pool_scale)
    qkv = [tuple(res[3 * di:3 * di + 3]) for di in range(len(DILATIONS))]
    return qkv, res[-2], res[-1]


def _t5_bucket(dist):
    max_exact = N_BUCKETS // 2
    dd = np.maximum(dist, 1).astype(np.float32)
    large = max_exact + (np.log(dd / max_exact) / math.log(MAX_DISTANCE / max_exact)
                         * (N_BUCKETS - max_exact)).astype(np.int32)
    large = np.minimum(large, N_BUCKETS - 1)
    return np.where(dist < max_exact, dist, large).astype(np.int32)


def _branch_bias(rel_bias, window, dilation, has_prev):
    qb = ATT_BLOCK
    n_keys = window // dilation
    assert n_keys == qb
    period = 3 * qb
    dist = np.arange(n_keys, -1, -1)
    row = rel_bias[_t5_bucket(dist * dilation)].T.astype(F32) * LOG2E
    row = jnp.concatenate([row, jnp.full((N_HEADS, period - n_keys - 1), NEG, F32)], axis=1)
    flat = jnp.tile(row, (1, qb))[:, :qb * (period - 1)]
    bias = flat.reshape(N_HEADS, qb, period - 1)[:, :, :2 * qb]
    if not has_prev:
        return bias[:, :, qb:]
    first = jnp.concatenate([jnp.full((N_HEADS, qb, qb), NEG, F32), bias[:, :, qb:]], axis=2)
    return jnp.stack([bias, first], 0)


def _stat_lane(h):
    return h if h % 2 == 1 else HEAD_DIM + h


def _attn_kernel(*refs, has_prev):
    qb = ATT_BLOCK
    if has_prev:
        q_ref, kc_ref, kh_ref, vc_ref, vh_ref, bias_ref, o_ref, st_ref, kbuf, vbuf = refs
        nsub = q_ref.shape[0] // qb
        kbuf[0:qb, :] = kh_ref[...]
        kbuf[qb:, :] = kc_ref[...]
        vbuf[0:qb, :] = vh_ref[...]
        vbuf[qb:, :] = vc_ref[...]
    else:
        q_ref, kc_ref, vc_ref, bias_ref, o_ref, st_ref = refs
        nsub = q_ref.shape[0]

    pair = 2 * HEAD_DIM
    assert pair == LANES
    lane = lax.broadcasted_iota(jnp.int32, (1, pair), 1)
    low = lane < HEAD_DIM
    stat_lane = lane

    def block(jj, carry):
        if has_prev:
            row0 = pl.multiple_of(jj * qb, qb)
            first = jnp.logical_and(pl.program_id(2) == 0, jj == 0).astype(jnp.int32)
            q_at = lambda cols: q_ref[pl.ds(row0, qb), cols]
            k_at = lambda cols: kbuf[pl.ds(row0, 2 * qb), cols]
            v_at = lambda cols: vbuf[pl.ds(row0, 2 * qb), cols]
            bias_at = lambda h: bias_ref[first, h]
        else:
            q_at = lambda cols: q_ref[jj, :, cols]
            k_at = lambda cols: kc_ref[jj, :, cols]
            v_at = lambda cols: vc_ref[jj, :, cols]
            bias_at = lambda h: bias_ref[h]

        def pair_cols(h):
            return slice((h // 2) * pair, (h // 2 + 1) * pair)

        def scores(h):
            q2 = q_at(pair_cols(h))
            qh = jnp.where(low, q2, jnp.zeros_like(q2)) if h % 2 == 0 else jnp.where(low, jnp.zeros_like(q2), q2)
            s = lax.dot_general(qh, k_at(pair_cols(h)), (((1,), (1,)), ((), ())), preferred_element_type=F32)
            return s + bias_at(h)

        pending = {h: scores(h) for h in range(min(ATT_LOOKAHEAD, N_HEADS))}
        outs = {}
        stats = jnp.ones((qb, LANES), F32)
        for h in range(N_HEADS):
            s = pending.pop(h)
            m = jnp.max(s, axis=-1, keepdims=True)
            p = jnp.exp2(s - m)
            if h + ATT_LOOKAHEAD < N_HEADS:
                pending[h + ATT_LOOKAHEAD] = scores(h + ATT_LOOKAHEAD)
            cols = pair_cols(h)
            v2 = v_at(cols)
            v2 = jnp.where(low, v2, jnp.ones_like(v2)) if h % 2 == 0 else jnp.where(low, jnp.ones_like(v2), v2)
            pv = jnp.dot(p.astype(BF16), v2, preferred_element_type=F32)
            outs[h] = pv
            stats = jnp.where(stat_lane == _stat_lane(h), m, stats)
            stats = jnp.where(stat_lane == _stat_lane(h) + STAT_SUM_OFFSET, pv, stats)
            if h % 2 == 1:
                o2 = jnp.where(low, outs.pop(h - 1), outs.pop(h)).astype(BF16)
                if has_prev:
                    o_ref[pl.ds(row0, qb), cols] = o2
                else:
                    o_ref[jj, :, cols] = o2
        if has_prev:
            st_ref[pl.ds(row0, qb), :] = stats
        else:
            st_ref[jj] = stats
        return carry

    lax.fori_loop(0, nsub, block, 0, unroll=True)


def _attn_branch(q, k, v, bias):
    bsz, d, ln, _ = q.shape
    qb = ATT_BLOCK
    has_prev = ln > qb
    out_shape = (jax.ShapeDtypeStruct((bsz, d, ln, D_ATT), BF16),
                 jax.ShapeDtypeStruct((bsz, d, ln, LANES), F32))
    if has_prev:
        rows = min(ATT_STEP_BLOCKS * qb, ln)
        per = rows // qb
        cur = lambda width: pl.BlockSpec((None, None, rows, width), lambda b, r, j: (b, r, j, 0))
        halo = pl.BlockSpec((None, None, qb, D_ATT), lambda b, r, j: (b, r, jnp.maximum(j * per - 1, 0), 0))
        grid = (bsz, d, ln // rows)
        args = [q, k, k, v, v, bias]
        specs = [cur(D_ATT), cur(D_ATT), halo, cur(D_ATT), halo, _resident(bias.shape)]
        scratch = [pltpu.VMEM((rows + qb, D_ATT), BF16), pltpu.VMEM((rows + qb, D_ATT), BF16)]
    else:
        per = min(ATT_STEP_BLOCKS, d)
        cur = lambda width: pl.BlockSpec((None, per, qb, width), lambda b, r: (b, r, 0, 0))
        grid = (bsz, d // per)
        args = [q, k, v, bias]
        specs = [cur(D_ATT), cur(D_ATT), cur(D_ATT), _resident(bias.shape)]
        scratch = []
    return pl.pallas_call(
        functools.partial(_attn_kernel, has_prev=has_prev),
        out_shape=out_shape,
        grid=grid,
        in_specs=specs,
        out_specs=(cur(D_ATT), cur(LANES)),
        scratch_shapes=scratch,
        compiler_params=pltpu.CompilerParams(vmem_limit_bytes=VMEM_LIMIT),
        name=f"attn_d{d}",
    )(*args)


def _dilated_attention(qkv, biases):
    return [_attn_branch(q, k, v, bias) for (q, k, v), bias in zip(qkv, biases)]


def _ssm_tables(*params):
    t, g, p, c = SSM_CHUNK, N_SSM_GROUPS, SSM_STATE, SSM_GROUP
    n = t * g * c
    w_in, w_intra, w_out, abar_t, d_row = jax.vmap(_ssm_group_blocks)(*params)
    nl = w_in.shape[0]

    def expand(dense, row_div, col_div, spread):
        full = jnp.dot(dense.reshape(nl * n, t * c).astype(BF16), spread, preferred_element_type=F32)
        rg = ((lax.broadcasted_iota(jnp.int32, (nl * n, n), 0) % n) // row_div) % g
        cg = (lax.broadcasted_iota(jnp.int32, (nl * n, n), 1) // col_div) % g
        return jnp.where(rg == cg, full, 0.0).astype(BF16).reshape(nl, n, n)

    kk = np.arange(t * c)[:, None]
    cc = np.arange(n)[None, :]
    spread_rp = jnp.asarray((cc // (g * p) == kk // p) & (cc % p == kk % p), BF16)
    spread_jc = jnp.asarray((cc // (g * c) == kk // c) & (cc % c == kk % c), BF16)
    return (expand(w_in, c, p, spread_rp), expand(w_intra, c, c, spread_jc),
            expand(w_out, p, c, spread_jc), abar_t, d_row)


def _ssm_group_blocks(a_re, a_im, log_dt, b_re, b_im, c_re, c_im, d_skip):
    hi = lax.Precision.HIGHEST
    t, g, p, c = SSM_CHUNK, N_SSM_GROUPS, SSM_STATE, SSM_GROUP
    dt = jnp.exp(log_dt)[:, None]
    mag = jnp.exp(a_re * dt)
    ar, ai = mag * jnp.cos(a_im * dt), mag * jnp.sin(a_im * dt)
    den = a_re * a_re + a_im * a_im
    fr = ((ar - 1.0) * a_re + ai * a_im) / den
    fi = (ai * a_re - (ar - 1.0) * a_im) / den
    bbr = fr[:, :, None] * b_re - fi[:, :, None] * b_im
    bbi = fr[:, :, None] * b_im + fi[:, :, None] * b_re
    pr, pi_ = [jnp.ones_like(ar)], [jnp.zeros_like(ar)]
    for _ in range(t):
        pr.append(pr[-1] * ar - pi_[-1] * ai)
        pi_.append(pr[-2] * ai + pi_[-1] * ar)
    n = t * g * c
    assert n == 2 * g * p and t * c == 2 * p

    win = []
    for j in range(t):
        qr, qi = pr[t - 1 - j][:, :, None], pi_[t - 1 - j][:, :, None]
        win.append(jnp.stack([qr * bbr - qi * bbi, qr * bbi + qi * bbr], 0))
    win = jnp.stack(win, 0)
    w_in = jnp.transpose(win, (0, 2, 4, 1, 3)).reshape(n, 2 * p)

    wout, hs = [], []
    for j in range(t + 1):
        wr = c_re * pr[j][:, None, :] - c_im * pi_[j][:, None, :]
        wi = c_re * pi_[j][:, None, :] + c_im * pr[j][:, None, :]
        if j >= 1:
            wout.append(jnp.stack([wr, -wi], 0))
        if j < t:
            hs.append(jnp.einsum('gcp,gpd->gdc', wr, bbr, precision=hi)
                      - jnp.einsum('gcp,gpd->gdc', wi, bbi, precision=hi))
    wout = jnp.stack(wout, 0)
    w_out = jnp.transpose(wout, (1, 2, 4, 0, 3)).reshape(n, t * c)

    zero_h = jnp.zeros_like(hs[0])
    toep = jnp.stack([jnp.stack([hs[j - jp] if j >= jp else zero_h for j in range(t)], 0)
                      for jp in range(t)], 0)
    w_intra = jnp.transpose(toep, (0, 2, 3, 1, 4)).reshape(n, t * c)

    abar_t = jnp.concatenate([pr[t].reshape(1, g * p), pi_[t].reshape(1, g * p)], axis=1)
    d_row = jnp.tile(d_skip.reshape(1, D_SSM), (1, t))
    return w_in, w_intra, w_out, abar_t, d_row


def _ssm_kernel(u_ref, win_ref, wintra_ref, wout_ref, abar_ref, d_ref, y_ref, z_ref, xp_ref):
    n = u_ref.shape[0]
    half = N_SSM_GROUPS * SSM_STATE
    u = u_ref[...]
    ub = u.astype(BF16)
    z_ref[...] = jnp.dot(ub, win_ref[...], preferred_element_type=F32)
    ar = abar_ref[:, 0:half]
    ai = abar_ref[:, half:]

    def step(k, carry):
        xr, xi = carry
        xp_ref[pl.ds(k, 1), 0:half] = xr
        xp_ref[pl.ds(k, 1), half:] = xi
        zr = z_ref[pl.ds(k, 1), 0:half]
        zi = z_ref[pl.ds(k, 1), half:]
        return ar * xr - ai * xi + zr, ar * xi + ai * xr + zi

    zero = jnp.zeros((1, half), F32)
    lax.fori_loop(0, n, step, (zero, zero))
    y = jnp.dot(ub, wintra_ref[...], preferred_element_type=F32)
    y = y + jnp.dot(xp_ref[...].astype(BF16), wout_ref[...], preferred_element_type=F32)
    y_ref[...] = y + d_ref[...] * u


def _ssm(u, tables, l):
    w_in, w_intra, w_out, abar_t, d_row = tables
    bsz, n, width = u.shape
    row = pl.BlockSpec((None, n, width), lambda b: (b, 0, 0))
    return pl.pallas_call(
        _ssm_kernel,
        out_shape=jax.ShapeDtypeStruct((bsz, n, width), F32),
        grid=(bsz,),
        in_specs=[row] + [_stacked(t.shape[1:], l) for t in tables],
        out_specs=row,
        scratch_shapes=[pltpu.VMEM((n, w_in.shape[2]), F32), pltpu.VMEM((n, w_in.shape[2]), F32)],
        compiler_params=pltpu.CompilerParams(vmem_limit_bytes=VMEM_LIMIT),
        name="ssm",
    )(u, w_in, w_intra, w_out, abar_t, d_row)


def _outproj_kernel(x_ref, mod_ref, *rest):
    nd = len(DILATIONS)
    o_refs, l_refs = rest[0:nd], rest[nd:2 * nd]
    (ys_ref, yp_ref, gw_ref, gb_ref, wa_ref, ws_ref, wp_ref, lg_ref, lb_ref, o_ref) = rest[2 * nd:2 * nd + 10]
    on4, on16, ot4, ln4, ln16, lt4, ysn_ref, ya_ref = rest[2 * nd + 10:]
    tm = x_ref.shape[0]
    slabs = D_ATT // LANES
    assert DILATIONS == (1, 4, 16)
    r4rows, r16rows = tm // 4, tm // 16

    for j in range(SSM_CHUNK):
        for c in range(D_SSM // LANES):
            lo = j * D_SSM + c * LANES
            ysn_ref[c, pl.ds(j, tm // SSM_CHUNK, stride=SSM_CHUNK), :] = ys_ref[:, lo:lo + LANES]

    ys = jnp.concatenate([ysn_ref[c] for c in range(D_SSM // LANES)], axis=1)
    cdf = 0.5 * (1.0 + jnp.tanh(math.sqrt(2.0 / math.pi) * (ys + 0.044715 * (ys * ys * ys))))
    t = jnp.dot((ys * cdf).astype(BF16), gw_ref[...], preferred_element_type=F32) + gb_ref[...]
    yg = (ys * _sigmoid(t)).astype(BF16)
    y_rest = (jnp.dot(yg, ws_ref[...], preferred_element_type=F32)
              + jnp.dot(yp_ref[...], wp_ref[...], preferred_element_type=F32))

    for r4 in range(4):
        ln4[pl.ds(r4, r4rows, stride=4), :] = l_refs[1][r4]
        for c4 in range(4):
            lt4[pl.ds(r4 * r4rows + c4, r16rows, stride=4), :] = l_refs[2][r4 + 4 * c4]
        for c in range(slabs):
            lanes = slice(c * LANES, (c + 1) * LANES)
            on4[c, pl.ds(r4, r4rows, stride=4), :] = o_refs[1][r4, :, lanes].astype(F32)
            for c4 in range(4):
                ot4[c, pl.ds(r4 * r4rows + c4, r16rows, stride=4), :] = o_refs[2][r4 + 4 * c4, :, lanes].astype(F32)
    for r4 in range(4):
        ln16[pl.ds(r4, r4rows, stride=4), :] = lt4[r4 * r4rows:(r4 + 1) * r4rows, :]
        for c in range(slabs):
            on16[c, pl.ds(r4, r4rows, stride=4), :] = ot4[c, r4 * r4rows:(r4 + 1) * r4rows, :]

    head = lax.broadcasted_iota(jnp.int32, (LANES, D_ATT), 1) // HEAD_DIM
    stat_of_head = jnp.where(head % 2 == 1, head, HEAD_DIM + head)
    spread = (stat_of_head == lax.broadcasted_iota(jnp.int32, (LANES, D_ATT), 0)).astype(BF16)
    st_lane = lax.broadcasted_iota(jnp.int32, (1, LANES), 1)
    is_max_lane = functools.reduce(jnp.logical_or, [st_lane == _stat_lane(h) for h in range(N_HEADS)])
    for rc in range(tm // MERGE_ROWS):
        rs = slice(rc * MERGE_ROWS, (rc + 1) * MERGE_ROWS)
        st_nat = [l_refs[0][0, rs, :], ln4[rs, :], ln16[rs, :]]
        o_nat = [o_refs[0][0, rs, :].astype(F32),
                 jnp.concatenate([on4[c, rs, :] for c in range(slabs)], axis=1),
                 jnp.concatenate([on16[c, rs, :] for c in range(slabs)], axis=1)]
        mx = functools.reduce(jnp.maximum, st_nat)
        es = [jnp.exp2(st - mx) for st in st_nat]
        sums = [pltpu.roll(st, LANES - STAT_SUM_OFFSET, axis=1) for st in st_nat]
        inv = 1.0 / functools.reduce(lambda a, b: a + b, [e * l for e, l in zip(es, sums)])
        ya = None
        for e, ov in zip(es, o_nat):
            w = jnp.where(is_max_lane, e * inv, 0.0)
            wx = jnp.dot(w.astype(BF16), spread, preferred_element_type=F32)
            ya = wx * ov if ya is None else ya + wx * ov
        ya_ref[rs, :] = ya.astype(BF16)

    x = x_ref[...]
    gate = mod_ref[5:6, :]
    y = jnp.dot(ya_ref[...], wa_ref[...], preferred_element_type=F32) + y_rest
    r = ALPHA * x + gate * y
    o_ref[...] = _layernorm(r) * lg_ref[...] + lb_ref[...]


def _outproj(x, mod_all, att, y_ssm, y_pool, glu_w, glu_b, w_out, ln_g, ln_b, l):
    bsz, s, d = x.shape
    tm = TOKEN_TILE
    tok = lambda width: pl.BlockSpec((None, tm, width), lambda b, i: (b, i, 0))
    res = lambda dil, width: pl.BlockSpec((None, dil, tm // dil, width), lambda b, i: (b, 0, i, 0))
    rows_of = lambda r0, n: pl.BlockSpec((None, n, d), lambda *_: (l, r0 // n, 0), pipeline_mode=pl.Buffered(1))
    assert D_ATT % D_SSM == 0 and D_SSM == D_POOL
    slab = lambda n: pltpu.VMEM((n, tm, LANES), F32)
    scratch = [slab(D_ATT // LANES)] * 3 + [pltpu.VMEM((tm, LANES), F32)] * 3
    scratch += [slab(D_SSM // LANES), pltpu.VMEM((tm, D_ATT), BF16)]
    return pl.pallas_call(
        _outproj_kernel,
        out_shape=jax.ShapeDtypeStruct(x.shape, F32),
        grid=(bsz, s // tm),
        in_specs=[tok(d), _mod_spec(l, d)]
        + [res(dil, D_ATT) for dil in DILATIONS]
        + [res(dil, LANES) for dil in DILATIONS]
        + [pl.BlockSpec((None, tm // SSM_CHUNK, SSM_CHUNK * D_SSM), lambda b, i: (b, i, 0)), tok(D_POOL),
           _stacked(glu_w.shape[1:], l), _stacked((1, D_SSM), l),
           rows_of(0, D_ATT), rows_of(D_ATT, D_SSM), rows_of(D_ATT + D_SSM, D_POOL),
           _stacked((1, d), l, 1), _stacked((1, d), l, 1)],
        out_specs=tok(d),
        scratch_shapes=scratch,
        compiler_params=pltpu.CompilerParams(vmem_limit_bytes=VMEM_LIMIT),
        name="outproj",
    )(x, mod_all, *[o for o, _ in att], *[st for _, st in att], y_ssm, y_pool,
      glu_w, glu_b, w_out, w_out, w_out, ln_g, ln_b)


def _block_diag(w):
    g, n, _ = w.shape
    return jnp.einsum('gab,gh->gahb', w, jnp.eye(g, dtype=w.dtype)).reshape(g * n, g * n)


def kernel(x, c, rel_bias, ada_w, ada_b, ln_g, ln_b, ffn_w_gate, ffn_w_up, ffn_w_down, w_in, w_out,
           ssm_a_re, ssm_a_im, ssm_log_dt, ssm_b_re, ssm_b_im, ssm_c_re, ssm_c_im, ssm_d, glu_w, glu_b,
           pool_w, pool_scale):
    bsz = x.shape[0]
    mod_all = _adaln(c, ada_w, ada_b).reshape(DEPTH, bsz, 9, D_MODEL)
    wg, wu, wd = ffn_w_gate, ffn_w_up, ffn_w_down
    w_in_b, w_out_b, glu_w_b = w_in.astype(BF16), w_out.astype(BF16), glu_w.astype(BF16)
    ln_g4, ln_b4 = ln_g.reshape(DEPTH, 3, 1, D_MODEL), ln_b.reshape(DEPTH, 3, 1, D_MODEL)
    glu_b3 = glu_b.reshape(DEPTH, 1, D_SSM)
    pool_scale3 = pool_scale.reshape(DEPTH, 1, D_POOL)
    pool_w_bd = jax.vmap(_block_diag)(pool_w).astype(BF16)
    biases = [_branch_bias(rel_bias, window, dilation, x.shape[1] // dilation > ATT_BLOCK)
              for window, dilation in DILATED_PATTERNS]
    tables = _ssm_tables(ssm_a_re, ssm_a_im, ssm_log_dt, ssm_b_re, ssm_b_im, ssm_c_re, ssm_c_im, ssm_d)
    for l in range(DEPTH):
        x = _ffn(x, mod_all, wg, wu, wd, ln_g4, ln_b4, l, 0)
        qkv, u_ssm, y_pool = _inproj(x, mod_all, w_in_b, pool_w_bd, pool_scale3, l)
        att = _dilated_attention(qkv, biases)
        y_ssm = _ssm(u_ssm, tables, l)
        x = _outproj(x, mod_all, att, y_ssm, y_pool, glu_w_b, glu_b3, w_out_b, ln_g4, ln_b4, l)
        x = _ffn(x, mod_all, wg, wu, wd, ln_g4, ln_b4, l, 1)
    return x
```

```python
import functools
import math

import jax
import jax.numpy as jnp
import numpy as np
from jax import lax
from jax.experimental import pallas as pl
from jax.experimental.pallas import tpu as pltpu

F32 = jnp.float32
BF16 = jnp.bfloat16

D_MODEL = 1024
DEPTH = 2
HEAD_DIM = 64
N_HEADS = 8
D_ATT = N_HEADS * HEAD_DIM
DILATED_PATTERNS = ((128, 1), (512, 4), (2048, 16))
DILATIONS = tuple(d for _, d in DILATED_PATTERNS)
ATT_BLOCK = 128
SSM_GROUP = 16
D_SSM = 256
N_SSM_GROUPS = D_SSM // SSM_GROUP
SSM_STATE = 64
POOL_WINDOWS = (2, 4, 8, 16)
D_POOL = 256
POOL_GROUP = D_POOL // len(POOL_WINDOWS)
D_IN = 3 * D_ATT + D_SSM + D_POOL
D_FF = 2816
N_BUCKETS = 32
MAX_DISTANCE = 2048
ALPHA = (2 * DEPTH) ** 0.25
FFN_RES = 0.5
LN_EPS = 1e-5
NEG = -1e30

LANES = 128
TOKEN_TILE = 512
FF_CHUNK = 512
FFN_TOKEN_TILE = 512
ADA_COL_TILE = 1152
SSM_CHUNK = 8
POOL_HALO = 2 * max(POOL_WINDOWS)
ATT_LOOKAHEAD = 8
ATT_STEP_BLOCKS = 8
MERGE_ROWS = 128
LOG2E = math.log2(math.e)
STAT_SUM_OFFSET = 16
VMEM_LIMIT = 56 * 1024 * 1024


def _sigmoid(x):
    return 1.0 / (1.0 + jnp.exp(-x))


def _layernorm(x):
    mu = jnp.mean(x, axis=-1, keepdims=True)
    xc = x - mu
    var = jnp.mean(xc * xc, axis=-1, keepdims=True)
    return xc * lax.rsqrt(var + LN_EPS)


def _resident(shape):
    zeros = (0,) * len(shape)
    return pl.BlockSpec(shape, lambda *_: zeros, pipeline_mode=pl.Buffered(1))


def _stacked(tail, *lead):
    idx = tuple(lead) + (0,) * len(tail)
    return pl.BlockSpec((None,) * len(lead) + tuple(tail), lambda *_: idx, pipeline_mode=pl.Buffered(1))


def _adaln_kernel(c_ref, w_ref, b_ref, o_ref):
    c = c_ref[...]
    cond = (c * _sigmoid(c)).astype(BF16)
    o_ref[...] = jnp.dot(cond, w_ref[...].astype(BF16), preferred_element_type=F32) + b_ref[...]


def _adaln(c, ada_w, ada_b):
    nl, d, n = ada_w.shape
    bsz = c.shape[0]
    return pl.pallas_call(
        _adaln_kernel,
        out_shape=jax.ShapeDtypeStruct((nl, bsz, n), F32),
        grid=(nl, n // ADA_COL_TILE),
        in_specs=[
            pl.BlockSpec((bsz, d), lambda l, j: (0, 0)),
            pl.BlockSpec((None, d, ADA_COL_TILE), lambda l, j: (l, 0, j)),
            pl.BlockSpec((None, 1, ADA_COL_TILE), lambda l, j: (l, 0, j)),
        ],
        out_specs=pl.BlockSpec((None, bsz, ADA_COL_TILE), lambda l, j: (l, 0, j)),
        compiler_params=pltpu.CompilerParams(vmem_limit_bytes=VMEM_LIMIT),
        name="adaln",
    )(c, ada_w, ada_b.reshape(nl, 1, n))


def _ff_chunks():
    chunks, c0 = [], 0
    while c0 < D_FF:
        cw = min(FF_CHUNK, D_FF - c0)
        chunks.append((c0, cw))
        c0 += cw
    return tuple(chunks)


def _ffn_kernel(x_ref, mod_ref, wg_ref, wu_ref, wd_ref, lg_ref, lb_ref, o_ref, acc_ref, *, sub):
    x = x_ref[...]
    shift = mod_ref[3 * sub + 0:3 * sub + 1, :]
    scale = mod_ref[3 * sub + 1:3 * sub + 2, :]
    gate = mod_ref[3 * sub + 2:3 * sub + 3, :]
    h = (_layernorm(x) * (1.0 + scale) + shift).astype(BF16)
    for idx, (c0, cw) in enumerate(_ff_chunks()):
        g = jnp.dot(h, wg_ref[:, c0:c0 + cw].astype(BF16), preferred_element_type=F32)
        u = jnp.dot(h, wu_ref[:, c0:c0 + cw].astype(BF16), preferred_element_type=F32)
        a = (g * _sigmoid(g) * u).astype(BF16)
        d = jnp.dot(a, wd_ref[c0:c0 + cw, :].astype(BF16), preferred_element_type=F32)
        if idx == 0:
            acc_ref[...] = d
        else:
            acc_ref[...] += d
    y = ALPHA * x + (FFN_RES * gate) * acc_ref[...]
    o_ref[...] = _layernorm(y) * lg_ref[...] + lb_ref[...]


def _mod_spec(l, d):
    return pl.BlockSpec((None, None, 9, d), lambda b, i: (l, b, 0, 0))


def _ffn(x, mod_all, wg, wu, wd, ln_g, ln_b, l, which):
    bsz, s, d = x.shape
    tm = FFN_TOKEN_TILE
    sub = 2 * which
    return pl.pallas_call(
        functools.partial(_ffn_kernel, sub=sub),
        out_shape=jax.ShapeDtypeStruct(x.shape, F32),
        grid=(bsz, s // tm),
        in_specs=[
            pl.BlockSpec((None, tm, d), lambda b, i: (b, i, 0)),
            _mod_spec(l, d),
            _stacked(wg.shape[2:], l, which),
            _stacked(wu.shape[2:], l, which),
            _stacked(wd.shape[2:], l, which),
            _stacked((1, d), l, sub),
            _stacked((1, d), l, sub),
        ],
        out_specs=pl.BlockSpec((None, tm, d), lambda b, i: (b, i, 0)),
        scratch_shapes=[pltpu.VMEM((tm, d), F32)],
        compiler_params=pltpu.CompilerParams(vmem_limit_bytes=VMEM_LIMIT),
        name=f"ffn{sub}",
    )(x, mod_all, wg, wu, wd, ln_g, ln_b)


def _inproj_kernel(x_ref, mod_ref, w_ref, pw_ref, ps_ref, *rest):
    qkv_refs = rest[:3 * len(DILATIONS)]
    us_ref, yp_ref, zs_ref, z4_ref = rest[3 * len(DILATIONS):3 * len(DILATIONS) + 4]
    e_refs = rest[3 * len(DILATIONS) + 4:]
    i = pl.program_id(1)
    tm = x_ref.shape[0]
    x = x_ref[...]
    shift = mod_ref[3:4, :]
    scale = mod_ref[4:5, :]
    h = (_layernorm(x) * (1.0 + scale) + shift).astype(BF16)
    z = jnp.dot(h, w_ref[...], preferred_element_type=F32)
    up = z[:, 3 * D_ATT + D_SSM:]

    slabs_per = D_ATT // LANES
    n_qkv = 3 * slabs_per
    n_slabs = (3 * D_ATT + D_SSM) // LANES
    for c in range(n_slabs):
        col = z[:, c * LANES:(c + 1) * LANES]
        zs_ref[c] = col * (HEAD_DIM ** -0.5 * LOG2E) if c < slabs_per else col
    assert DILATIONS == (1, 4, 16)
    q4rows, q16rows = tm // 4, tm // 16
    for c in range(n_qkv):
        which, lanes = c // slabs_per, slice((c % slabs_per) * LANES, (c % slabs_per + 1) * LANES)
        qkv_refs[which][0, :, lanes] = zs_ref[c].astype(BF16)
        for r4 in range(4):
            blk = zs_ref[c, pl.ds(r4, q4rows, stride=4), :]
            z4_ref[c, r4 * q4rows:(r4 + 1) * q4rows, :] = blk
            qkv_refs[3 + which][r4, :, lanes] = blk.astype(BF16)
        for r4 in range(4):
            for c4 in range(4):
                blk = z4_ref[c, pl.ds(r4 * q4rows + c4, q16rows, stride=4), :]
                qkv_refs[6 + which][r4 + 4 * c4, :, lanes] = blk.astype(BF16)
    for c in range(n_qkv, n_slabs):
        for j in range(SSM_CHUNK):
            lo = j * D_SSM + (c - n_qkv) * LANES
            us_ref[:, lo:lo + LANES] = zs_ref[c, pl.ds(j, tm // SSM_CHUNK, stride=SSM_CHUNK), :]

    e1, e2, e4, e8 = e_refs
    hl = POOL_HALO
    assert POOL_WINDOWS == (2, 4, 8, 16) and hl == 32

    @pl.when(i == 0)
    def _():
        e1[0:hl, :] = jnp.zeros((hl, D_POOL), F32)

    @pl.when(i > 0)
    def _():
        e1[0:hl, :] = e1[tm:tm + hl, :]

    e1[hl:hl + tm, :] = up
    e2[8:, :] = e1[8:, :] + e1[7:tm + hl - 1, :]
    e4[16:, :] = e2[16:, :] + e2[14:tm + hl - 2, :]
    e8[24:, :] = e4[24:, :] + e4[20:tm + hl - 4, :]
    sums = {2: e2[hl:, :], 4: e4[hl:, :], 8: e8[hl:, :], 16: e8[hl:, :] + e8[hl - 8:tm + hl - 8, :]}
    pos = (i * tm + lax.broadcasted_iota(jnp.int32, (tm, 1), 0) + 1).astype(F32)
    group = lax.broadcasted_iota(jnp.int32, (1, D_POOL), 1) // POOL_GROUP
    mean = sums[POOL_WINDOWS[-1]] / jnp.minimum(pos, float(POOL_WINDOWS[-1]))
    for gi in range(len(POOL_WINDOWS) - 2, -1, -1):
        w = POOL_WINDOWS[gi]
        mean = jnp.where(group == gi, sums[w] / jnp.minimum(pos, float(w)), mean)
    pooled = (mean - up).astype(BF16)
    yp = jnp.dot(pooled, pw_ref[...], preferred_element_type=F32) * ps_ref[...]
    yp_ref[...] = yp.astype(BF16)


def _inproj(x, mod_all, w_in, pool_w_bd, pool_scale, l):
    bsz, s, d = x.shape
    tm = TOKEN_TILE
    tok = lambda width: pl.BlockSpec((None, tm, width), lambda b, i: (b, i, 0))
    qkv_shapes, qkv_specs = [], []
    for dil in DILATIONS:
        for _ in range(3):
            qkv_shapes.append(jax.ShapeDtypeStruct((bsz, dil, s // dil, D_ATT), BF16))
            qkv_specs.append(pl.BlockSpec((None, dil, tm // dil, D_ATT), lambda b, i: (b, 0, i, 0)))
    res = pl.pallas_call(
        _inproj_kernel,
        out_shape=tuple(qkv_shapes) + (
            jax.ShapeDtypeStruct((bsz, s // SSM_CHUNK, SSM_CHUNK * D_SSM), F32),
            jax.ShapeDtypeStruct((bsz, s, D_POOL), BF16),
        ),
        grid=(bsz, s // tm),
        in_specs=[
            tok(d),
            _mod_spec(l, d),
            _stacked(w_in.shape[1:], l),
            _stacked(pool_w_bd.shape[1:], l),
            _stacked((1, D_POOL), l),
        ],
        out_specs=tuple(qkv_specs) + (
            pl.BlockSpec((None, tm // SSM_CHUNK, SSM_CHUNK * D_SSM), lambda b, i: (b, i, 0)), tok(D_POOL)),
        scratch_shapes=[pltpu.VMEM(((3 * D_ATT + D_SSM) // LANES, tm, LANES), F32),
                        pltpu.VMEM((3 * D_ATT // LANES, tm, LANES), F32)]
        + [pltpu.VMEM((tm + POOL_HALO, D_POOL), F32) for _ in range(4)],
        compiler_params=pltpu.CompilerParams(
            dimension_semantics=("arbitrary", "arbitrary"), vmem_limit_bytes=VMEM_LIMIT),
        name="inproj",
    )(x, mod_all, w_in, pool_w_bd, pool_scale)
    qkv = [tuple(res[3 * di:3 * di + 3]) for di in range(len(DILATIONS))]
    return qkv, res[-2], res[-1]


def _t5_bucket(dist):
    max_exact = N_BUCKETS // 2
    dd = np.maximum(dist, 1).astype(np.float32)
    large = max_exact + (np.log(dd / max_exact) / math.log(MAX_DISTANCE / max_exact)
                         * (N_BUCKETS - max_exact)).astype(np.int32)
    large = np.minimum(large, N_BUCKETS - 1)
    return np.where(dist < max_exact, dist, large).astype(np.int32)


def _branch_bias(rel_bias, window, dilation, has_prev):
    qb = ATT_BLOCK
    n_keys = window // dilation
    assert n_keys == qb
    period = 3 * qb
    dist = np.arange(n_keys, -1, -1)
    row = rel_bias[_t5_bucket(dist * dilation)].T.astype(F32) * LOG2E
    row = jnp.concatenate([row, jnp.full((N_HEADS, period - n_keys - 1), NEG, F32)], axis=1)
    flat = jnp.tile(row, (1, qb))[:, :qb * (period - 1)]
    bias = flat.reshape(N_HEADS, qb, period - 1)[:, :, :2 * qb]
    if not has_prev:
        return bias[:, :, qb:]
    first = jnp.concatenate([jnp.full((N_HEADS, qb, qb), NEG, F32), bias[:, :, qb:]], axis=2)
    return jnp.stack([bias, first], 0)


def _stat_lane(h):
    return h if h % 2 == 1 else HEAD_DIM + h


def _attn_kernel(*refs, has_prev):
    qb = ATT_BLOCK
    if has_prev:
        q_ref, kc_ref, kh_ref, vc_ref, vh_ref, bias_ref, o_ref, st_ref, kbuf, vbuf = refs
        nsub = q_ref.shape[0] // qb
        kbuf[0:qb, :] = kh_ref[...]
        kbuf[qb:, :] = kc_ref[...]
        vbuf[0:qb, :] = vh_ref[...]
        vbuf[qb:, :] = vc_ref[...]
    else:
        q_ref, kc_ref, vc_ref, bias_ref, o_ref, st_ref = refs
        nsub = q_ref.shape[0]

    pair = 2 * HEAD_DIM
    assert pair == LANES
    lane = lax.broadcasted_iota(jnp.int32, (1, pair), 1)
    low = lane < HEAD_DIM
    stat_lane = lane

    def block(jj, carry):
        if has_prev:
            row0 = pl.multiple_of(jj * qb, qb)
            first = jnp.logical_and(pl.program_id(2) == 0, jj == 0).astype(jnp.int32)
            q_at = lambda cols: q_ref[pl.ds(row0, qb), cols]
            k_at = lambda cols: kbuf[pl.ds(row0, 2 * qb), cols]
            v_at = lambda cols: vbuf[pl.ds(row0, 2 * qb), cols]
            bias_at = lambda h: bias_ref[first, h]
        else:
            q_at = lambda cols: q_ref[jj, :, cols]
            k_at = lambda cols: kc_ref[jj, :, cols]
            v_at = lambda cols: vc_ref[jj, :, cols]
            bias_at = lambda h: bias_ref[h]

        def pair_cols(h):
            return slice((h // 2) * pair, (h // 2 + 1) * pair)

        def scores(h):
            q2 = q_at(pair_cols(h))
            qh = jnp.where(low, q2, jnp.zeros_like(q2)) if h % 2 == 0 else jnp.where(low, jnp.zeros_like(q2), q2)
            s = lax.dot_general(qh, k_at(pair_cols(h)), (((1,), (1,)), ((), ())), preferred_element_type=F32)
            return s + bias_at(h)

        pending = {h: scores(h) for h in range(min(ATT_LOOKAHEAD, N_HEADS))}
        outs = {}
        stats = jnp.ones((qb, LANES), F32)
        for h in range(N_HEADS):
            s = pending.pop(h)
            m = jnp.max(s, axis=-1, keepdims=True)
            p = jnp.exp2(s - m)
            if h + ATT_LOOKAHEAD < N_HEADS:
                pending[h + ATT_LOOKAHEAD] = scores(h + ATT_LOOKAHEAD)
            cols = pair_cols(h)
            v2 = v_at(cols)
            v2 = jnp.where(low, v2, jnp.ones_like(v2)) if h % 2 == 0 else jnp.where(low, jnp.ones_like(v2), v2)
            pv = jnp.dot(p.astype(BF16), v2, preferred_element_type=F32)
            outs[h] = pv
            stats = jnp.where(stat_lane == _stat_lane(h), m, stats)
            stats = jnp.where(stat_lane == _stat_lane(h) + STAT_SUM_OFFSET, pv, stats)
            if h % 2 == 1:
                o2 = jnp.where(low, outs.pop(h - 1), outs.pop(h)).astype(BF16)
                if has_prev:
                    o_ref[pl.ds(row0, qb), cols] = o2
                else:
                    o_ref[jj, :, cols] = o2
        if has_prev:
            st_ref[pl.ds(row0, qb), :] = stats
        else:
            st_ref[jj] = stats
        return carry

    lax.fori_loop(0, nsub, block, 0, unroll=True)


def _attn_branch(q, k, v, bias):
    bsz, d, ln, _ = q.shape
    qb = ATT_BLOCK
    has_prev = ln > qb
    out_shape = (jax.ShapeDtypeStruct((bsz, d, ln, D_ATT), BF16),
                 jax.ShapeDtypeStruct((bsz, d, ln, LANES), F32))
    if has_prev:
        rows = min(ATT_STEP_BLOCKS * qb, ln)
        per = rows // qb
        cur = lambda width: pl.BlockSpec((None, None, rows, width), lambda b, r, j: (b, r, j, 0))
        halo = pl.BlockSpec((None, None, qb, D_ATT), lambda b, r, j: (b, r, jnp.maximum(j * per - 1, 0), 0))
        grid = (bsz, d, ln // rows)
        args = [q, k, k, v, v, bias]
        specs = [cur(D_ATT), cur(D_ATT), halo, cur(D_ATT), halo, _resident(bias.shape)]
        scratch = [pltpu.VMEM((rows + qb, D_ATT), BF16), pltpu.VMEM((rows + qb, D_ATT), BF16)]
    else:
        per = min(ATT_STEP_BLOCKS, d)
        cur = lambda width: pl.BlockSpec((None, per, qb, width), lambda b, r: (b, r, 0, 0))
        grid = (bsz, d // per)
        args = [q, k, v, bias]
        specs = [cur(D_ATT), cur(D_ATT), cur(D_ATT), _resident(bias.shape)]
        scratch = []
    return pl.pallas_call(
        functools.partial(_attn_kernel, has_prev=has_prev),
        out_shape=out_shape,
        grid=grid,
        in_specs=specs,
        out_specs=(cur(D_ATT), cur(LANES)),
        scratch_shapes=scratch,
        compiler_params=pltpu.CompilerParams(vmem_limit_bytes=VMEM_LIMIT),
        name=f"attn_d{d}",
    )(*args)


def _dilated_attention(qkv, biases):
    return [_attn_branch(q, k, v, bias) for (q, k, v), bias in zip(qkv, biases)]


def _ssm_tables(*params):
    t, g, p, c = SSM_CHUNK, N_SSM_GROUPS, SSM_STATE, SSM_GROUP
    n = t * g * c
    w_in, w_intra, w_out, abar_t, d_row = jax.vmap(_ssm_group_blocks)(*params)
    nl = w_in.shape[0]

    def expand(dense, row_div, col_div, spread):
        full = jnp.dot(dense.reshape(nl * n, t * c).astype(BF16), spread, preferred_element_type=F32)
        rg = ((lax.broadcasted_iota(jnp.int32, (nl * n, n), 0) % n) // row_div) % g
        cg = (lax.broadcasted_iota(jnp.int32, (nl * n, n), 1) // col_div) % g
        return jnp.where(rg == cg, full, 0.0).astype(BF16).reshape(nl, n, n)

    kk = np.arange(t * c)[:, None]
    cc = np.arange(n)[None, :]
    spread_rp = jnp.asarray((cc // (g * p) == kk // p) & (cc % p == kk % p), BF16)
    spread_jc = jnp.asarray((cc // (g * c) == kk // c) & (cc % c == kk % c), BF16)
    return (expand(w_in, c, p, spread_rp), expand(w_intra, c, c, spread_jc),
            expand(w_out, p, c, spread_jc), abar_t, d_row)


def _ssm_group_blocks(a_re, a_im, log_dt, b_re, b_im, c_re, c_im, d_skip):
    hi = lax.Precision.HIGHEST
    t, g, p, c = SSM_CHUNK, N_SSM_GROUPS, SSM_STATE, SSM_GROUP
    dt = jnp.exp(log_dt)[:, None]
    mag = jnp.exp(a_re * dt)
    ar, ai = mag * jnp.cos(a_im * dt), mag * jnp.sin(a_im * dt)
    den = a_re * a_re + a_im * a_im
    fr = ((ar - 1.0) * a_re + ai * a_im) / den
    fi = (ai * a_re - (ar - 1.0) * a_im) / den
    bbr = fr[:, :, None] * b_re - fi[:, :, None] * b_im
    bbi = fr[:, :, None] * b_im + fi[:, :, None] * b_re
    pr, pi_ = [jnp.ones_like(ar)], [jnp.zeros_like(ar)]
    for _ in range(t):
        pr.append(pr[-1] * ar - pi_[-1] * ai)
        pi_.append(pr[-2] * ai + pi_[-1] * ar)
    n = t * g * c
    assert n == 2 * g * p and t * c == 2 * p

    bbr_t, bbi_t = jnp.transpose(bbr, (0, 2, 1)), jnp.transpose(bbi, (0, 2, 1))
    win = []
    for j in range(t):
        qr, qi = pr[t - 1 - j][:, None, :], pi_[t - 1 - j][:, None, :]
        win.append(jnp.concatenate([qr * bbr_t - qi * bbi_t, qr * bbi_t + qi * bbr_t], axis=-1))
    w_in = jnp.stack(win, 0).reshape(n, 2 * p)

    c_re_t, c_im_t = jnp.transpose(c_re, (0, 2, 1)), jnp.transpose(c_im, (0, 2, 1))
    wr_cols, wi_cols, hs = [], [], []
    for j in range(t + 1):
        if j >= 1:
            wr_cols.append(c_re_t * pr[j][:, :, None] - c_im_t * pi_[j][:, :, None])
            wi_cols.append(c_re_t * pi_[j][:, :, None] + c_im_t * pr[j][:, :, None])
        if j < t:
            wr = c_re * pr[j][:, None, :] - c_im * pi_[j][:, None, :]
            wi = c_re * pi_[j][:, None, :] + c_im * pr[j][:, None, :]
            hs.append(jnp.einsum('gcp,gpd->gdc', wr, bbr, precision=hi)
                      - jnp.einsum('gcp,gpd->gdc', wi, bbi, precision=hi))
    w_out = jnp.stack([jnp.concatenate(wr_cols, axis=-1), -jnp.concatenate(wi_cols, axis=-1)],
                      0).reshape(n, t * c)

    hcat = jnp.concatenate(hs, axis=-1)
    rows = [hcat if jp == 0 else
            jnp.concatenate([jnp.zeros((g, c, jp * c), F32), hcat[:, :, :(t - jp) * c]], axis=-1)
            for jp in range(t)]
    w_intra = jnp.stack(rows, 0).reshape(n, t * c)

    abar_t = jnp.concatenate([pr[t].reshape(1, g * p), pi_[t].reshape(1, g * p)], axis=1)
    d_row = jnp.tile(d_skip.reshape(1, D_SSM), (1, t))
    return w_in, w_intra, w_out, abar_t, d_row


def _ssm_kernel(u_ref, win_ref, wintra_ref, wout_ref, abar_ref, d_ref, y_ref, z_ref, xp_ref):
    n = u_ref.shape[0]
    half = N_SSM_GROUPS * SSM_STATE
    u = u_ref[...]
    ub = u.astype(BF16)
    z_ref[...] = jnp.dot(ub, win_ref[...], preferred_element_type=F32)
    ar = abar_ref[:, 0:half]
    ai = abar_ref[:, half:]

    def step(k, carry):
        xr, xi = carry
        xp_ref[pl.ds(k, 1), 0:half] = xr
        xp_ref[pl.ds(k, 1), half:] = xi
        zr = z_ref[pl.ds(k, 1), 0:half]
        zi = z_ref[pl.ds(k, 1), half:]
        return ar * xr - ai * xi + zr, ar * xi + ai * xr + zi

    zero = jnp.zeros((1, half), F32)
    lax.fori_loop(0, n, step, (zero, zero))
    y = jnp.dot(ub, wintra_ref[...], preferred_element_type=F32)
    y = y + jnp.dot(xp_ref[...].astype(BF16), wout_ref[...], preferred_element_type=F32)
    y_ref[...] = y + d_ref[...] * u


def _ssm(u, tables, l):
    w_in, w_intra, w_out, abar_t, d_row = tables
    bsz, n, width = u.shape
    row = pl.BlockSpec((None, n, width), lambda b: (b, 0, 0))
    return pl.pallas_call(
        _ssm_kernel,
        out_shape=jax.ShapeDtypeStruct((bsz, n, width), F32),
        grid=(bsz,),
        in_specs=[row] + [_stacked(t.shape[1:], l) for t in tables],
        out_specs=row,
        scratch_shapes=[pltpu.VMEM((n, w_in.shape[2]), F32), pltpu.VMEM((n, w_in.shape[2]), F32)],
        compiler_params=pltpu.CompilerParams(vmem_limit_bytes=VMEM_LIMIT),
        name="ssm",
    )(u, w_in, w_intra, w_out, abar_t, d_row)


def _outproj_kernel(x_ref, mod_ref, *rest):
    nd = len(DILATIONS)
    o_refs, l_refs = rest[0:nd], rest[nd:2 * nd]
    (ys_ref, yp_ref, gw_ref, gb_ref, wa_ref, ws_ref, wp_ref, lg_ref, lb_ref, o_ref) = rest[2 * nd:2 * nd + 10]
    on4, on16, ot4, ln4, ln16, lt4, ysn_ref, ya_ref = rest[2 * nd + 10:]
    tm = x_ref.shape[0]
    slabs = D_ATT // LANES
    assert DILATIONS == (1, 4, 16)
    r4rows, r16rows = tm // 4, tm // 16

    for j in range(SSM_CHUNK):
        for c in range(D_SSM // LANES):
            lo = j * D_SSM + c * LANES
            ysn_ref[c, pl.ds(j, tm // SSM_CHUNK, stride=SSM_CHUNK), :] = ys_ref[:, lo:lo + LANES]

    ys = jnp.concatenate([ysn_ref[c] for c in range(D_SSM // LANES)], axis=1)
    cdf = 0.5 * (1.0 + jnp.tanh(math.sqrt(2.0 / math.pi) * (ys + 0.044715 * (ys * ys * ys))))
    t = jnp.dot((ys * cdf).astype(BF16), gw_ref[...], preferred_element_type=F32) + gb_ref[...]
    yg = (ys * _sigmoid(t)).astype(BF16)
    y_rest = (jnp.dot(yg, ws_ref[...], preferred_element_type=F32)
              + jnp.dot(yp_ref[...], wp_ref[...], preferred_element_type=F32))

    for r4 in range(4):
        ln4[pl.ds(r4, r4rows, stride=4), :] = l_refs[1][r4]
        for c4 in range(4):
            lt4[pl.ds(r4 * r4rows + c4, r16rows, stride=4), :] = l_refs[2][r4 + 4 * c4]
        for c in range(slabs):
            lanes = slice(c * LANES, (c + 1) * LANES)
            on4[c, pl.ds(r4, r4rows, stride=4), :] = o_refs[1][r4, :, lanes].astype(F32)
            for c4 in range(4):
                ot4[c, pl.ds(r4 * r4rows + c4, r16rows, stride=4), :] = o_refs[2][r4 + 4 * c4, :, lanes].astype(F32)
    for r4 in range(4):
        ln16[pl.ds(r4, r4rows, stride=4), :] = lt4[r4 * r4rows:(r4 + 1) * r4rows, :]
        for c in range(slabs):
            on16[c, pl.ds(r4, r4rows, stride=4), :] = ot4[c, r4 * r4rows:(r4 + 1) * r4rows, :]

    head = lax.broadcasted_iota(jnp.int32, (LANES, D_ATT), 1) // HEAD_DIM
    stat_of_head = jnp.where(head % 2 == 1, head, HEAD_DIM + head)
    spread = (stat_of_head == lax.broadcasted_iota(jnp.int32, (LANES, D_ATT), 0)).astype(BF16)
    st_lane = lax.broadcasted_iota(jnp.int32, (1, LANES), 1)
    is_max_lane = functools.reduce(jnp.logical_or, [st_lane == _stat_lane(h) for h in range(N_HEADS)])
    for rc in range(tm // MERGE_ROWS):
        rs = slice(rc * MERGE_ROWS, (rc + 1) * MERGE_ROWS)
        st_nat = [l_refs[0][0, rs, :], ln4[rs, :], ln16[rs, :]]
        o_nat = [o_refs[0][0, rs, :].astype(F32),
                 jnp.concatenate([on4[c, rs, :] for c in range(slabs)], axis=1),
                 jnp.concatenate([on16[c, rs, :] for c in range(slabs)], axis=1)]
        mx = functools.reduce(jnp.maximum, st_nat)
        es = [jnp.exp2(st - mx) for st in st_nat]
        sums = [pltpu.roll(st, LANES - STAT_SUM_OFFSET, axis=1) for st in st_nat]
        inv = 1.0 / functools.reduce(lambda a, b: a + b, [e * l for e, l in zip(es, sums)])
        ya = None
        for e, ov in zip(es, o_nat):
            w = jnp.where(is_max_lane, e * inv, 0.0)
            wx = jnp.dot(w.astype(BF16), spread, preferred_element_type=F32)
            ya = wx * ov if ya is None else ya + wx * ov
        ya_ref[rs, :] = ya.astype(BF16)

    x = x_ref[...]
    gate = mod_ref[5:6, :]
    y = jnp.dot(ya_ref[...], wa_ref[...], preferred_element_type=F32) + y_rest
    r = ALPHA * x + gate * y
    o_ref[...] = _layernorm(r) * lg_ref[...] + lb_ref[...]


def _outproj(x, mod_all, att, y_ssm, y_pool, glu_w, glu_b, w_out, ln_g, ln_b, l):
    bsz, s, d = x.shape
    tm = TOKEN_TILE
    tok = lambda width: pl.BlockSpec((None, tm, width), lambda b, i: (b, i, 0))
    res = lambda dil, width: pl.BlockSpec((None, dil, tm // dil, width), lambda b, i: (b, 0, i, 0))
    rows_of = lambda r0, n: pl.BlockSpec((None, n, d), lambda *_: (l, r0 // n, 0), pipeline_mode=pl.Buffered(1))
    assert D_ATT % D_SSM == 0 and D_SSM == D_POOL
    slab = lambda n: pltpu.VMEM((n, tm, LANES), F32)
    scratch = [slab(D_ATT // LANES)] * 3 + [pltpu.VMEM((tm, LANES), F32)] * 3
    scratch += [slab(D_SSM // LANES), pltpu.VMEM((tm, D_ATT), BF16)]
    return pl.pallas_call(
        _outproj_kernel,
        out_shape=jax.ShapeDtypeStruct(x.shape, F32),
        grid=(bsz, s // tm),
        in_specs=[tok(d), _mod_spec(l, d)]
        + [res(dil, D_ATT) for dil in DILATIONS]
        + [res(dil, LANES) for dil in DILATIONS]
        + [pl.BlockSpec((None, tm // SSM_CHUNK, SSM_CHUNK * D_SSM), lambda b, i: (b, i, 0)), tok(D_POOL),
           _stacked(glu_w.shape[1:], l), _stacked((1, D_SSM), l),
           rows_of(0, D_ATT), rows_of(D_ATT, D_SSM), rows_of(D_ATT + D_SSM, D_POOL),
           _stacked((1, d), l, 1), _stacked((1, d), l, 1)],
        out_specs=tok(d),
        scratch_shapes=scratch,
        compiler_params=pltpu.CompilerParams(vmem_limit_bytes=VMEM_LIMIT),
        name="outproj",
    )(x, mod_all, *[o for o, _ in att], *[st for _, st in att], y_ssm, y_pool,
      glu_w, glu_b, w_out, w_out, w_out, ln_g, ln_b)


def _block_diag(w):
    g, n, _ = w.shape
    return jnp.einsum('gab,gh->gahb', w, jnp.eye(g, dtype=w.dtype)).reshape(g * n, g * n)


def kernel(x, c, rel_bias, ada_w, ada_b, ln_g, ln_b, ffn_w_gate, ffn_w_up, ffn_w_down, w_in, w_out,
           ssm_a_re, ssm_a_im, ssm_log_dt, ssm_b_re, ssm_b_im, ssm_c_re, ssm_c_im, ssm_d, glu_w, glu_b,
           pool_w, pool_scale):
    bsz = x.shape[0]
    mod_all = _adaln(c, ada_w, ada_b).reshape(DEPTH, bsz, 9, D_MODEL)
    wg, wu, wd = ffn_w_gate, ffn_w_up, ffn_w_down
    w_in_b, w_out_b, glu_w_b = w_in.astype(BF16), w_out.astype(BF16), glu_w.astype(BF16)
    ln_g4, ln_b4 = ln_g.reshape(DEPTH, 3, 1, D_MODEL), ln_b.reshape(DEPTH, 3, 1, D_MODEL)
    glu_b3 = glu_b.reshape(DEPTH, 1, D_SSM)
    pool_scale3 = pool_scale.reshape(DEPTH, 1, D_POOL)
    pool_w_bd = jax.vmap(_block_diag)(pool_w).astype(BF16)
    biases = [_branch_bias(rel_bias, window, dilation, x.shape[1] // dilation > ATT_BLOCK)
              for window, dilation in DILATED_PATTERNS]
    tables = _ssm_tables(ssm_a_re, ssm_a_im, ssm_log_dt, ssm_b_re, ssm_b_im, ssm_c_re, ssm_c_im, ssm_d)
    for l in range(DEPTH):
        x = _ffn(x, mod_all, wg, wu, wd, ln_g4, ln_b4, l, 0)
        qkv, u_ssm, y_pool = _inproj(x, mod_all, w_in_b, pool_w_bd, pool_scale3, l)
        att = _dilated_attention(qkv, biases)
        y_ssm = _ssm(u_ssm, tables, l)
        x = _outproj(x, mod_all, att, y_ssm, y_pool, glu_w_b, glu_b3, w_out_b, ln_g4, ln_b4, l)
        x = _ffn(x, mod_all, wg, wu, wd, ln_g4, ln_b4, l, 1)
    return x
```

```python
import functools
import math

import jax
import jax.numpy as jnp
import numpy as np
from jax import lax
from jax.experimental import pallas as pl
from jax.experimental.pallas import tpu as pltpu

F32 = jnp.float32
BF16 = jnp.bfloat16

D_MODEL = 1024
DEPTH = 2
HEAD_DIM = 64
N_HEADS = 8
D_ATT = N_HEADS * HEAD_DIM
DILATED_PATTERNS = ((128, 1), (512, 4), (2048, 16))
DILATIONS = tuple(d for _, d in DILATED_PATTERNS)
ATT_BLOCK = 128
SSM_GROUP = 16
D_SSM = 256
N_SSM_GROUPS = D_SSM // SSM_GROUP
SSM_STATE = 64
POOL_WINDOWS = (2, 4, 8, 16)
D_POOL = 256
POOL_GROUP = D_POOL // len(POOL_WINDOWS)
D_IN = 3 * D_ATT + D_SSM + D_POOL
D_FF = 2816
N_BUCKETS = 32
MAX_DISTANCE = 2048
ALPHA = (2 * DEPTH) ** 0.25
FFN_RES = 0.5
LN_EPS = 1e-5
NEG = -1e30

LANES = 128
TOKEN_TILE = 512
FF_CHUNK = 512
FFN_TOKEN_TILE = 512
ADA_COL_TILE = 1152
SSM_CHUNK = 8
SSM_EXPAND_ROWS = 256
POOL_HALO = 2 * max(POOL_WINDOWS)
ATT_LOOKAHEAD = 8
ATT_STEP_BLOCKS = 8
MERGE_ROWS = 128
LOG2E = math.log2(math.e)
STAT_SUM_OFFSET = 16
VMEM_LIMIT = 56 * 1024 * 1024


def _sigmoid(x):
    return 1.0 / (1.0 + jnp.exp(-x))


def _layernorm(x):
    mu = jnp.mean(x, axis=-1, keepdims=True)
    xc = x - mu
    var = jnp.mean(xc * xc, axis=-1, keepdims=True)
    return xc * lax.rsqrt(var + LN_EPS)


def _resident(shape):
    zeros = (0,) * len(shape)
    return pl.BlockSpec(shape, lambda *_: zeros, pipeline_mode=pl.Buffered(1))


def _stacked(tail, *lead):
    idx = tuple(lead) + (0,) * len(tail)
    return pl.BlockSpec((None,) * len(lead) + tuple(tail), lambda *_: idx, pipeline_mode=pl.Buffered(1))


def _adaln_kernel(c_ref, w_ref, b_ref, o_ref):
    c = c_ref[...]
    cond = (c * _sigmoid(c)).astype(BF16)
    o_ref[...] = jnp.dot(cond, w_ref[...].astype(BF16), preferred_element_type=F32) + b_ref[...]


def _adaln(c, ada_w, ada_b):
    nl, d, n = ada_w.shape
    bsz = c.shape[0]
    return pl.pallas_call(
        _adaln_kernel,
        out_shape=jax.ShapeDtypeStruct((nl, bsz, n), F32),
        grid=(nl, n // ADA_COL_TILE),
        in_specs=[
            pl.BlockSpec((bsz, d), lambda l, j: (0, 0)),
            pl.BlockSpec((None, d, ADA_COL_TILE), lambda l, j: (l, 0, j)),
            pl.BlockSpec((None, 1, ADA_COL_TILE), lambda l, j: (l, 0, j)),
        ],
        out_specs=pl.BlockSpec((None, bsz, ADA_COL_TILE), lambda l, j: (l, 0, j)),
        compiler_params=pltpu.CompilerParams(vmem_limit_bytes=VMEM_LIMIT),
        name="adaln",
    )(c, ada_w, ada_b.reshape(nl, 1, n))


def _ff_chunks():
    chunks, c0 = [], 0
    while c0 < D_FF:
        cw = min(FF_CHUNK, D_FF - c0)
        chunks.append((c0, cw))
        c0 += cw
    return tuple(chunks)


def _ffn_kernel(x_ref, mod_ref, wg_ref, wu_ref, wd_ref, lg_ref, lb_ref, o_ref, acc_ref, *, sub):
    x = x_ref[...]
    shift = mod_ref[3 * sub + 0:3 * sub + 1, :]
    scale = mod_ref[3 * sub + 1:3 * sub + 2, :]
    gate = mod_ref[3 * sub + 2:3 * sub + 3, :]
    h = (_layernorm(x) * (1.0 + scale) + shift).astype(BF16)
    for idx, (c0, cw) in enumerate(_ff_chunks()):
        g = jnp.dot(h, wg_ref[:, c0:c0 + cw].astype(BF16), preferred_element_type=F32)
        u = jnp.dot(h, wu_ref[:, c0:c0 + cw].astype(BF16), preferred_element_type=F32)
        a = (g * _sigmoid(g) * u).astype(BF16)
        d = jnp.dot(a, wd_ref[c0:c0 + cw, :].astype(BF16), preferred_element_type=F32)
        if idx == 0:
            acc_ref[...] = d
        else:
            acc_ref[...] += d
    y = ALPHA * x + (FFN_RES * gate) * acc_ref[...]
    o_ref[...] = _layernorm(y) * lg_ref[...] + lb_ref[...]


def _mod_spec(l, d):
    return pl.BlockSpec((None, None, 9, d), lambda b, i: (l, b, 0, 0))


def _ffn(x, mod_all, wg, wu, wd, ln_g, ln_b, l, which):
    bsz, s, d = x.shape
    tm = FFN_TOKEN_TILE
    sub = 2 * which
    return pl.pallas_call(
        functools.partial(_ffn_kernel, sub=sub),
        out_shape=jax.ShapeDtypeStruct(x.shape, F32),
        grid=(bsz, s // tm),
        in_specs=[
            pl.BlockSpec((None, tm, d), lambda b, i: (b, i, 0)),
            _mod_spec(l, d),
            _stacked(wg.shape[2:], l, which),
            _stacked(wu.shape[2:], l, which),
            _stacked(wd.shape[2:], l, which),
            _stacked((1, d), l, sub),
            _stacked((1, d), l, sub),
        ],
        out_specs=pl.BlockSpec((None, tm, d), lambda b, i: (b, i, 0)),
        scratch_shapes=[pltpu.VMEM((tm, d), F32)],
        compiler_params=pltpu.CompilerParams(vmem_limit_bytes=VMEM_LIMIT),
        name=f"ffn{sub}",
    )(x, mod_all, wg, wu, wd, ln_g, ln_b)


def _inproj_kernel(x_ref, mod_ref, w_ref, pw_ref, ps_ref, *rest):
    qkv_refs = rest[:3 * len(DILATIONS)]
    us_ref, yp_ref, zs_ref, z4_ref = rest[3 * len(DILATIONS):3 * len(DILATIONS) + 4]
    e_refs = rest[3 * len(DILATIONS) + 4:]
    i = pl.program_id(1)
    tm = x_ref.shape[0]
    x = x_ref[...]
    shift = mod_ref[3:4, :]
    scale = mod_ref[4:5, :]
    h = (_layernorm(x) * (1.0 + scale) + shift).astype(BF16)
    z = jnp.dot(h, w_ref[...], preferred_element_type=F32)
    up = z[:, 3 * D_ATT + D_SSM:]

    slabs_per = D_ATT // LANES
    n_qkv = 3 * slabs_per
    n_slabs = (3 * D_ATT + D_SSM) // LANES
    for c in range(n_slabs):
        col = z[:, c * LANES:(c + 1) * LANES]
        zs_ref[c] = col * (HEAD_DIM ** -0.5 * LOG2E) if c < slabs_per else col
    assert DILATIONS == (1, 4, 16)
    q4rows, q16rows = tm // 4, tm // 16
    for c in range(n_qkv):
        which, lanes = c // slabs_per, slice((c % slabs_per) * LANES, (c % slabs_per + 1) * LANES)
        qkv_refs[which][0, :, lanes] = zs_ref[c].astype(BF16)
        for r4 in range(4):
            blk = zs_ref[c, pl.ds(r4, q4rows, stride=4), :]
            z4_ref[c, r4 * q4rows:(r4 + 1) * q4rows, :] = blk
            qkv_refs[3 + which][r4, :, lanes] = blk.astype(BF16)
        for r4 in range(4):
            for c4 in range(4):
                blk = z4_ref[c, pl.ds(r4 * q4rows + c4, q16rows, stride=4), :]
                qkv_refs[6 + which][r4 + 4 * c4, :, lanes] = blk.astype(BF16)
    for c in range(n_qkv, n_slabs):
        for j in range(SSM_CHUNK):
            lo = j * D_SSM + (c - n_qkv) * LANES
            us_ref[:, lo:lo + LANES] = zs_ref[c, pl.ds(j, tm // SSM_CHUNK, stride=SSM_CHUNK), :]

    e1, e2, e4, e8 = e_refs
    hl = POOL_HALO
    assert POOL_WINDOWS == (2, 4, 8, 16) and hl == 32

    @pl.when(i == 0)
    def _():
        e1[0:hl, :] = jnp.zeros((hl, D_POOL), F32)

    @pl.when(i > 0)
    def _():
        e1[0:hl, :] = e1[tm:tm + hl, :]

    e1[hl:hl + tm, :] = up
    e2[8:, :] = e1[8:, :] + e1[7:tm + hl - 1, :]
    e4[16:, :] = e2[16:, :] + e2[14:tm + hl - 2, :]
    e8[24:, :] = e4[24:, :] + e4[20:tm + hl - 4, :]
    sums = {2: e2[hl:, :], 4: e4[hl:, :], 8: e8[hl:, :], 16: e8[hl:, :] + e8[hl - 8:tm + hl - 8, :]}
    pos = (i * tm + lax.broadcasted_iota(jnp.int32, (tm, 1), 0) + 1).astype(F32)
    group = lax.broadcasted_iota(jnp.int32, (1, D_POOL), 1) // POOL_GROUP
    mean = sums[POOL_WINDOWS[-1]] / jnp.minimum(pos, float(POOL_WINDOWS[-1]))
    for gi in range(len(POOL_WINDOWS) - 2, -1, -1):
        w = POOL_WINDOWS[gi]
        mean = jnp.where(group == gi, sums[w] / jnp.minimum(pos, float(w)), mean)
    pooled = (mean - up).astype(BF16)
    yp = jnp.dot(pooled, pw_ref[...], preferred_element_type=F32) * ps_ref[...]
    yp_ref[...] = yp.astype(BF16)


def _inproj(x, mod_all, w_in, pool_w_bd, pool_scale, l):
    bsz, s, d = x.shape
    tm = TOKEN_TILE
    tok = lambda width: pl.BlockSpec((None, tm, width), lambda b, i: (b, i, 0))
    qkv_shapes, qkv_specs = [], []
    for dil in DILATIONS:
        for _ in range(3):
            qkv_shapes.append(jax.ShapeDtypeStruct((bsz, dil, s // dil, D_ATT), BF16))
            qkv_specs.append(pl.BlockSpec((None, dil, tm // dil, D_ATT), lambda b, i: (b, 0, i, 0)))
    res = pl.pallas_call(
        _inproj_kernel,
        out_shape=tuple(qkv_shapes) + (
            jax.ShapeDtypeStruct((bsz, s // SSM_CHUNK, SSM_CHUNK * D_SSM), F32),
            jax.ShapeDtypeStruct((bsz, s, D_POOL), BF16),
        ),
        grid=(bsz, s // tm),
        in_specs=[
            tok(d),
            _mod_spec(l, d),
            _stacked(w_in.shape[1:], l),
            _stacked(pool_w_bd.shape[1:], l),
            _stacked((1, D_POOL), l),
        ],
        out_specs=tuple(qkv_specs) + (
            pl.BlockSpec((None, tm // SSM_CHUNK, SSM_CHUNK * D_SSM), lambda b, i: (b, i, 0)), tok(D_POOL)),
        scratch_shapes=[pltpu.VMEM(((3 * D_ATT + D_SSM) // LANES, tm, LANES), F32),
                        pltpu.VMEM((3 * D_ATT // LANES, tm, LANES), F32)]
        + [pltpu.VMEM((tm + POOL_HALO, D_POOL), F32) for _ in range(4)],
        compiler_params=pltpu.CompilerParams(
            dimension_semantics=("arbitrary", "arbitrary"), vmem_limit_bytes=VMEM_LIMIT),
        name="inproj",
    )(x, mod_all, w_in, pool_w_bd, pool_scale)
    qkv = [tuple(res[3 * di:3 * di + 3]) for di in range(len(DILATIONS))]
    return qkv, res[-2], res[-1]


def _t5_bucket(dist):
    max_exact = N_BUCKETS // 2
    dd = np.maximum(dist, 1).astype(np.float32)
    large = max_exact + (np.log(dd / max_exact) / math.log(MAX_DISTANCE / max_exact)
                         * (N_BUCKETS - max_exact)).astype(np.int32)
    large = np.minimum(large, N_BUCKETS - 1)
    return np.where(dist < max_exact, dist, large).astype(np.int32)


def _branch_bias(rel_bias, window, dilation, has_prev):
    qb = ATT_BLOCK
    n_keys = window // dilation
    assert n_keys == qb
    period = 3 * qb
    dist = np.arange(n_keys, -1, -1)
    row = rel_bias[_t5_bucket(dist * dilation)].T.astype(F32) * LOG2E
    row = jnp.concatenate([row, jnp.full((N_HEADS, period - n_keys - 1), NEG, F32)], axis=1)
    flat = jnp.tile(row, (1, qb))[:, :qb * (period - 1)]
    bias = flat.reshape(N_HEADS, qb, period - 1)[:, :, :2 * qb]
    if not has_prev:
        return bias[:, :, qb:]
    first = jnp.concatenate([jnp.full((N_HEADS, qb, qb), NEG, F32), bias[:, :, qb:]], axis=2)
    return jnp.stack([bias, first], 0)


def _stat_lane(h):
    return h if h % 2 == 1 else HEAD_DIM + h


def _attn_kernel(*refs, has_prev):
    qb = ATT_BLOCK
    if has_prev:
        q_ref, kc_ref, kh_ref, vc_ref, vh_ref, bias_ref, o_ref, st_ref, kbuf, vbuf = refs
        nsub = q_ref.shape[0] // qb
        kbuf[0:qb, :] = kh_ref[...]
        kbuf[qb:, :] = kc_ref[...]
        vbuf[0:qb, :] = vh_ref[...]
        vbuf[qb:, :] = vc_ref[...]
    else:
        q_ref, kc_ref, vc_ref, bias_ref, o_ref, st_ref = refs
        nsub = q_ref.shape[0]

    pair = 2 * HEAD_DIM
    assert pair == LANES
    lane = lax.broadcasted_iota(jnp.int32, (1, pair), 1)
    low = lane < HEAD_DIM
    stat_lane = lane

    def block(jj, carry):
        if has_prev:
            row0 = pl.multiple_of(jj * qb, qb)
            first = jnp.logical_and(pl.program_id(2) == 0, jj == 0).astype(jnp.int32)
            q_at = lambda cols: q_ref[pl.ds(row0, qb), cols]
            k_at = lambda cols: kbuf[pl.ds(row0, 2 * qb), cols]
            v_at = lambda cols: vbuf[pl.ds(row0, 2 * qb), cols]
            bias_at = lambda h: bias_ref[first, h]
        else:
            q_at = lambda cols: q_ref[jj, :, cols]
            k_at = lambda cols: kc_ref[jj, :, cols]
            v_at = lambda cols: vc_ref[jj, :, cols]
            bias_at = lambda h: bias_ref[h]

        def pair_cols(h):
            return slice((h // 2) * pair, (h // 2 + 1) * pair)

        def scores(h):
            q2 = q_at(pair_cols(h))
            qh = jnp.where(low, q2, jnp.zeros_like(q2)) if h % 2 == 0 else jnp.where(low, jnp.zeros_like(q2), q2)
            s = lax.dot_general(qh, k_at(pair_cols(h)), (((1,), (1,)), ((), ())), preferred_element_type=F32)
            return s + bias_at(h)

        pending = {h: scores(h) for h in range(min(ATT_LOOKAHEAD, N_HEADS))}
        outs = {}
        stats = jnp.ones((qb, LANES), F32)
        for h in range(N_HEADS):
            s = pending.pop(h)
            m = jnp.max(s, axis=-1, keepdims=True)
            p = jnp.exp2(s - m)
            if h + ATT_LOOKAHEAD < N_HEADS:
                pending[h + ATT_LOOKAHEAD] = scores(h + ATT_LOOKAHEAD)
            cols = pair_cols(h)
            v2 = v_at(cols)
            v2 = jnp.where(low, v2, jnp.ones_like(v2)) if h % 2 == 0 else jnp.where(low, jnp.ones_like(v2), v2)
            pv = jnp.dot(p.astype(BF16), v2, preferred_element_type=F32)
            outs[h] = pv
            stats = jnp.where(stat_lane == _stat_lane(h), m, stats)
            stats = jnp.where(stat_lane == _stat_lane(h) + STAT_SUM_OFFSET, pv, stats)
            if h % 2 == 1:
                o2 = jnp.where(low, outs.pop(h - 1), outs.pop(h)).astype(BF16)
                if has_prev:
                    o_ref[pl.ds(row0, qb), cols] = o2
                else:
                    o_ref[jj, :, cols] = o2
        if has_prev:
            st_ref[pl.ds(row0, qb), :] = stats
        else:
            st_ref[jj] = stats
        return carry

    lax.fori_loop(0, nsub, block, 0, unroll=True)


def _attn_branch(q, k, v, bias):
    bsz, d, ln, _ = q.shape
    qb = ATT_BLOCK
    has_prev = ln > qb
    out_shape = (jax.ShapeDtypeStruct((bsz, d, ln, D_ATT), BF16),
                 jax.ShapeDtypeStruct((bsz, d, ln, LANES), F32))
    if has_prev:
        rows = min(ATT_STEP_BLOCKS * qb, ln)
        per = rows // qb
        cur = lambda width: pl.BlockSpec((None, None, rows, width), lambda b, r, j: (b, r, j, 0))
        halo = pl.BlockSpec((None, None, qb, D_ATT), lambda b, r, j: (b, r, jnp.maximum(j * per - 1, 0), 0))
        grid = (bsz, d, ln // rows)
        args = [q, k, k, v, v, bias]
        specs = [cur(D_ATT), cur(D_ATT), halo, cur(D_ATT), halo, _resident(bias.shape)]
        scratch = [pltpu.VMEM((rows + qb, D_ATT), BF16), pltpu.VMEM((rows + qb, D_ATT), BF16)]
    else:
        per = min(ATT_STEP_BLOCKS, d)
        cur = lambda width: pl.BlockSpec((None, per, qb, width), lambda b, r: (b, r, 0, 0))
        grid = (bsz, d // per)
        args = [q, k, v, bias]
        specs = [cur(D_ATT), cur(D_ATT), cur(D_ATT), _resident(bias.shape)]
        scratch = []
    return pl.pallas_call(
        functools.partial(_attn_kernel, has_prev=has_prev),
        out_shape=out_shape,
        grid=grid,
        in_specs=specs,
        out_specs=(cur(D_ATT), cur(LANES)),
        scratch_shapes=scratch,
        compiler_params=pltpu.CompilerParams(vmem_limit_bytes=VMEM_LIMIT),
        name=f"attn_d{d}",
    )(*args)


def _dilated_attention(qkv, biases):
    return [_attn_branch(q, k, v, bias) for (q, k, v), bias in zip(qkv, biases)]


def _ssm_tables(*params):
    t, g, p, c = SSM_CHUNK, N_SSM_GROUPS, SSM_STATE, SSM_GROUP
    n = t * g * c
    w_in, w_intra, w_out, abar_t, d_row = jax.vmap(_ssm_group_blocks)(*params)
    kk = np.arange(t * c)[:, None]
    cc = np.arange(n)[None, :]
    spread_rp = jnp.asarray((cc // (g * p) == kk // p) & (cc % p == kk % p), BF16)
    spread_jc = jnp.asarray((cc // (g * c) == kk // c) & (cc % c == kk % c), BF16)
    blocks = (w_in.astype(BF16), w_intra.astype(BF16), w_out.astype(BF16))
    return blocks, (spread_rp, spread_jc), abar_t, d_row


def _ssm_group_blocks(a_re, a_im, log_dt, b_re, b_im, c_re, c_im, d_skip):
    hi = lax.Precision.HIGHEST
    t, g, p, c = SSM_CHUNK, N_SSM_GROUPS, SSM_STATE, SSM_GROUP
    dt = jnp.exp(log_dt)[:, None]
    mag = jnp.exp(a_re * dt)
    ar, ai = mag * jnp.cos(a_im * dt), mag * jnp.sin(a_im * dt)
    den = a_re * a_re + a_im * a_im
    fr = ((ar - 1.0) * a_re + ai * a_im) / den
    fi = (ai * a_re - (ar - 1.0) * a_im) / den
    bbr = fr[:, :, None] * b_re - fi[:, :, None] * b_im
    bbi = fr[:, :, None] * b_im + fi[:, :, None] * b_re
    pr, pi_ = [jnp.ones_like(ar)], [jnp.zeros_like(ar)]
    for _ in range(t):
        pr.append(pr[-1] * ar - pi_[-1] * ai)
        pi_.append(pr[-2] * ai + pi_[-1] * ar)
    n = t * g * c
    assert n == 2 * g * p and t * c == 2 * p

    bbr_t, bbi_t = jnp.transpose(bbr, (0, 2, 1)), jnp.transpose(bbi, (0, 2, 1))
    win = []
    for j in range(t):
        qr, qi = pr[t - 1 - j][:, None, :], pi_[t - 1 - j][:, None, :]
        win.append(jnp.concatenate([qr * bbr_t - qi * bbi_t, qr * bbi_t + qi * bbr_t], axis=-1))
    w_in = jnp.stack(win, 0).reshape(n, 2 * p)

    c_re_t, c_im_t = jnp.transpose(c_re, (0, 2, 1)), jnp.transpose(c_im, (0, 2, 1))
    wr_cols, wi_cols, hs = [], [], []
    for j in range(t + 1):
        if j >= 1:
            wr_cols.append(c_re_t * pr[j][:, :, None] - c_im_t * pi_[j][:, :, None])
            wi_cols.append(c_re_t * pi_[j][:, :, None] + c_im_t * pr[j][:, :, None])
        if j < t:
            wr = c_re * pr[j][:, None, :] - c_im * pi_[j][:, None, :]
            wi = c_re * pi_[j][:, None, :] + c_im * pr[j][:, None, :]
            hs.append(jnp.einsum('gcp,gpd->gdc', wr, bbr, precision=hi)
                      - jnp.einsum('gcp,gpd->gdc', wi, bbi, precision=hi))
    w_out = jnp.stack([jnp.concatenate(wr_cols, axis=-1), -jnp.concatenate(wi_cols, axis=-1)],
                      0).reshape(n, t * c)

    hcat = jnp.concatenate(hs, axis=-1)
    rows = [hcat if jp == 0 else
            jnp.concatenate([jnp.zeros((g, c, jp * c), F32), hcat[:, :, :(t - jp) * c]], axis=-1)
            for jp in range(t)]
    w_intra = jnp.stack(rows, 0).reshape(n, t * c)

    abar_t = jnp.concatenate([pr[t].reshape(1, g * p), pi_[t].reshape(1, g * p)], axis=1)
    d_row = jnp.tile(d_skip.reshape(1, D_SSM), (1, t))
    return w_in, w_intra, w_out, abar_t, d_row


def _ssm_kernel(u_ref, win_b, wintra_b, wout_b, srp_ref, sjc_ref, abar_ref, d_ref, y_ref,
                z_ref, xp_ref, win_ref, wintra_ref, wout_ref):
    n = u_ref.shape[0]
    half = N_SSM_GROUPS * SSM_STATE
    c, p, g = SSM_GROUP, SSM_STATE, N_SSM_GROUPS

    @pl.when(pl.program_id(0) == 0)
    def _():
        size = win_ref.shape[0]
        rows_per = SSM_EXPAND_ROWS
        col = lax.broadcasted_iota(jnp.int32, (rows_per, size), 1)
        row = lax.broadcasted_iota(jnp.int32, (rows_per, size), 0)
        for blk, dst, row_div, col_div, spread in ((win_b, win_ref, c, p, srp_ref),
                                                   (wintra_b, wintra_ref, c, c, sjc_ref),
                                                   (wout_b, wout_ref, p, c, sjc_ref)):
            cg = (col // col_div) % g
            for r0 in range(0, size, rows_per):
                full = jnp.dot(blk[r0:r0 + rows_per, :], spread[...], preferred_element_type=F32)
                rg = ((row + r0) // row_div) % g
                dst[r0:r0 + rows_per, :] = jnp.where(rg == cg, full, 0.0).astype(BF16)

    u = u_ref[...]
    ub = u.astype(BF16)
    z_ref[...] = jnp.dot(ub, win_ref[...], preferred_element_type=F32)
    ar = abar_ref[:, 0:half]
    ai = abar_ref[:, half:]

    def step(k, carry):
        xr, xi = carry
        xp_ref[pl.ds(k, 1), 0:half] = xr
        xp_ref[pl.ds(k, 1), half:] = xi
        zr = z_ref[pl.ds(k, 1), 0:half]
        zi = z_ref[pl.ds(k, 1), half:]
        return ar * xr - ai * xi + zr, ar * xi + ai * xr + zi

    zero = jnp.zeros((1, half), F32)
    lax.fori_loop(0, n, step, (zero, zero))
    y = jnp.dot(ub, wintra_ref[...], preferred_element_type=F32)
    y = y + jnp.dot(xp_ref[...].astype(BF16), wout_ref[...], preferred_element_type=F32)
    y_ref[...] = y + d_ref[...] * u


def _ssm(u, tables, l):
    blocks, spreads, abar_t, d_row = tables
    bsz, n, width = u.shape
    size = blocks[0].shape[1]
    row = pl.BlockSpec((None, n, width), lambda b: (b, 0, 0))
    return pl.pallas_call(
        _ssm_kernel,
        out_shape=jax.ShapeDtypeStruct((bsz, n, width), F32),
        grid=(bsz,),
        in_specs=[row] + [_stacked(t.shape[1:], l) for t in blocks] + [_resident(sp.shape) for sp in spreads]
        + [_stacked(abar_t.shape[1:], l), _stacked(d_row.shape[1:], l)],
        out_specs=row,
        scratch_shapes=[pltpu.VMEM((n, size), F32), pltpu.VMEM((n, size), F32)]
        + [pltpu.VMEM((size, size), BF16)] * 3,
        compiler_params=pltpu.CompilerParams(dimension_semantics=("arbitrary",), vmem_limit_bytes=VMEM_LIMIT),
        name="ssm",
    )(u, *blocks, *spreads, abar_t, d_row)


def _outproj_kernel(x_ref, mod_ref, *rest):
    nd = len(DILATIONS)
    o_refs, l_refs = rest[0:nd], rest[nd:2 * nd]
    (ys_ref, yp_ref, gw_ref, gb_ref, wa_ref, ws_ref, wp_ref, lg_ref, lb_ref, o_ref) = rest[2 * nd:2 * nd + 10]
    on4, on16, ot4, ln4, ln16, lt4, ysn_ref, ya_ref = rest[2 * nd + 10:]
    tm = x_ref.shape[0]
    slabs = D_ATT // LANES
    assert DILATIONS == (1, 4, 16)
    r4rows, r16rows = tm // 4, tm // 16

    for j in range(SSM_CHUNK):
        for c in range(D_SSM // LANES):
            lo = j * D_SSM + c * LANES
            ysn_ref[c, pl.ds(j, tm // SSM_CHUNK, stride=SSM_CHUNK), :] = ys_ref[:, lo:lo + LANES]

    ys = jnp.concatenate([ysn_ref[c] for c in range(D_SSM // LANES)], axis=1)
    cdf = 0.5 * (1.0 + jnp.tanh(math.sqrt(2.0 / math.pi) * (ys + 0.044715 * (ys * ys * ys))))
    t = jnp.dot((ys * cdf).astype(BF16), gw_ref[...], preferred_element_type=F32) + gb_ref[...]
    yg = (ys * _sigmoid(t)).astype(BF16)
    y_rest = (jnp.dot(yg, ws_ref[...], preferred_element_type=F32)
              + jnp.dot(yp_ref[...], wp_ref[...], preferred_element_type=F32))

    for r4 in range(4):
        ln4[pl.ds(r4, r4rows, stride=4), :] = l_refs[1][r4]
        for c4 in range(4):
            lt4[pl.ds(r4 * r4rows + c4, r16rows, stride=4), :] = l_refs[2][r4 + 4 * c4]
        for c in range(slabs):
            lanes = slice(c * LANES, (c + 1) * LANES)
            on4[c, pl.ds(r4, r4rows, stride=4), :] = o_refs[1][r4, :, lanes].astype(F32)
            for c4 in range(4):
                ot4[c, pl.ds(r4 * r4rows + c4, r16rows, stride=4), :] = o_refs[2][r4 + 4 * c4, :, lanes].astype(F32)
    for r4 in range(4):
        ln16[pl.ds(r4, r4rows, stride=4), :] = lt4[r4 * r4rows:(r4 + 1) * r4rows, :]
        for c in range(slabs):
            on16[c, pl.ds(r4, r4rows, stride=4), :] = ot4[c, r4 * r4rows:(r4 + 1) * r4rows, :]

    head = lax.broadcasted_iota(jnp.int32, (LANES, D_ATT), 1) // HEAD_DIM
    stat_of_head = jnp.where(head % 2 == 1, head, HEAD_DIM + head)
    spread = (stat_of_head == lax.broadcasted_iota(jnp.int32, (LANES, D_ATT), 0)).astype(BF16)
    st_lane = lax.broadcasted_iota(jnp.int32, (1, LANES), 1)
    is_max_lane = functools.reduce(jnp.logical_or, [st_lane == _stat_lane(h) for h in range(N_HEADS)])
    for rc in range(tm // MERGE_ROWS):
        rs = slice(rc * MERGE_ROWS, (rc + 1) * MERGE_ROWS)
        st_nat = [l_refs[0][0, rs, :], ln4[rs, :], ln16[rs, :]]
        o_nat = [o_refs[0][0, rs, :].astype(F32),
                 jnp.concatenate([on4[c, rs, :] for c in range(slabs)], axis=1),
                 jnp.concatenate([on16[c, rs, :] for c in range(slabs)], axis=1)]
        mx = functools.reduce(jnp.maximum, st_nat)
        es = [jnp.exp2(st - mx) for st in st_nat]
        sums = [pltpu.roll(st, LANES - STAT_SUM_OFFSET, axis=1) for st in st_nat]
        inv = 1.0 / functools.reduce(lambda a, b: a + b, [e * l for e, l in zip(es, sums)])
        ya = None
        for e, ov in zip(es, o_nat):
            w = jnp.where(is_max_lane, e * inv, 0.0)
            wx = jnp.dot(w.astype(BF16), spread, preferred_element_type=F32)
            ya = wx * ov if ya is None else ya + wx * ov
        ya_ref[rs, :] = ya.astype(BF16)

    x = x_ref[...]
    gate = mod_ref[5:6, :]
    y = jnp.dot(ya_ref[...], wa_ref[...], preferred_element_type=F32) + y_rest
    r = ALPHA * x + gate * y
    o_ref[...] = _layernorm(r) * lg_ref[...] + lb_ref[...]


def _outproj(x, mod_all, att, y_ssm, y_pool, glu_w, glu_b, w_out, ln_g, ln_b, l):
    bsz, s, d = x.shape
    tm = TOKEN_TILE
    tok = lambda width: pl.BlockSpec((None, tm, width), lambda b, i: (b, i, 0))
    res = lambda dil, width: pl.BlockSpec((None, dil, tm // dil, width), lambda b, i: (b, 0, i, 0))
    rows_of = lambda r0, n: pl.BlockSpec((None, n, d), lambda *_: (l, r0 // n, 0), pipeline_mode=pl.Buffered(1))
    assert D_ATT % D_SSM == 0 and D_SSM == D_POOL
    slab = lambda n: pltpu.VMEM((n, tm, LANES), F32)
    scratch = [slab(D_ATT // LANES)] * 3 + [pltpu.VMEM((tm, LANES), F32)] * 3
    scratch += [slab(D_SSM // LANES), pltpu.VMEM((tm, D_ATT), BF16)]
    return pl.pallas_call(
        _outproj_kernel,
        out_shape=jax.ShapeDtypeStruct(x.shape, F32),
        grid=(bsz, s // tm),
        in_specs=[tok(d), _mod_spec(l, d)]
        + [res(dil, D_ATT) for dil in DILATIONS]
        + [res(dil, LANES) for dil in DILATIONS]
        + [pl.BlockSpec((None, tm // SSM_CHUNK, SSM_CHUNK * D_SSM), lambda b, i: (b, i, 0)), tok(D_POOL),
           _stacked(glu_w.shape[1:], l), _stacked((1, D_SSM), l),
           rows_of(0, D_ATT), rows_of(D_ATT, D_SSM), rows_of(D_ATT + D_SSM, D_POOL),
           _stacked((1, d), l, 1), _stacked((1, d), l, 1)],
        out_specs=tok(d),
        scratch_shapes=scratch,
        compiler_params=pltpu.CompilerParams(vmem_limit_bytes=VMEM_LIMIT),
        name="outproj",
    )(x, mod_all, *[o for o, _ in att], *[st for _, st in att], y_ssm, y_pool,
      glu_w, glu_b, w_out, w_out, w_out, ln_g, ln_b)


def _block_diag(w):
    g, n, _ = w.shape
    return jnp.einsum('gab,gh->gahb', w, jnp.eye(g, dtype=w.dtype)).reshape(g * n, g * n)


def kernel(x, c, rel_bias, ada_w, ada_b, ln_g, ln_b, ffn_w_gate, ffn_w_up, ffn_w_down, w_in, w_out,
           ssm_a_re, ssm_a_im, ssm_log_dt, ssm_b_re, ssm_b_im, ssm_c_re, ssm_c_im, ssm_d, glu_w, glu_b,
           pool_w, pool_scale):
    bsz = x.shape[0]
    mod_all = _adaln(c, ada_w, ada_b).reshape(DEPTH, bsz, 9, D_MODEL)
    wg, wu, wd = ffn_w_gate, ffn_w_up, ffn_w_down
    w_in_b, w_out_b, glu_w_b = w_in.astype(BF16), w_out.astype(BF16), glu_w.astype(BF16)
    ln_g4, ln_b4 = ln_g.reshape(DEPTH, 3, 1, D_MODEL), ln_b.reshape(DEPTH, 3, 1, D_MODEL)
    glu_b3 = glu_b.reshape(DEPTH, 1, D_SSM)
    pool_scale3 = pool_scale.reshape(DEPTH, 1, D_POOL)
    pool_w_bd = jax.vmap(_block_diag)(pool_w).astype(BF16)
    biases = [_branch_bias(rel_bias, window, dilation, x.shape[1] // dilation > ATT_BLOCK)
              for window, dilation in DILATED_PATTERNS]
    tables = _ssm_tables(ssm_a_re, ssm_a_im, ssm_log_dt, ssm_b_re, ssm_b_im, ssm_c_re, ssm_c_im, ssm_d)
    for l in range(DEPTH):
        x = _ffn(x, mod_all, wg, wu, wd, ln_g4, ln_b4, l, 0)
        qkv, u_ssm, y_pool = _inproj(x, mod_all, w_in_b, pool_w_bd, pool_scale3, l)
        att = _dilated_attention(qkv, biases)
        y_ssm = _ssm(u_ssm, tables, l)
        x = _outproj(x, mod_all, att, y_ssm, y_pool, glu_w_b, glu_b3, w_out_b, ln_g4, ln_b4, l)
        x = _ffn(x, mod_all, wg, wu, wd, ln_g4, ln_b4, l, 1)
    return x
```

```python
import functools
import math

import jax
import jax.numpy as jnp
import numpy as np
from jax import lax
from jax.experimental import pallas as pl
from jax.experimental.pallas import tpu as pltpu

F32 = jnp.float32
BF16 = jnp.bfloat16

D_MODEL = 1024
DEPTH = 2
HEAD_DIM = 64
N_HEADS = 8
D_ATT = N_HEADS * HEAD_DIM
DILATED_PATTERNS = ((128, 1), (512, 4), (2048, 16))
DILATIONS = tuple(d for _, d in DILATED_PATTERNS)
ATT_BLOCK = 128
SSM_GROUP = 16
D_SSM = 256
N_SSM_GROUPS = D_SSM // SSM_GROUP
SSM_STATE = 64
POOL_WINDOWS = (2, 4, 8, 16)
D_POOL = 256
POOL_GROUP = D_POOL // len(POOL_WINDOWS)
D_IN = 3 * D_ATT + D_SSM + D_POOL
D_FF = 2816
N_BUCKETS = 32
MAX_DISTANCE = 2048
ALPHA = (2 * DEPTH) ** 0.25
FFN_RES = 0.5
LN_EPS = 1e-5
NEG = -1e30

LANES = 128
TOKEN_TILE = 512
FF_CHUNK = 512
FFN_TOKEN_TILE = 512
ADA_COL_TILE = 1152
SSM_CHUNK = 8
SSM_EXPAND_ROWS = 256
POOL_HALO = 2 * max(POOL_WINDOWS)
ATT_LOOKAHEAD = 8
ATT_STEP_BLOCKS = 16
MERGE_ROWS = 128
LOG2E = math.log2(math.e)
STAT_SUM_OFFSET = 16
VMEM_LIMIT = 56 * 1024 * 1024


def _sigmoid(x):
    return 1.0 / (1.0 + jnp.exp(-x))


def _layernorm(x):
    mu = jnp.mean(x, axis=-1, keepdims=True)
    xc = x - mu
    var = jnp.mean(xc * xc, axis=-1, keepdims=True)
    return xc * lax.rsqrt(var + LN_EPS)


def _resident(shape):
    zeros = (0,) * len(shape)
    return pl.BlockSpec(shape, lambda *_: zeros, pipeline_mode=pl.Buffered(1))


def _stacked(tail, *lead):
    idx = tuple(lead) + (0,) * len(tail)
    return pl.BlockSpec((None,) * len(lead) + tuple(tail), lambda *_: idx, pipeline_mode=pl.Buffered(1))


def _adaln_kernel(c_ref, w_ref, b_ref, o_ref):
    c = c_ref[...]
    cond = (c * _sigmoid(c)).astype(BF16)
    o_ref[...] = jnp.dot(cond, w_ref[...].astype(BF16), preferred_element_type=F32) + b_ref[...]


def _adaln(c, ada_w, ada_b):
    nl, d, n = ada_w.shape
    bsz = c.shape[0]
    return pl.pallas_call(
        _adaln_kernel,
        out_shape=jax.ShapeDtypeStruct((nl, bsz, n), F32),
        grid=(nl, n // ADA_COL_TILE),
        in_specs=[
            pl.BlockSpec((bsz, d), lambda l, j: (0, 0)),
            pl.BlockSpec((None, d, ADA_COL_TILE), lambda l, j: (l, 0, j)),
            pl.BlockSpec((None, 1, ADA_COL_TILE), lambda l, j: (l, 0, j)),
        ],
        out_specs=pl.BlockSpec((None, bsz, ADA_COL_TILE), lambda l, j: (l, 0, j)),
        compiler_params=pltpu.CompilerParams(vmem_limit_bytes=VMEM_LIMIT),
        name="adaln",
    )(c, ada_w, ada_b.reshape(nl, 1, n))


def _ff_chunks():
    chunks, c0 = [], 0
    while c0 < D_FF:
        cw = min(FF_CHUNK, D_FF - c0)
        chunks.append((c0, cw))
        c0 += cw
    return tuple(chunks)


def _ffn_kernel(x_ref, mod_ref, wg_ref, wu_ref, wd_ref, lg_ref, lb_ref, o_ref, acc_ref, *, sub):
    x = x_ref[...]
    shift = mod_ref[3 * sub + 0:3 * sub + 1, :]
    scale = mod_ref[3 * sub + 1:3 * sub + 2, :]
    gate = mod_ref[3 * sub + 2:3 * sub + 3, :]
    h = (_layernorm(x) * (1.0 + scale) + shift).astype(BF16)
    for idx, (c0, cw) in enumerate(_ff_chunks()):
        g = jnp.dot(h, wg_ref[:, c0:c0 + cw].astype(BF16), preferred_element_type=F32)
        u = jnp.dot(h, wu_ref[:, c0:c0 + cw].astype(BF16), preferred_element_type=F32)
        a = (g * _sigmoid(g) * u).astype(BF16)
        d = jnp.dot(a, wd_ref[c0:c0 + cw, :].astype(BF16), preferred_element_type=F32)
        if idx == 0:
            acc_ref[...] = d
        else:
            acc_ref[...] += d
    y = ALPHA * x + (FFN_RES * gate) * acc_ref[...]
    o_ref[...] = _layernorm(y) * lg_ref[...] + lb_ref[...]


def _mod_spec(l, d):
    return pl.BlockSpec((None, None, 9, d), lambda b, i: (l, b, 0, 0))


def _ffn(x, mod_all, wg, wu, wd, ln_g, ln_b, l, which):
    bsz, s, d = x.shape
    tm = FFN_TOKEN_TILE
    sub = 2 * which
    return pl.pallas_call(
        functools.partial(_ffn_kernel, sub=sub),
        out_shape=jax.ShapeDtypeStruct(x.shape, F32),
        grid=(bsz, s // tm),
        in_specs=[
            pl.BlockSpec((None, tm, d), lambda b, i: (b, i, 0)),
            _mod_spec(l, d),
            _stacked(wg.shape[2:], l, which),
            _stacked(wu.shape[2:], l, which),
            _stacked(wd.shape[2:], l, which),
            _stacked((1, d), l, sub),
            _stacked((1, d), l, sub),
        ],
        out_specs=pl.BlockSpec((None, tm, d), lambda b, i: (b, i, 0)),
        scratch_shapes=[pltpu.VMEM((tm, d), F32)],
        compiler_params=pltpu.CompilerParams(vmem_limit_bytes=VMEM_LIMIT),
        name=f"ffn{sub}",
    )(x, mod_all, wg, wu, wd, ln_g, ln_b)


def _inproj_kernel(x_ref, mod_ref, w_ref, pw_ref, ps_ref, *rest):
    qkv_refs = rest[:3 * len(DILATIONS)]
    us_ref, yp_ref, zs_ref, z4_ref = rest[3 * len(DILATIONS):3 * len(DILATIONS) + 4]
    e_refs = rest[3 * len(DILATIONS) + 4:]
    i = pl.program_id(1)
    tm = x_ref.shape[0]
    x = x_ref[...]
    shift = mod_ref[3:4, :]
    scale = mod_ref[4:5, :]
    h = (_layernorm(x) * (1.0 + scale) + shift).astype(BF16)
    z = jnp.dot(h, w_ref[...], preferred_element_type=F32)
    up = z[:, 3 * D_ATT + D_SSM:]

    slabs_per = D_ATT // LANES
    n_qkv = 3 * slabs_per
    n_slabs = (3 * D_ATT + D_SSM) // LANES
    for c in range(n_slabs):
        col = z[:, c * LANES:(c + 1) * LANES]
        zs_ref[c] = col * (HEAD_DIM ** -0.5 * LOG2E) if c < slabs_per else col
    assert DILATIONS == (1, 4, 16)
    q4rows, q16rows = tm // 4, tm // 16
    for c in range(n_qkv):
        which, lanes = c // slabs_per, slice((c % slabs_per) * LANES, (c % slabs_per + 1) * LANES)
        qkv_refs[which][0, :, lanes] = zs_ref[c].astype(BF16)
        for r4 in range(4):
            blk = zs_ref[c, pl.ds(r4, q4rows, stride=4), :]
            z4_ref[c, r4 * q4rows:(r4 + 1) * q4rows, :] = blk
            qkv_refs[3 + which][r4, :, lanes] = blk.astype(BF16)
        for r4 in range(4):
            for c4 in range(4):
                blk = z4_ref[c, pl.ds(r4 * q4rows + c4, q16rows, stride=4), :]
                qkv_refs[6 + which][r4 + 4 * c4, :, lanes] = blk.astype(BF16)
    for c in range(n_qkv, n_slabs):
        for j in range(SSM_CHUNK):
            lo = j * D_SSM + (c - n_qkv) * LANES
            us_ref[:, lo:lo + LANES] = zs_ref[c, pl.ds(j, tm // SSM_CHUNK, stride=SSM_CHUNK), :]

    e1, e2, e4, e8 = e_refs
    hl = POOL_HALO
    assert POOL_WINDOWS == (2, 4, 8, 16) and hl == 32

    @pl.when(i == 0)
    def _():
        e1[0:hl, :] = jnp.zeros((hl, D_POOL), F32)

    @pl.when(i > 0)
    def _():
        e1[0:hl, :] = e1[tm:tm + hl, :]

    e1[hl:hl + tm, :] = up
    e2[8:, :] = e1[8:, :] + e1[7:tm + hl - 1, :]
    e4[16:, :] = e2[16:, :] + e2[14:tm + hl - 2, :]
    e8[24:, :] = e4[24:, :] + e4[20:tm + hl - 4, :]
    sums = {2: e2[hl:, :], 4: e4[hl:, :], 8: e8[hl:, :], 16: e8[hl:, :] + e8[hl - 8:tm + hl - 8, :]}
    pos = (i * tm + lax.broadcasted_iota(jnp.int32, (tm, 1), 0) + 1).astype(F32)
    group = lax.broadcasted_iota(jnp.int32, (1, D_POOL), 1) // POOL_GROUP
    mean = sums[POOL_WINDOWS[-1]] / jnp.minimum(pos, float(POOL_WINDOWS[-1]))
    for gi in range(len(POOL_WINDOWS) - 2, -1, -1):
        w = POOL_WINDOWS[gi]
        mean = jnp.where(group == gi, sums[w] / jnp.minimum(pos, float(w)), mean)
    pooled = (mean - up).astype(BF16)
    yp = jnp.dot(pooled, pw_ref[...], preferred_element_type=F32) * ps_ref[...]
    yp_ref[...] = yp.astype(BF16)


def _inproj(x, mod_all, w_in, pool_w_bd, pool_scale, l):
    bsz, s, d = x.shape
    tm = TOKEN_TILE
    tok = lambda width: pl.BlockSpec((None, tm, width), lambda b, i: (b, i, 0))
    qkv_shapes, qkv_specs = [], []
    for dil in DILATIONS:
        for _ in range(3):
            qkv_shapes.append(jax.ShapeDtypeStruct((bsz, dil, s // dil, D_ATT), BF16))
            qkv_specs.append(pl.BlockSpec((None, dil, tm // dil, D_ATT), lambda b, i: (b, 0, i, 0)))
    res = pl.pallas_call(
        _inproj_kernel,
        out_shape=tuple(qkv_shapes) + (
            jax.ShapeDtypeStruct((bsz, s // SSM_CHUNK, SSM_CHUNK * D_SSM), F32),
            jax.ShapeDtypeStruct((bsz, s, D_POOL), BF16),
        ),
        grid=(bsz, s // tm),
        in_specs=[
            tok(d),
            _mod_spec(l, d),
            _stacked(w_in.shape[1:], l),
            _stacked(pool_w_bd.shape[1:], l),
            _stacked((1, D_POOL), l),
        ],
        out_specs=tuple(qkv_specs) + (
            pl.BlockSpec((None, tm // SSM_CHUNK, SSM_CHUNK * D_SSM), lambda b, i: (b, i, 0)), tok(D_POOL)),
        scratch_shapes=[pltpu.VMEM(((3 * D_ATT + D_SSM) // LANES, tm, LANES), F32),
                        pltpu.VMEM((3 * D_ATT // LANES, tm, LANES), F32)]
        + [pltpu.VMEM((tm + POOL_HALO, D_POOL), F32) for _ in range(4)],
        compiler_params=pltpu.CompilerParams(
            dimension_semantics=("arbitrary", "arbitrary"), vmem_limit_bytes=VMEM_LIMIT),
        name="inproj",
    )(x, mod_all, w_in, pool_w_bd, pool_scale)
    qkv = [tuple(res[3 * di:3 * di + 3]) for di in range(len(DILATIONS))]
    return qkv, res[-2], res[-1]


def _t5_bucket(dist):
    max_exact = N_BUCKETS // 2
    dd = np.maximum(dist, 1).astype(np.float32)
    large = max_exact + (np.log(dd / max_exact) / math.log(MAX_DISTANCE / max_exact)
                         * (N_BUCKETS - max_exact)).astype(np.int32)
    large = np.minimum(large, N_BUCKETS - 1)
    return np.where(dist < max_exact, dist, large).astype(np.int32)


def _branch_bias(rel_bias, window, dilation, has_prev):
    qb = ATT_BLOCK
    n_keys = window // dilation
    assert n_keys == qb
    period = 3 * qb
    dist = np.arange(n_keys, -1, -1)
    row = rel_bias[_t5_bucket(dist * dilation)].T.astype(F32) * LOG2E
    row = jnp.concatenate([row, jnp.full((N_HEADS, period - n_keys - 1), NEG, F32)], axis=1)
    flat = jnp.tile(row, (1, qb))[:, :qb * (period - 1)]
    bias = flat.reshape(N_HEADS, qb, period - 1)[:, :, :2 * qb]
    if not has_prev:
        return bias[:, :, qb:]
    first = jnp.concatenate([jnp.full((N_HEADS, qb, qb), NEG, F32), bias[:, :, qb:]], axis=2)
    return jnp.stack([bias, first], 0)


def _stat_lane(h):
    return h if h % 2 == 1 else HEAD_DIM + h


def _attn_kernel(*refs, has_prev):
    qb = ATT_BLOCK
    if has_prev:
        q_ref, kc_ref, kh_ref, vc_ref, vh_ref, bias_ref, o_ref, st_ref, kbuf, vbuf = refs
        nsub = q_ref.shape[0] // qb
        kbuf[0:qb, :] = kh_ref[...]
        kbuf[qb:, :] = kc_ref[...]
        vbuf[0:qb, :] = vh_ref[...]
        vbuf[qb:, :] = vc_ref[...]
    else:
        q_ref, kc_ref, vc_ref, bias_ref, o_ref, st_ref = refs
        nsub = q_ref.shape[0]

    pair = 2 * HEAD_DIM
    assert pair == LANES
    lane = lax.broadcasted_iota(jnp.int32, (1, pair), 1)
    low = lane < HEAD_DIM
    stat_lane = lane

    def block(jj, carry):
        if has_prev:
            row0 = pl.multiple_of(jj * qb, qb)
            first = jnp.logical_and(pl.program_id(2) == 0, jj == 0).astype(jnp.int32)
            q_at = lambda cols: q_ref[pl.ds(row0, qb), cols]
            k_at = lambda cols: kbuf[pl.ds(row0, 2 * qb), cols]
            v_at = lambda cols: vbuf[pl.ds(row0, 2 * qb), cols]
            bias_at = lambda h: bias_ref[first, h]
        else:
            q_at = lambda cols: q_ref[jj, :, cols]
            k_at = lambda cols: kc_ref[jj, :, cols]
            v_at = lambda cols: vc_ref[jj, :, cols]
            bias_at = lambda h: bias_ref[h]

        def pair_cols(h):
            return slice((h // 2) * pair, (h // 2 + 1) * pair)

        def scores(h):
            q2 = q_at(pair_cols(h))
            qh = jnp.where(low, q2, jnp.zeros_like(q2)) if h % 2 == 0 else jnp.where(low, jnp.zeros_like(q2), q2)
            s = lax.dot_general(qh, k_at(pair_cols(h)), (((1,), (1,)), ((), ())), preferred_element_type=F32)
            return s + bias_at(h)

        pending = {h: scores(h) for h in range(min(ATT_LOOKAHEAD, N_HEADS))}
        outs = {}
        stats = jnp.ones((qb, LANES), F32)
        for h in range(N_HEADS):
            s = pending.pop(h)
            m = jnp.max(s, axis=-1, keepdims=True)
            p = jnp.exp2(s - m)
            if h + ATT_LOOKAHEAD < N_HEADS:
                pending[h + ATT_LOOKAHEAD] = scores(h + ATT_LOOKAHEAD)
            cols = pair_cols(h)
            v2 = v_at(cols)
            v2 = jnp.where(low, v2, jnp.ones_like(v2)) if h % 2 == 0 else jnp.where(low, jnp.ones_like(v2), v2)
            pv = jnp.dot(p.astype(BF16), v2, preferred_element_type=F32)
            outs[h] = pv
            stats = jnp.where(stat_lane == _stat_lane(h), m, stats)
            stats = jnp.where(stat_lane == _stat_lane(h) + STAT_SUM_OFFSET, pv, stats)
            if h % 2 == 1:
                o2 = jnp.where(low, outs.pop(h - 1), outs.pop(h)).astype(BF16)
                if has_prev:
                    o_ref[pl.ds(row0, qb), cols] = o2
                else:
                    o_ref[jj, :, cols] = o2
        if has_prev:
            st_ref[pl.ds(row0, qb), :] = stats
        else:
            st_ref[jj] = stats
        return carry

    lax.fori_loop(0, nsub, block, 0, unroll=True)


def _attn_branch(q, k, v, bias):
    bsz, d, ln, _ = q.shape
    qb = ATT_BLOCK
    has_prev = ln > qb
    out_shape = (jax.ShapeDtypeStruct((bsz, d, ln, D_ATT), BF16),
                 jax.ShapeDtypeStruct((bsz, d, ln, LANES), F32))
    if has_prev:
        rows = min(ATT_STEP_BLOCKS * qb, ln)
        per = rows // qb
        cur = lambda width: pl.BlockSpec((None, None, rows, width), lambda b, r, j: (b, r, j, 0))
        halo = pl.BlockSpec((None, None, qb, D_ATT), lambda b, r, j: (b, r, jnp.maximum(j * per - 1, 0), 0))
        grid = (bsz, d, ln // rows)
        args = [q, k, k, v, v, bias]
        specs = [cur(D_ATT), cur(D_ATT), halo, cur(D_ATT), halo, _resident(bias.shape)]
        scratch = [pltpu.VMEM((rows + qb, D_ATT), BF16), pltpu.VMEM((rows + qb, D_ATT), BF16)]
    else:
        per = min(ATT_STEP_BLOCKS, d)
        cur = lambda width: pl.BlockSpec((None, per, qb, width), lambda b, r: (b, r, 0, 0))
        grid = (bsz, d // per)
        args = [q, k, v, bias]
        specs = [cur(D_ATT), cur(D_ATT), cur(D_ATT), _resident(bias.shape)]
        scratch = []
    return pl.pallas_call(
        functools.partial(_attn_kernel, has_prev=has_prev),
        out_shape=out_shape,
        grid=grid,
        in_specs=specs,
        out_specs=(cur(D_ATT), cur(LANES)),
        scratch_shapes=scratch,
        compiler_params=pltpu.CompilerParams(vmem_limit_bytes=VMEM_LIMIT),
        name=f"attn_d{d}",
    )(*args)


def _dilated_attention(qkv, biases):
    return [_attn_branch(q, k, v, bias) for (q, k, v), bias in zip(qkv, biases)]


def _ssm_tables(*params):
    t, g, p, c = SSM_CHUNK, N_SSM_GROUPS, SSM_STATE, SSM_GROUP
    n = t * g * c
    w_in, w_intra, w_out, abar_t, d_row = jax.vmap(_ssm_group_blocks)(*params)
    kk = np.arange(t * c)[:, None]
    cc = np.arange(n)[None, :]
    spread_rp = jnp.asarray((cc // (g * p) == kk // p) & (cc % p == kk % p), BF16)
    spread_jc = jnp.asarray((cc // (g * c) == kk // c) & (cc % c == kk % c), BF16)
    blocks = (w_in.astype(BF16), w_intra.astype(BF16), w_out.astype(BF16))
    return blocks, (spread_rp, spread_jc), abar_t, d_row


def _ssm_group_blocks(a_re, a_im, log_dt, b_re, b_im, c_re, c_im, d_skip):
    hi = lax.Precision.HIGHEST
    t, g, p, c = SSM_CHUNK, N_SSM_GROUPS, SSM_STATE, SSM_GROUP
    dt = jnp.exp(log_dt)[:, None]
    mag = jnp.exp(a_re * dt)
    ar, ai = mag * jnp.cos(a_im * dt), mag * jnp.sin(a_im * dt)
    den = a_re * a_re + a_im * a_im
    fr = ((ar - 1.0) * a_re + ai * a_im) / den
    fi = (ai * a_re - (ar - 1.0) * a_im) / den
    bbr = fr[:, :, None] * b_re - fi[:, :, None] * b_im
    bbi = fr[:, :, None] * b_im + fi[:, :, None] * b_re
    pr, pi_ = [jnp.ones_like(ar)], [jnp.zeros_like(ar)]
    for _ in range(t):
        pr.append(pr[-1] * ar - pi_[-1] * ai)
        pi_.append(pr[-2] * ai + pi_[-1] * ar)
    n = t * g * c
    assert n == 2 * g * p and t * c == 2 * p

    bbr_t, bbi_t = jnp.transpose(bbr, (0, 2, 1)), jnp.transpose(bbi, (0, 2, 1))
    win = []
    for j in range(t):
        qr, qi = pr[t - 1 - j][:, None, :], pi_[t - 1 - j][:, None, :]
        win.append(jnp.concatenate([qr * bbr_t - qi * bbi_t, qr * bbi_t + qi * bbr_t], axis=-1))
    w_in = jnp.stack(win, 0).reshape(n, 2 * p)

    c_re_t, c_im_t = jnp.transpose(c_re, (0, 2, 1)), jnp.transpose(c_im, (0, 2, 1))
    wr_cols, wi_cols, hs = [], [], []
    for j in range(t + 1):
        if j >= 1:
            wr_cols.append(c_re_t * pr[j][:, :, None] - c_im_t * pi_[j][:, :, None])
            wi_cols.append(c_re_t * pi_[j][:, :, None] + c_im_t * pr[j][:, :, None])
        if j < t:
            wr = c_re * pr[j][:, None, :] - c_im * pi_[j][:, None, :]
            wi = c_re * pi_[j][:, None, :] + c_im * pr[j][:, None, :]
            hs.append(jnp.einsum('gcp,gpd->gdc', wr, bbr, precision=hi)
                      - jnp.einsum('gcp,gpd->gdc', wi, bbi, precision=hi))
    w_out = jnp.stack([jnp.concatenate(wr_cols, axis=-1), -jnp.concatenate(wi_cols, axis=-1)],
                      0).reshape(n, t * c)

    hcat = jnp.concatenate(hs, axis=-1)
    rows = [hcat if jp == 0 else
            jnp.concatenate([jnp.zeros((g, c, jp * c), F32), hcat[:, :, :(t - jp) * c]], axis=-1)
            for jp in range(t)]
    w_intra = jnp.stack(rows, 0).reshape(n, t * c)

    abar_t = jnp.concatenate([pr[t].reshape(1, g * p), pi_[t].reshape(1, g * p)], axis=1)
    d_row = jnp.tile(d_skip.reshape(1, D_SSM), (1, t))
    return w_in, w_intra, w_out, abar_t, d_row


def _ssm_kernel(u_ref, win_b, wintra_b, wout_b, srp_ref, sjc_ref, abar_ref, d_ref, y_ref,
                z_ref, xp_ref, win_ref, wintra_ref, wout_ref):
    n = u_ref.shape[0]
    half = N_SSM_GROUPS * SSM_STATE
    c, p, g = SSM_GROUP, SSM_STATE, N_SSM_GROUPS

    @pl.when(pl.program_id(0) == 0)
    def _():
        size = win_ref.shape[0]
        rows_per = SSM_EXPAND_ROWS
        col = lax.broadcasted_iota(jnp.int32, (rows_per, size), 1)
        row = lax.broadcasted_iota(jnp.int32, (rows_per, size), 0)
        for blk, dst, row_div, col_div, spread in ((win_b, win_ref, c, p, srp_ref),
                                                   (wintra_b, wintra_ref, c, c, sjc_ref),
                                                   (wout_b, wout_ref, p, c, sjc_ref)):
            cg = (col // col_div) % g
            for r0 in range(0, size, rows_per):
                full = jnp.dot(blk[r0:r0 + rows_per, :], spread[...], preferred_element_type=F32)
                rg = ((row + r0) // row_div) % g
                dst[r0:r0 + rows_per, :] = jnp.where(rg == cg, full, 0.0).astype(BF16)

    u = u_ref[...]
    ub = u.astype(BF16)
    z_ref[...] = jnp.dot(ub, win_ref[...], preferred_element_type=F32)
    ar = abar_ref[:, 0:half]
    ai = abar_ref[:, half:]

    def step(k, carry):
        xr, xi = carry
        xp_ref[pl.ds(k, 1), 0:half] = xr
        xp_ref[pl.ds(k, 1), half:] = xi
        zr = z_ref[pl.ds(k, 1), 0:half]
        zi = z_ref[pl.ds(k, 1), half:]
        return ar * xr - ai * xi + zr, ar * xi + ai * xr + zi

    zero = jnp.zeros((1, half), F32)
    lax.fori_loop(0, n, step, (zero, zero))
    y = jnp.dot(ub, wintra_ref[...], preferred_element_type=F32)
    y = y + jnp.dot(xp_ref[...].astype(BF16), wout_ref[...], preferred_element_type=F32)
    y_ref[...] = y + d_ref[...] * u


def _ssm(u, tables, l):
    blocks, spreads, abar_t, d_row = tables
    bsz, n, width = u.shape
    size = blocks[0].shape[1]
    row = pl.BlockSpec((None, n, width), lambda b: (b, 0, 0))
    return pl.pallas_call(
        _ssm_kernel,
        out_shape=jax.ShapeDtypeStruct((bsz, n, width), F32),
        grid=(bsz,),
        in_specs=[row] + [_stacked(t.shape[1:], l) for t in blocks] + [_resident(sp.shape) for sp in spreads]
        + [_stacked(abar_t.shape[1:], l), _stacked(d_row.shape[1:], l)],
        out_specs=row,
        scratch_shapes=[pltpu.VMEM((n, size), F32), pltpu.VMEM((n, size), F32)]
        + [pltpu.VMEM((size, size), BF16)] * 3,
        compiler_params=pltpu.CompilerParams(dimension_semantics=("arbitrary",), vmem_limit_bytes=VMEM_LIMIT),
        name="ssm",
    )(u, *blocks, *spreads, abar_t, d_row)


def _outproj_kernel(x_ref, mod_ref, *rest):
    nd = len(DILATIONS)
    o_refs, l_refs = rest[0:nd], rest[nd:2 * nd]
    (ys_ref, yp_ref, gw_ref, gb_ref, wa_ref, ws_ref, wp_ref, lg_ref, lb_ref, o_ref) = rest[2 * nd:2 * nd + 10]
    on4, on16, ot4, ln4, ln16, lt4, ysn_ref, ya_ref = rest[2 * nd + 10:]
    tm = x_ref.shape[0]
    slabs = D_ATT // LANES
    assert DILATIONS == (1, 4, 16)
    r4rows, r16rows = tm // 4, tm // 16

    for j in range(SSM_CHUNK):
        for c in range(D_SSM // LANES):
            lo = j * D_SSM + c * LANES
            ysn_ref[c, pl.ds(j, tm // SSM_CHUNK, stride=SSM_CHUNK), :] = ys_ref[:, lo:lo + LANES]

    ys = jnp.concatenate([ysn_ref[c] for c in range(D_SSM // LANES)], axis=1)
    cdf = 0.5 * (1.0 + jnp.tanh(math.sqrt(2.0 / math.pi) * (ys + 0.044715 * (ys * ys * ys))))
    t = jnp.dot((ys * cdf).astype(BF16), gw_ref[...], preferred_element_type=F32) + gb_ref[...]
    yg = (ys * _sigmoid(t)).astype(BF16)
    y_rest = (jnp.dot(yg, ws_ref[...], preferred_element_type=F32)
              + jnp.dot(yp_ref[...], wp_ref[...], preferred_element_type=F32))

    for r4 in range(4):
        ln4[pl.ds(r4, r4rows, stride=4), :] = l_refs[1][r4]
        for c4 in range(4):
            lt4[pl.ds(r4 * r4rows + c4, r16rows, stride=4), :] = l_refs[2][r4 + 4 * c4]
        for c in range(slabs):
            lanes = slice(c * LANES, (c + 1) * LANES)
            on4[c, pl.ds(r4, r4rows, stride=4), :] = o_refs[1][r4, :, lanes].astype(F32)
            for c4 in range(4):
                ot4[c, pl.ds(r4 * r4rows + c4, r16rows, stride=4), :] = o_refs[2][r4 + 4 * c4, :, lanes].astype(F32)
    for r4 in range(4):
        ln16[pl.ds(r4, r4rows, stride=4), :] = lt4[r4 * r4rows:(r4 + 1) * r4rows, :]
        for c in range(slabs):
            on16[c, pl.ds(r4, r4rows, stride=4), :] = ot4[c, r4 * r4rows:(r4 + 1) * r4rows, :]

    head = lax.broadcasted_iota(jnp.int32, (LANES, D_ATT), 1) // HEAD_DIM
    stat_of_head = jnp.where(head % 2 == 1, head, HEAD_DIM + head)
    spread = (stat_of_head == lax.broadcasted_iota(jnp.int32, (LANES, D_ATT), 0)).astype(BF16)
    st_lane = lax.broadcasted_iota(jnp.int32, (1, LANES), 1)
    is_max_lane = functools.reduce(jnp.logical_or, [st_lane == _stat_lane(h) for h in range(N_HEADS)])
    for rc in range(tm // MERGE_ROWS):
        rs = slice(rc * MERGE_ROWS, (rc + 1) * MERGE_ROWS)
        st_nat = [l_refs[0][0, rs, :], ln4[rs, :], ln16[rs, :]]
        o_nat = [o_refs[0][0, rs, :].astype(F32),
                 jnp.concatenate([on4[c, rs, :] for c in range(slabs)], axis=1),
                 jnp.concatenate([on16[c, rs, :] for c in range(slabs)], axis=1)]
        mx = functools.reduce(jnp.maximum, st_nat)
        es = [jnp.exp2(st - mx) for st in st_nat]
        sums = [pltpu.roll(st, LANES - STAT_SUM_OFFSET, axis=1) for st in st_nat]
        inv = 1.0 / functools.reduce(lambda a, b: a + b, [e * l for e, l in zip(es, sums)])
        ya = None
        for e, ov in zip(es, o_nat):
            w = jnp.where(is_max_lane, e * inv, 0.0)
            wx = jnp.dot(w.astype(BF16), spread, preferred_element_type=F32)
            ya = wx * ov if ya is None else ya + wx * ov
        ya_ref[rs, :] = ya.astype(BF16)

    x = x_ref[...]
    gate = mod_ref[5:6, :]
    y = jnp.dot(ya_ref[...], wa_ref[...], preferred_element_type=F32) + y_rest
    r = ALPHA * x + gate * y
    o_ref[...] = _layernorm(r) * lg_ref[...] + lb_ref[...]


def _outproj(x, mod_all, att, y_ssm, y_pool, glu_w, glu_b, w_out, ln_g, ln_b, l):
    bsz, s, d = x.shape
    tm = TOKEN_TILE
    tok = lambda width: pl.BlockSpec((None, tm, width), lambda b, i: (b, i, 0))
    res = lambda dil, width: pl.BlockSpec((None, dil, tm // dil, width), lambda b, i: (b, 0, i, 0))
    rows_of = lambda r0, n: pl.BlockSpec((None, n, d), lambda *_: (l, r0 // n, 0), pipeline_mode=pl.Buffered(1))
    assert D_ATT % D_SSM == 0 and D_SSM == D_POOL
    slab = lambda n: pltpu.VMEM((n, tm, LANES), F32)
    scratch = [slab(D_ATT // LANES)] * 3 + [pltpu.VMEM((tm, LANES), F32)] * 3
    scratch += [slab(D_SSM // LANES), pltpu.VMEM((tm, D_ATT), BF16)]
    return pl.pallas_call(
        _outproj_kernel,
        out_shape=jax.ShapeDtypeStruct(x.shape, F32),
        grid=(bsz, s // tm),
        in_specs=[tok(d), _mod_spec(l, d)]
        + [res(dil, D_ATT) for dil in DILATIONS]
        + [res(dil, LANES) for dil in DILATIONS]
        + [pl.BlockSpec((None, tm // SSM_CHUNK, SSM_CHUNK * D_SSM), lambda b, i: (b, i, 0)), tok(D_POOL),
           _stacked(glu_w.shape[1:], l), _stacked((1, D_SSM), l),
           rows_of(0, D_ATT), rows_of(D_ATT, D_SSM), rows_of(D_ATT + D_SSM, D_POOL),
           _stacked((1, d), l, 1), _stacked((1, d), l, 1)],
        out_specs=tok(d),
        scratch_shapes=scratch,
        compiler_params=pltpu.CompilerParams(vmem_limit_bytes=VMEM_LIMIT),
        name="outproj",
    )(x, mod_all, *[o for o, _ in att], *[st for _, st in att], y_ssm, y_pool,
      glu_w, glu_b, w_out, w_out, w_out, ln_g, ln_b)


def _block_diag(w):
    g, n, _ = w.shape
    return jnp.einsum('gab,gh->gahb', w, jnp.eye(g, dtype=w.dtype)).reshape(g * n, g * n)


def kernel(x, c, rel_bias, ada_w, ada_b, ln_g, ln_b, ffn_w_gate, ffn_w_up, ffn_w_down, w_in, w_out,
           ssm_a_re, ssm_a_im, ssm_log_dt, ssm_b_re, ssm_b_im, ssm_c_re, ssm_c_im, ssm_d, glu_w, glu_b,
           pool_w, pool_scale):
    bsz = x.shape[0]
    mod_all = _adaln(c, ada_w, ada_b).reshape(DEPTH, bsz, 9, D_MODEL)
    wg, wu, wd = ffn_w_gate, ffn_w_up, ffn_w_down
    w_in_b, w_out_b, glu_w_b = w_in.astype(BF16), w_out.astype(BF16), glu_w.astype(BF16)
    ln_g4, ln_b4 = ln_g.reshape(DEPTH, 3, 1, D_MODEL), ln_b.reshape(DEPTH, 3, 1, D_MODEL)
    glu_b3 = glu_b.reshape(DEPTH, 1, D_SSM)
    pool_scale3 = pool_scale.reshape(DEPTH, 1, D_POOL)
    pool_w_bd = jax.vmap(_block_diag)(pool_w).astype(BF16)
    biases = [_branch_bias(rel_bias, window, dilation, x.shape[1] // dilation > ATT_BLOCK)
              for window, dilation in DILATED_PATTERNS]
    tables = _ssm_tables(ssm_a_re, ssm_a_im, ssm_log_dt, ssm_b_re, ssm_b_im, ssm_c_re, ssm_c_im, ssm_d)
    for l in range(DEPTH):
        x = _ffn(x, mod_all, wg, wu, wd, ln_g4, ln_b4, l, 0)
        qkv, u_ssm, y_pool = _inproj(x, mod_all, w_in_b, pool_w_bd, pool_scale3, l)
        att = _dilated_attention(qkv, biases)
        y_ssm = _ssm(u_ssm, tables, l)
        x = _outproj(x, mod_all, att, y_ssm, y_pool, glu_w_b, glu_b3, w_out_b, ln_g4, ln_b4, l)
        x = _ffn(x, mod_all, wg, wu, wd, ln_g4, ln_b4, l, 1)
    return x
```

```python
import functools
import math

import jax
import jax.numpy as jnp
import numpy as np
from jax import lax
from jax.experimental import pallas as pl
from jax.experimental.pallas import tpu as pltpu

F32 = jnp.float32
BF16 = jnp.bfloat16

D_MODEL = 1024
DEPTH = 2
HEAD_DIM = 64
N_HEADS = 8
D_ATT = N_HEADS * HEAD_DIM
DILATED_PATTERNS = ((128, 1), (512, 4), (2048, 16))
DILATIONS = tuple(d for _, d in DILATED_PATTERNS)
ATT_BLOCK = 128
SSM_GROUP = 16
D_SSM = 256
N_SSM_GROUPS = D_SSM // SSM_GROUP
SSM_STATE = 64
POOL_WINDOWS = (2, 4, 8, 16)
D_POOL = 256
POOL_GROUP = D_POOL // len(POOL_WINDOWS)
D_IN = 3 * D_ATT + D_SSM + D_POOL
D_FF = 2816
N_BUCKETS = 32
MAX_DISTANCE = 2048
ALPHA = (2 * DEPTH) ** 0.25
FFN_RES = 0.5
LN_EPS = 1e-5
NEG = -1e30

LANES = 128
TOKEN_TILE = 512
FF_CHUNK = 512
FFN_TOKEN_TILE = 512
ADA_COL_TILE = 1152
SSM_CHUNK = 8
SSM_EXPAND_ROWS = 256
POOL_HALO = 2 * max(POOL_WINDOWS)
ATT_LOOKAHEAD = 8
ATT_STEP_BLOCKS = 16
MERGE_ROWS = 128
LOG2E = math.log2(math.e)
STAT_SUM_OFFSET = 16
VMEM_LIMIT = 56 * 1024 * 1024


def _sigmoid(x):
    return 1.0 / (1.0 + jnp.exp(-x))


def _layernorm(x):
    mu = jnp.mean(x, axis=-1, keepdims=True)
    xc = x - mu
    var = jnp.mean(xc * xc, axis=-1, keepdims=True)
    return xc * lax.rsqrt(var + LN_EPS)


def _resident(shape):
    zeros = (0,) * len(shape)
    return pl.BlockSpec(shape, lambda *_: zeros, pipeline_mode=pl.Buffered(1))


def _stacked(tail, *lead):
    idx = tuple(lead) + (0,) * len(tail)
    return pl.BlockSpec((None,) * len(lead) + tuple(tail), lambda *_: idx, pipeline_mode=pl.Buffered(1))


def _adaln_kernel(c_ref, w_ref, b_ref, o_ref):
    c = c_ref[...]
    cond = (c * _sigmoid(c)).astype(BF16)
    o_ref[...] = jnp.dot(cond, w_ref[...].astype(BF16), preferred_element_type=F32) + b_ref[...]


def _adaln(c, ada_w, ada_b):
    nl, d, n = ada_w.shape
    bsz = c.shape[0]
    return pl.pallas_call(
        _adaln_kernel,
        out_shape=jax.ShapeDtypeStruct((nl, bsz, n), F32),
        grid=(nl, n // ADA_COL_TILE),
        in_specs=[
            pl.BlockSpec((bsz, d), lambda l, j: (0, 0)),
            pl.BlockSpec((None, d, ADA_COL_TILE), lambda l, j: (l, 0, j)),
            pl.BlockSpec((None, 1, ADA_COL_TILE), lambda l, j: (l, 0, j)),
        ],
        out_specs=pl.BlockSpec((None, bsz, ADA_COL_TILE), lambda l, j: (l, 0, j)),
        compiler_params=pltpu.CompilerParams(vmem_limit_bytes=VMEM_LIMIT),
        name="adaln",
    )(c, ada_w, ada_b.reshape(nl, 1, n))


def _ff_chunks():
    chunks, c0 = [], 0
    while c0 < D_FF:
        cw = min(FF_CHUNK, D_FF - c0)
        chunks.append((c0, cw))
        c0 += cw
    return tuple(chunks)


def _ffn_kernel(x_ref, mod_ref, wg_hbm, wu_hbm, wd_hbm, lg_ref, lb_ref, o_ref, acc_ref,
                wg_ref, wu_ref, wd_ref, sems, *, sub, l, which):
    chunks = _ff_chunks()

    def chunk_copies(idx):
        c0, cw = chunks[idx]
        return (pltpu.make_async_copy(wg_hbm.at[l, which, :, pl.ds(c0, cw)], wg_ref.at[:, pl.ds(c0, cw)], sems.at[0, idx]),
                pltpu.make_async_copy(wu_hbm.at[l, which, :, pl.ds(c0, cw)], wu_ref.at[:, pl.ds(c0, cw)], sems.at[1, idx]),
                pltpu.make_async_copy(wd_hbm.at[l, which, pl.ds(c0, cw), :], wd_ref.at[pl.ds(c0, cw), :], sems.at[2, idx]))

    def body(wait_weights):
        x = x_ref[...]
        shift = mod_ref[3 * sub + 0:3 * sub + 1, :]
        scale = mod_ref[3 * sub + 1:3 * sub + 2, :]
        gate = mod_ref[3 * sub + 2:3 * sub + 3, :]
        h = (_layernorm(x) * (1.0 + scale) + shift).astype(BF16)
        for idx, (c0, cw) in enumerate(chunks):
            if wait_weights:
                for cp in chunk_copies(idx):
                    cp.wait()
            g = jnp.dot(h, wg_ref[:, c0:c0 + cw].astype(BF16), preferred_element_type=F32)
            u = jnp.dot(h, wu_ref[:, c0:c0 + cw].astype(BF16), preferred_element_type=F32)
            a = (g * _sigmoid(g) * u).astype(BF16)
            d = jnp.dot(a, wd_ref[c0:c0 + cw, :].astype(BF16), preferred_element_type=F32)
            if idx == 0:
                acc_ref[...] = d
            else:
                acc_ref[...] += d
        y = ALPHA * x + (FFN_RES * gate) * acc_ref[...]
        o_ref[...] = _layernorm(y) * lg_ref[...] + lb_ref[...]

    first = jnp.logical_and(pl.program_id(0) == 0, pl.program_id(1) == 0)

    @pl.when(first)
    def _():
        for idx in range(len(chunks)):
            for cp in chunk_copies(idx):
                cp.start()
        body(True)

    @pl.when(jnp.logical_not(first))
    def _():
        body(False)


def _mod_spec(l, d):
    return pl.BlockSpec((None, None, 9, d), lambda b, i: (l, b, 0, 0))


def _ffn(x, mod_all, wg, wu, wd, ln_g, ln_b, l, which):
    bsz, s, d = x.shape
    tm = FFN_TOKEN_TILE
    sub = 2 * which
    hbm = pl.BlockSpec(memory_space=pl.ANY)
    return pl.pallas_call(
        functools.partial(_ffn_kernel, sub=sub, l=l, which=which),
        out_shape=jax.ShapeDtypeStruct(x.shape, F32),
        grid=(bsz, s // tm),
        in_specs=[
            pl.BlockSpec((None, tm, d), lambda b, i: (b, i, 0)),
            _mod_spec(l, d),
            hbm, hbm, hbm,
            _stacked((1, d), l, sub),
            _stacked((1, d), l, sub),
        ],
        out_specs=pl.BlockSpec((None, tm, d), lambda b, i: (b, i, 0)),
        scratch_shapes=[pltpu.VMEM((tm, d), F32), pltpu.VMEM(wg.shape[2:], F32), pltpu.VMEM(wu.shape[2:], F32),
                        pltpu.VMEM(wd.shape[2:], F32), pltpu.SemaphoreType.DMA((3, len(_ff_chunks())))],
        compiler_params=pltpu.CompilerParams(
            dimension_semantics=("arbitrary", "arbitrary"), vmem_limit_bytes=VMEM_LIMIT),
        name=f"ffn{sub}",
    )(x, mod_all, wg, wu, wd, ln_g, ln_b)


def _inproj_kernel(x_ref, mod_ref, w_ref, pw_ref, ps_ref, *rest):
    qkv_refs = rest[:3 * len(DILATIONS)]
    us_ref, yp_ref, zs_ref, z4_ref = rest[3 * len(DILATIONS):3 * len(DILATIONS) + 4]
    e_refs = rest[3 * len(DILATIONS) + 4:]
    i = pl.program_id(1)
    tm = x_ref.shape[0]
    x = x_ref[...]
    shift = mod_ref[3:4, :]
    scale = mod_ref[4:5, :]
    h = (_layernorm(x) * (1.0 + scale) + shift).astype(BF16)
    z = jnp.dot(h, w_ref[...], preferred_element_type=F32)
    up = z[:, 3 * D_ATT + D_SSM:]

    slabs_per = D_ATT // LANES
    n_qkv = 3 * slabs_per
    n_slabs = (3 * D_ATT + D_SSM) // LANES
    for c in range(n_slabs):
        col = z[:, c * LANES:(c + 1) * LANES]
        zs_ref[c] = col * (HEAD_DIM ** -0.5 * LOG2E) if c < slabs_per else col
    assert DILATIONS == (1, 4, 16)
    q4rows, q16rows = tm // 4, tm // 16
    for c in range(n_qkv):
        which, lanes = c // slabs_per, slice((c % slabs_per) * LANES, (c % slabs_per + 1) * LANES)
        qkv_refs[which][0, :, lanes] = zs_ref[c].astype(BF16)
        for r4 in range(4):
            blk = zs_ref[c, pl.ds(r4, q4rows, stride=4), :]
            z4_ref[c, r4 * q4rows:(r4 + 1) * q4rows, :] = blk
            qkv_refs[3 + which][r4, :, lanes] = blk.astype(BF16)
        for r4 in range(4):
            for c4 in range(4):
                blk = z4_ref[c, pl.ds(r4 * q4rows + c4, q16rows, stride=4), :]
                qkv_refs[6 + which][r4 + 4 * c4, :, lanes] = blk.astype(BF16)
    for c in range(n_qkv, n_slabs):
        for j in range(SSM_CHUNK):
            lo = j * D_SSM + (c - n_qkv) * LANES
            us_ref[:, lo:lo + LANES] = zs_ref[c, pl.ds(j, tm // SSM_CHUNK, stride=SSM_CHUNK), :]

    e1, e2, e4, e8 = e_refs
    hl = POOL_HALO
    assert POOL_WINDOWS == (2, 4, 8, 16) and hl == 32

    @pl.when(i == 0)
    def _():
        e1[0:hl, :] = jnp.zeros((hl, D_POOL), F32)

    @pl.when(i > 0)
    def _():
        e1[0:hl, :] = e1[tm:tm + hl, :]

    e1[hl:hl + tm, :] = up
    e2[8:, :] = e1[8:, :] + e1[7:tm + hl - 1, :]
    e4[16:, :] = e2[16:, :] + e2[14:tm + hl - 2, :]
    e8[24:, :] = e4[24:, :] + e4[20:tm + hl - 4, :]
    sums = {2: e2[hl:, :], 4: e4[hl:, :], 8: e8[hl:, :], 16: e8[hl:, :] + e8[hl - 8:tm + hl - 8, :]}
    pos = (i * tm + lax.broadcasted_iota(jnp.int32, (tm, 1), 0) + 1).astype(F32)
    group = lax.broadcasted_iota(jnp.int32, (1, D_POOL), 1) // POOL_GROUP
    mean = sums[POOL_WINDOWS[-1]] / jnp.minimum(pos, float(POOL_WINDOWS[-1]))
    for gi in range(len(POOL_WINDOWS) - 2, -1, -1):
        w = POOL_WINDOWS[gi]
        mean = jnp.where(group == gi, sums[w] / jnp.minimum(pos, float(w)), mean)
    pooled = (mean - up).astype(BF16)
    yp = jnp.dot(pooled, pw_ref[...], preferred_element_type=F32) * ps_ref[...]
    yp_ref[...] = yp.astype(BF16)


def _inproj(x, mod_all, w_in, pool_w_bd, pool_scale, l):
    bsz, s, d = x.shape
    tm = TOKEN_TILE
    tok = lambda width: pl.BlockSpec((None, tm, width), lambda b, i: (b, i, 0))
    qkv_shapes, qkv_specs = [], []
    for dil in DILATIONS:
        for _ in range(3):
            qkv_shapes.append(jax.ShapeDtypeStruct((bsz, dil, s // dil, D_ATT), BF16))
            qkv_specs.append(pl.BlockSpec((None, dil, tm // dil, D_ATT), lambda b, i: (b, 0, i, 0)))
    res = pl.pallas_call(
        _inproj_kernel,
        out_shape=tuple(qkv_shapes) + (
            jax.ShapeDtypeStruct((bsz, s // SSM_CHUNK, SSM_CHUNK * D_SSM), F32),
            jax.ShapeDtypeStruct((bsz, s, D_POOL), BF16),
        ),
        grid=(bsz, s // tm),
        in_specs=[
            tok(d),
            _mod_spec(l, d),
            _stacked(w_in.shape[1:], l),
            _stacked(pool_w_bd.shape[1:], l),
            _stacked((1, D_POOL), l),
        ],
        out_specs=tuple(qkv_specs) + (
            pl.BlockSpec((None, tm // SSM_CHUNK, SSM_CHUNK * D_SSM), lambda b, i: (b, i, 0)), tok(D_POOL)),
        scratch_shapes=[pltpu.VMEM(((3 * D_ATT + D_SSM) // LANES, tm, LANES), F32),
                        pltpu.VMEM((3 * D_ATT // LANES, tm, LANES), F32)]
        + [pltpu.VMEM((tm + POOL_HALO, D_POOL), F32) for _ in range(4)],
        compiler_params=pltpu.CompilerParams(
            dimension_semantics=("arbitrary", "arbitrary"), vmem_limit_bytes=VMEM_LIMIT),
        name="inproj",
    )(x, mod_all, w_in, pool_w_bd, pool_scale)
    qkv = [tuple(res[3 * di:3 * di + 3]) for di in range(len(DILATIONS))]
    return qkv, res[-2], res[-1]


def _t5_bucket(dist):
    max_exact = N_BUCKETS // 2
    dd = np.maximum(dist, 1).astype(np.float32)
    large = max_exact + (np.log(dd / max_exact) / math.log(MAX_DISTANCE / max_exact)
                         * (N_BUCKETS - max_exact)).astype(np.int32)
    large = np.minimum(large, N_BUCKETS - 1)
    return np.where(dist < max_exact, dist, large).astype(np.int32)


def _branch_bias(rel_bias, window, dilation, has_prev):
    qb = ATT_BLOCK
    n_keys = window // dilation
    assert n_keys == qb
    period = 3 * qb
    dist = np.arange(n_keys, -1, -1)
    row = rel_bias[_t5_bucket(dist * dilation)].T.astype(F32) * LOG2E
    row = jnp.concatenate([row, jnp.full((N_HEADS, period - n_keys - 1), NEG, F32)], axis=1)
    flat = jnp.tile(row, (1, qb))[:, :qb * (period - 1)]
    bias = flat.reshape(N_HEADS, qb, period - 1)[:, :, :2 * qb]
    if not has_prev:
        return bias[:, :, qb:]
    first = jnp.concatenate([jnp.full((N_HEADS, qb, qb), NEG, F32), bias[:, :, qb:]], axis=2)
    return jnp.stack([bias, first], 0)


def _stat_lane(h):
    return h if h % 2 == 1 else HEAD_DIM + h


def _attn_kernel(*refs, has_prev):
    qb = ATT_BLOCK
    if has_prev:
        q_ref, kc_ref, kh_ref, vc_ref, vh_ref, bias_ref, o_ref, st_ref, kbuf, vbuf = refs
        nsub = q_ref.shape[0] // qb
        kbuf[0:qb, :] = kh_ref[...]
        kbuf[qb:, :] = kc_ref[...]
        vbuf[0:qb, :] = vh_ref[...]
        vbuf[qb:, :] = vc_ref[...]
    else:
        q_ref, kc_ref, vc_ref, bias_ref, o_ref, st_ref = refs
        nsub = q_ref.shape[0]

    pair = 2 * HEAD_DIM
    assert pair == LANES
    lane = lax.broadcasted_iota(jnp.int32, (1, pair), 1)
    low = lane < HEAD_DIM
    stat_lane = lane

    def block(jj, carry):
        if has_prev:
            row0 = pl.multiple_of(jj * qb, qb)
            first = jnp.logical_and(pl.program_id(2) == 0, jj == 0).astype(jnp.int32)
            q_at = lambda cols: q_ref[pl.ds(row0, qb), cols]
            k_at = lambda cols: kbuf[pl.ds(row0, 2 * qb), cols]
            v_at = lambda cols: vbuf[pl.ds(row0, 2 * qb), cols]
            bias_at = lambda h: bias_ref[first, h]
        else:
            q_at = lambda cols: q_ref[jj, :, cols]
            k_at = lambda cols: kc_ref[jj, :, cols]
            v_at = lambda cols: vc_ref[jj, :, cols]
            bias_at = lambda h: bias_ref[h]

        def pair_cols(h):
            return slice((h // 2) * pair, (h // 2 + 1) * pair)

        def scores(h):
            q2 = q_at(pair_cols(h))
            qh = jnp.where(low, q2, jnp.zeros_like(q2)) if h % 2 == 0 else jnp.where(low, jnp.zeros_like(q2), q2)
            s = lax.dot_general(qh, k_at(pair_cols(h)), (((1,), (1,)), ((), ())), preferred_element_type=F32)
            return s + bias_at(h)

        pending = {h: scores(h) for h in range(min(ATT_LOOKAHEAD, N_HEADS))}
        outs = {}
        stats = jnp.ones((qb, LANES), F32)
        for h in range(N_HEADS):
            s = pending.pop(h)
            m = jnp.max(s, axis=-1, keepdims=True)
            p = jnp.exp2(s - m)
            if h + ATT_LOOKAHEAD < N_HEADS:
                pending[h + ATT_LOOKAHEAD] = scores(h + ATT_LOOKAHEAD)
            cols = pair_cols(h)
            v2 = v_at(cols)
            v2 = jnp.where(low, v2, jnp.ones_like(v2)) if h % 2 == 0 else jnp.where(low, jnp.ones_like(v2), v2)
            pv = jnp.dot(p.astype(BF16), v2, preferred_element_type=F32)
            outs[h] = pv
            stats = jnp.where(stat_lane == _stat_lane(h), m, stats)
            stats = jnp.where(stat_lane == _stat_lane(h) + STAT_SUM_OFFSET, pv, stats)
            if h % 2 == 1:
                o2 = jnp.where(low, outs.pop(h - 1), outs.pop(h)).astype(BF16)
                if has_prev:
                    o_ref[pl.ds(row0, qb), cols] = o2
                else:
                    o_ref[jj, :, cols] = o2
        if has_prev:
            st_ref[pl.ds(row0, qb), :] = stats
        else:
            st_ref[jj] = stats
        return carry

    lax.fori_loop(0, nsub, block, 0, unroll=True)


def _attn_branch(q, k, v, bias):
    bsz, d, ln, _ = q.shape
    qb = ATT_BLOCK
    has_prev = ln > qb
    out_shape = (jax.ShapeDtypeStruct((bsz, d, ln, D_ATT), BF16),
                 jax.ShapeDtypeStruct((bsz, d, ln, LANES), F32))
    if has_prev:
        rows = min(ATT_STEP_BLOCKS * qb, ln)
        per = rows // qb
        cur = lambda width: pl.BlockSpec((None, None, rows, width), lambda b, r, j: (b, r, j, 0))
        halo = pl.BlockSpec((None, None, qb, D_ATT), lambda b, r, j: (b, r, jnp.maximum(j * per - 1, 0), 0))
        grid = (bsz, d, ln // rows)
        args = [q, k, k, v, v, bias]
        specs = [cur(D_ATT), cur(D_ATT), halo, cur(D_ATT), halo, _resident(bias.shape)]
        scratch = [pltpu.VMEM((rows + qb, D_ATT), BF16), pltpu.VMEM((rows + qb, D_ATT), BF16)]
    else:
        per = min(ATT_STEP_BLOCKS, d)
        cur = lambda width: pl.BlockSpec((None, per, qb, width), lambda b, r: (b, r, 0, 0))
        grid = (bsz, d // per)
        args = [q, k, v, bias]
        specs = [cur(D_ATT), cur(D_ATT), cur(D_ATT), _resident(bias.shape)]
        scratch = []
    return pl.pallas_call(
        functools.partial(_attn_kernel, has_prev=has_prev),
        out_shape=out_shape,
        grid=grid,
        in_specs=specs,
        out_specs=(cur(D_ATT), cur(LANES)),
        scratch_shapes=scratch,
        compiler_params=pltpu.CompilerParams(vmem_limit_bytes=VMEM_LIMIT),
        name=f"attn_d{d}",
    )(*args)


def _dilated_attention(qkv, biases):
    return [_attn_branch(q, k, v, bias) for (q, k, v), bias in zip(qkv, biases)]


def _ssm_tables(*params):
    t, g, p, c = SSM_CHUNK, N_SSM_GROUPS, SSM_STATE, SSM_GROUP
    n = t * g * c
    w_in, w_intra, w_out, abar_t, d_row = jax.vmap(_ssm_group_blocks)(*params)
    kk = np.arange(t * c)[:, None]
    cc = np.arange(n)[None, :]
    spread_rp = jnp.asarray((cc // (g * p) == kk // p) & (cc % p == kk % p), BF16)
    spread_jc = jnp.asarray((cc // (g * c) == kk // c) & (cc % c == kk % c), BF16)
    blocks = (w_in.astype(BF16), w_intra.astype(BF16), w_out.astype(BF16))
    return blocks, (spread_rp, spread_jc), abar_t, d_row


def _ssm_group_blocks(a_re, a_im, log_dt, b_re, b_im, c_re, c_im, d_skip):
    hi = lax.Precision.HIGHEST
    t, g, p, c = SSM_CHUNK, N_SSM_GROUPS, SSM_STATE, SSM_GROUP
    dt = jnp.exp(log_dt)[:, None]
    mag = jnp.exp(a_re * dt)
    ar, ai = mag * jnp.cos(a_im * dt), mag * jnp.sin(a_im * dt)
    den = a_re * a_re + a_im * a_im
    fr = ((ar - 1.0) * a_re + ai * a_im) / den
    fi = (ai * a_re - (ar - 1.0) * a_im) / den
    bbr = fr[:, :, None] * b_re - fi[:, :, None] * b_im
    bbi = fr[:, :, None] * b_im + fi[:, :, None] * b_re
    pr, pi_ = [jnp.ones_like(ar)], [jnp.zeros_like(ar)]
    for _ in range(t):
        pr.append(pr[-1] * ar - pi_[-1] * ai)
        pi_.append(pr[-2] * ai + pi_[-1] * ar)
    n = t * g * c
    assert n == 2 * g * p and t * c == 2 * p

    bbr_t, bbi_t = jnp.transpose(bbr, (0, 2, 1)), jnp.transpose(bbi, (0, 2, 1))
    win = []
    for j in range(t):
        qr, qi = pr[t - 1 - j][:, None, :], pi_[t - 1 - j][:, None, :]
        win.append(jnp.concatenate([qr * bbr_t - qi * bbi_t, qr * bbi_t + qi * bbr_t], axis=-1))
    w_in = jnp.stack(win, 0).reshape(n, 2 * p)

    c_re_t, c_im_t = jnp.transpose(c_re, (0, 2, 1)), jnp.transpose(c_im, (0, 2, 1))
    wr_cols, wi_cols, hs = [], [], []
    for j in range(t + 1):
        if j >= 1:
            wr_cols.append(c_re_t * pr[j][:, :, None] - c_im_t * pi_[j][:, :, None])
            wi_cols.append(c_re_t * pi_[j][:, :, None] + c_im_t * pr[j][:, :, None])
        if j < t:
            wr = c_re * pr[j][:, None, :] - c_im * pi_[j][:, None, :]
            wi = c_re * pi_[j][:, None, :] + c_im * pr[j][:, None, :]
            hs.append(jnp.einsum('gcp,gpd->gdc', wr, bbr, precision=hi)
                      - jnp.einsum('gcp,gpd->gdc', wi, bbi, precision=hi))
    w_out = jnp.stack([jnp.concatenate(wr_cols, axis=-1), -jnp.concatenate(wi_cols, axis=-1)],
                      0).reshape(n, t * c)

    hcat = jnp.concatenate(hs, axis=-1)
    rows = [hcat if jp == 0 else
            jnp.concatenate([jnp.zeros((g, c, jp * c), F32), hcat[:, :, :(t - jp) * c]], axis=-1)
            for jp in range(t)]
    w_intra = jnp.stack(rows, 0).reshape(n, t * c)

    abar_t = jnp.concatenate([pr[t].reshape(1, g * p), pi_[t].reshape(1, g * p)], axis=1)
    d_row = jnp.tile(d_skip.reshape(1, D_SSM), (1, t))
    return w_in, w_intra, w_out, abar_t, d_row


def _ssm_kernel(u_ref, win_b, wintra_b, wout_b, srp_ref, sjc_ref, abar_ref, d_ref, y_ref,
                z_ref, xp_ref, win_ref, wintra_ref, wout_ref):
    n = u_ref.shape[0]
    half = N_SSM_GROUPS * SSM_STATE
    c, p, g = SSM_GROUP, SSM_STATE, N_SSM_GROUPS

    @pl.when(pl.program_id(0) == 0)
    def _():
        size = win_ref.shape[0]
        rows_per = SSM_EXPAND_ROWS
        col = lax.broadcasted_iota(jnp.int32, (rows_per, size), 1)
        row = lax.broadcasted_iota(jnp.int32, (rows_per, size), 0)
        for blk, dst, row_div, col_div, spread in ((win_b, win_ref, c, p, srp_ref),
                                                   (wintra_b, wintra_ref, c, c, sjc_ref),
                                                   (wout_b, wout_ref, p, c, sjc_ref)):
            cg = (col // col_div) % g
            for r0 in range(0, size, rows_per):
                full = jnp.dot(blk[r0:r0 + rows_per, :], spread[...], preferred_element_type=F32)
                rg = ((row + r0) // row_div) % g
                dst[r0:r0 + rows_per, :] = jnp.where(rg == cg, full, 0.0).astype(BF16)

    u = u_ref[...]
    ub = u.astype(BF16)
    z_ref[...] = jnp.dot(ub, win_ref[...], preferred_element_type=F32)
    ar = abar_ref[:, 0:half]
    ai = abar_ref[:, half:]

    def step(k, carry):
        xr, xi = carry
        xp_ref[pl.ds(k, 1), 0:half] = xr
        xp_ref[pl.ds(k, 1), half:] = xi
        zr = z_ref[pl.ds(k, 1), 0:half]
        zi = z_ref[pl.ds(k, 1), half:]
        return ar * xr - ai * xi + zr, ar * xi + ai * xr + zi

    zero = jnp.zeros((1, half), F32)
    lax.fori_loop(0, n, step, (zero, zero))
    y = jnp.dot(ub, wintra_ref[...], preferred_element_type=F32)
    y = y + jnp.dot(xp_ref[...].astype(BF16), wout_ref[...], preferred_element_type=F32)
    y_ref[...] = y + d_ref[...] * u


def _ssm(u, tables, l):
    blocks, spreads, abar_t, d_row = tables
    bsz, n, width = u.shape
    size = blocks[0].shape[1]
    row = pl.BlockSpec((None, n, width), lambda b: (b, 0, 0))
    return pl.pallas_call(
        _ssm_kernel,
        out_shape=jax.ShapeDtypeStruct((bsz, n, width), F32),
        grid=(bsz,),
        in_specs=[row] + [_stacked(t.shape[1:], l) for t in blocks] + [_resident(sp.shape) for sp in spreads]
        + [_stacked(abar_t.shape[1:], l), _stacked(d_row.shape[1:], l)],
        out_specs=row,
        scratch_shapes=[pltpu.VMEM((n, size), F32), pltpu.VMEM((n, size), F32)]
        + [pltpu.VMEM((size, size), BF16)] * 3,
        compiler_params=pltpu.CompilerParams(dimension_semantics=("arbitrary",), vmem_limit_bytes=VMEM_LIMIT),
        name="ssm",
    )(u, *blocks, *spreads, abar_t, d_row)


def _outproj_kernel(x_ref, mod_ref, *rest):
    nd = len(DILATIONS)
    o_refs, l_refs = rest[0:nd], rest[nd:2 * nd]
    (ys_ref, yp_ref, gw_ref, gb_ref, wa_ref, ws_ref, wp_ref, lg_ref, lb_ref, o_ref) = rest[2 * nd:2 * nd + 10]
    on4, on16, ot4, ln4, ln16, lt4, ysn_ref, ya_ref = rest[2 * nd + 10:]
    tm = x_ref.shape[0]
    slabs = D_ATT // LANES
    assert DILATIONS == (1, 4, 16)
    r4rows, r16rows = tm // 4, tm // 16

    for j in range(SSM_CHUNK):
        for c in range(D_SSM // LANES):
            lo = j * D_SSM + c * LANES
            ysn_ref[c, pl.ds(j, tm // SSM_CHUNK, stride=SSM_CHUNK), :] = ys_ref[:, lo:lo + LANES]

    ys = jnp.concatenate([ysn_ref[c] for c in range(D_SSM // LANES)], axis=1)
    cdf = 0.5 * (1.0 + jnp.tanh(math.sqrt(2.0 / math.pi) * (ys + 0.044715 * (ys * ys * ys))))
    t = jnp.dot((ys * cdf).astype(BF16), gw_ref[...], preferred_element_type=F32) + gb_ref[...]
    yg = (ys * _sigmoid(t)).astype(BF16)
    y_rest = (jnp.dot(yg, ws_ref[...], preferred_element_type=F32)
              + jnp.dot(yp_ref[...], wp_ref[...], preferred_element_type=F32))

    for r4 in range(4):
        ln4[pl.ds(r4, r4rows, stride=4), :] = l_refs[1][r4]
        for c4 in range(4):
            lt4[pl.ds(r4 * r4rows + c4, r16rows, stride=4), :] = l_refs[2][r4 + 4 * c4]
        for c in range(slabs):
            lanes = slice(c * LANES, (c + 1) * LANES)
            on4[c, pl.ds(r4, r4rows, stride=4), :] = o_refs[1][r4, :, lanes].astype(F32)
            for c4 in range(4):
                ot4[c, pl.ds(r4 * r4rows + c4, r16rows, stride=4), :] = o_refs[2][r4 + 4 * c4, :, lanes].astype(F32)
    for r4 in range(4):
        ln16[pl.ds(r4, r4rows, stride=4), :] = lt4[r4 * r4rows:(r4 + 1) * r4rows, :]
        for c in range(slabs):
            on16[c, pl.ds(r4, r4rows, stride=4), :] = ot4[c, r4 * r4rows:(r4 + 1) * r4rows, :]

    head = lax.broadcasted_iota(jnp.int32, (LANES, D_ATT), 1) // HEAD_DIM
    stat_of_head = jnp.where(head % 2 == 1, head, HEAD_DIM + head)
    spread = (stat_of_head == lax.broadcasted_iota(jnp.int32, (LANES, D_ATT), 0)).astype(BF16)
    st_lane = lax.broadcasted_iota(jnp.int32, (1, LANES), 1)
    is_max_lane = functools.reduce(jnp.logical_or, [st_lane == _stat_lane(h) for h in range(N_HEADS)])
    for rc in range(tm // MERGE_ROWS):
        rs = slice(rc * MERGE_ROWS, (rc + 1) * MERGE_ROWS)
        st_nat = [l_refs[0][0, rs, :], ln4[rs, :], ln16[rs, :]]
        o_nat = [o_refs[0][0, rs, :].astype(F32),
                 jnp.concatenate([on4[c, rs, :] for c in range(slabs)], axis=1),
                 jnp.concatenate([on16[c, rs, :] for c in range(slabs)], axis=1)]
        mx = functools.reduce(jnp.maximum, st_nat)
        es = [jnp.exp2(st - mx) for st in st_nat]
        sums = [pltpu.roll(st, LANES - STAT_SUM_OFFSET, axis=1) for st in st_nat]
        inv = 1.0 / functools.reduce(lambda a, b: a + b, [e * l for e, l in zip(es, sums)])
        ya = None
        for e, ov in zip(es, o_nat):
            w = jnp.where(is_max_lane, e * inv, 0.0)
            wx = jnp.dot(w.astype(BF16), spread, preferred_element_type=F32)
            ya = wx * ov if ya is None else ya + wx * ov
        ya_ref[rs, :] = ya.astype(BF16)

    x = x_ref[...]
    gate = mod_ref[5:6, :]
    y = jnp.dot(ya_ref[...], wa_ref[...], preferred_element_type=F32) + y_rest
    r = ALPHA * x + gate * y
    o_ref[...] = _layernorm(r) * lg_ref[...] + lb_ref[...]


def _outproj(x, mod_all, att, y_ssm, y_pool, glu_w, glu_b, w_out, ln_g, ln_b, l):
    bsz, s, d = x.shape
    tm = TOKEN_TILE
    tok = lambda width: pl.BlockSpec((None, tm, width), lambda b, i: (b, i, 0))
    res = lambda dil, width: pl.BlockSpec((None, dil, tm // dil, width), lambda b, i: (b, 0, i, 0))
    rows_of = lambda r0, n: pl.BlockSpec((None, n, d), lambda *_: (l, r0 // n, 0), pipeline_mode=pl.Buffered(1))
    assert D_ATT % D_SSM == 0 and D_SSM == D_POOL
    slab = lambda n: pltpu.VMEM((n, tm, LANES), F32)
    scratch = [slab(D_ATT // LANES)] * 3 + [pltpu.VMEM((tm, LANES), F32)] * 3
    scratch += [slab(D_SSM // LANES), pltpu.VMEM((tm, D_ATT), BF16)]
    return pl.pallas_call(
        _outproj_kernel,
        out_shape=jax.ShapeDtypeStruct(x.shape, F32),
        grid=(bsz, s // tm),
        in_specs=[tok(d), _mod_spec(l, d)]
        + [res(dil, D_ATT) for dil in DILATIONS]
        + [res(dil, LANES) for dil in DILATIONS]
        + [pl.BlockSpec((None, tm // SSM_CHUNK, SSM_CHUNK * D_SSM), lambda b, i: (b, i, 0)), tok(D_POOL),
           _stacked(glu_w.shape[1:], l), _stacked((1, D_SSM), l),
           rows_of(0, D_ATT), rows_of(D_ATT, D_SSM), rows_of(D_ATT + D_SSM, D_POOL),
           _stacked((1, d), l, 1), _stacked((1, d), l, 1)],
        out_specs=tok(d),
        scratch_shapes=scratch,
        compiler_params=pltpu.CompilerParams(vmem_limit_bytes=VMEM_LIMIT),
        name="outproj",
    )(x, mod_all, *[o for o, _ in att], *[st for _, st in att], y_ssm, y_pool,
      glu_w, glu_b, w_out, w_out, w_out, ln_g, ln_b)


def _block_diag(w):
    g, n, _ = w.shape
    return jnp.einsum('gab,gh->gahb', w, jnp.eye(g, dtype=w.dtype)).reshape(g * n, g * n)


def kernel(x, c, rel_bias, ada_w, ada_b, ln_g, ln_b, ffn_w_gate, ffn_w_up, ffn_w_down, w_in, w_out,
           ssm_a_re, ssm_a_im, ssm_log_dt, ssm_b_re, ssm_b_im, ssm_c_re, ssm_c_im, ssm_d, glu_w, glu_b,
           pool_w, pool_scale):
    bsz = x.shape[0]
    mod_all = _adaln(c, ada_w, ada_b).reshape(DEPTH, bsz, 9, D_MODEL)
    wg, wu, wd = ffn_w_gate, ffn_w_up, ffn_w_down
    w_in_b, w_out_b, glu_w_b = w_in.astype(BF16), w_out.astype(BF16), glu_w.astype(BF16)
    ln_g4, ln_b4 = ln_g.reshape(DEPTH, 3, 1, D_MODEL), ln_b.reshape(DEPTH, 3, 1, D_MODEL)
    glu_b3 = glu_b.reshape(DEPTH, 1, D_SSM)
    pool_scale3 = pool_scale.reshape(DEPTH, 1, D_POOL)
    pool_w_bd = jax.vmap(_block_diag)(pool_w).astype(BF16)
    biases = [_branch_bias(rel_bias, window, dilation, x.shape[1] // dilation > ATT_BLOCK)
              for window, dilation in DILATED_PATTERNS]
    tables = _ssm_tables(ssm_a_re, ssm_a_im, ssm_log_dt, ssm_b_re, ssm_b_im, ssm_c_re, ssm_c_im, ssm_d)
    for l in range(DEPTH):
        x = _ffn(x, mod_all, wg, wu, wd, ln_g4, ln_b4, l, 0)
        qkv, u_ssm, y_pool = _inproj(x, mod_all, w_in_b, pool_w_bd, pool_scale3, l)
        att = _dilated_attention(qkv, biases)
        y_ssm = _ssm(u_ssm, tables, l)
        x = _outproj(x, mod_all, att, y_ssm, y_pool, glu_w_b, glu_b3, w_out_b, ln_g4, ln_b4, l)
        x = _ffn(x, mod_all, wg, wu, wd, ln_g4, ln_b4, l, 1)
    return x
```

```python
import functools
import math

import jax
import jax.numpy as jnp
import numpy as np
from jax import lax
from jax.experimental import pallas as pl
from jax.experimental.pallas import tpu as pltpu

F32 = jnp.float32
BF16 = jnp.bfloat16

D_MODEL = 1024
DEPTH = 2
HEAD_DIM = 64
N_HEADS = 8
D_ATT = N_HEADS * HEAD_DIM
DILATED_PATTERNS = ((128, 1), (512, 4), (2048, 16))
DILATIONS = tuple(d for _, d in DILATED_PATTERNS)
ATT_BLOCK = 128
SSM_GROUP = 16
D_SSM = 256
N_SSM_GROUPS = D_SSM // SSM_GROUP
SSM_STATE = 64
POOL_WINDOWS = (2, 4, 8, 16)
D_POOL = 256
POOL_GROUP = D_POOL // len(POOL_WINDOWS)
D_IN = 3 * D_ATT + D_SSM + D_POOL
D_FF = 2816
N_BUCKETS = 32
MAX_DISTANCE = 2048
ALPHA = (2 * DEPTH) ** 0.25
FFN_RES = 0.5
LN_EPS = 1e-5
NEG = -1e30

LANES = 128
TOKEN_TILE = 512
OUTPROJ_TOKEN_TILE = 1024
FF_CHUNK = 512
FFN_TOKEN_TILE = 512
ADA_COL_TILE = 2304
SSM_CHUNK = 8
SSM_EXPAND_ROWS = 256
POOL_HALO = 2 * max(POOL_WINDOWS)
ATT_LOOKAHEAD = 8
ATT_STEP_BLOCKS = 16
MERGE_ROWS = 128
LOG2E = math.log2(math.e)
STAT_SUM_OFFSET = 16
VMEM_LIMIT = 56 * 1024 * 1024


def _sigmoid(x):
    return 1.0 / (1.0 + jnp.exp(-x))


def _layernorm(x):
    mu = jnp.mean(x, axis=-1, keepdims=True)
    xc = x - mu
    var = jnp.mean(xc * xc, axis=-1, keepdims=True)
    return xc * lax.rsqrt(var + LN_EPS)


def _resident(shape):
    zeros = (0,) * len(shape)
    return pl.BlockSpec(shape, lambda *_: zeros, pipeline_mode=pl.Buffered(1))


def _stacked(tail, *lead):
    idx = tuple(lead) + (0,) * len(tail)
    return pl.BlockSpec((None,) * len(lead) + tuple(tail), lambda *_: idx, pipeline_mode=pl.Buffered(1))


def _adaln_kernel(c_ref, w_ref, b_ref, o_ref):
    c = c_ref[...]
    cond = (c * _sigmoid(c)).astype(BF16)
    o_ref[...] = jnp.dot(cond, w_ref[...].astype(BF16), preferred_element_type=F32) + b_ref[...]


def _adaln(c, ada_w, ada_b):
    nl, d, n = ada_w.shape
    bsz = c.shape[0]
    return pl.pallas_call(
        _adaln_kernel,
        out_shape=jax.ShapeDtypeStruct((nl, bsz, n), F32),
        grid=(nl, n // ADA_COL_TILE),
        in_specs=[
            pl.BlockSpec((bsz, d), lambda l, j: (0, 0)),
            pl.BlockSpec((None, d, ADA_COL_TILE), lambda l, j: (l, 0, j)),
            pl.BlockSpec((None, 1, ADA_COL_TILE), lambda l, j: (l, 0, j)),
        ],
        out_specs=pl.BlockSpec((None, bsz, ADA_COL_TILE), lambda l, j: (l, 0, j)),
        compiler_params=pltpu.CompilerParams(vmem_limit_bytes=VMEM_LIMIT),
        name="adaln",
    )(c, ada_w, ada_b.reshape(nl, 1, n))


def _ff_chunks():
    chunks, c0 = [], 0
    while c0 < D_FF:
        cw = min(FF_CHUNK, D_FF - c0)
        chunks.append((c0, cw))
        c0 += cw
    return tuple(chunks)


def _ffn_kernel(x_ref, mod_ref, wg_ref, wu_ref, wd_ref, lg_ref, lb_ref, o_ref, acc_ref, *, sub):
    x = x_ref[...]
    shift = mod_ref[3 * sub + 0:3 * sub + 1, :]
    scale = mod_ref[3 * sub + 1:3 * sub + 2, :]
    gate = mod_ref[3 * sub + 2:3 * sub + 3, :]
    h = (_layernorm(x) * (1.0 + scale) + shift).astype(BF16)
    for idx, (c0, cw) in enumerate(_ff_chunks()):
        g = jnp.dot(h, wg_ref[:, c0:c0 + cw].astype(BF16), preferred_element_type=F32)
        u = jnp.dot(h, wu_ref[:, c0:c0 + cw].astype(BF16), preferred_element_type=F32)
        a = (g * _sigmoid(g) * u).astype(BF16)
        d = jnp.dot(a, wd_ref[c0:c0 + cw, :].astype(BF16), preferred_element_type=F32)
        if idx == 0:
            acc_ref[...] = d
        else:
            acc_ref[...] += d
    y = ALPHA * x + (FFN_RES * gate) * acc_ref[...]
    o_ref[...] = _layernorm(y) * lg_ref[...] + lb_ref[...]


def _mod_spec(l, d):
    return pl.BlockSpec((None, None, 9, d), lambda b, i: (l, b, 0, 0))


def _ffn(x, mod_all, wg, wu, wd, ln_g, ln_b, l, which):
    bsz, s, d = x.shape
    tm = FFN_TOKEN_TILE
    sub = 2 * which
    return pl.pallas_call(
        functools.partial(_ffn_kernel, sub=sub),
        out_shape=jax.ShapeDtypeStruct(x.shape, F32),
        grid=(bsz, s // tm),
        in_specs=[
            pl.BlockSpec((None, tm, d), lambda b, i: (b, i, 0)),
            _mod_spec(l, d),
            _stacked(wg.shape[2:], l, which),
            _stacked(wu.shape[2:], l, which),
            _stacked(wd.shape[2:], l, which),
            _stacked((1, d), l, sub),
            _stacked((1, d), l, sub),
        ],
        out_specs=pl.BlockSpec((None, tm, d), lambda b, i: (b, i, 0)),
        scratch_shapes=[pltpu.VMEM((tm, d), F32)],
        compiler_params=pltpu.CompilerParams(vmem_limit_bytes=VMEM_LIMIT),
        name=f"ffn{sub}",
    )(x, mod_all, wg, wu, wd, ln_g, ln_b)


def _inproj_kernel(x_ref, mod_ref, w_ref, pw_ref, ps_ref, *rest):
    qkv_refs = rest[:3 * len(DILATIONS)]
    us_ref, yp_ref, zs_ref, z4_ref = rest[3 * len(DILATIONS):3 * len(DILATIONS) + 4]
    e_refs = rest[3 * len(DILATIONS) + 4:]
    i = pl.program_id(1)
    tm = x_ref.shape[0]
    x = x_ref[...]
    shift = mod_ref[3:4, :]
    scale = mod_ref[4:5, :]
    h = (_layernorm(x) * (1.0 + scale) + shift).astype(BF16)
    z = jnp.dot(h, w_ref[...], preferred_element_type=F32)
    up = z[:, 3 * D_ATT + D_SSM:]

    slabs_per = D_ATT // LANES
    n_qkv = 3 * slabs_per
    n_slabs = (3 * D_ATT + D_SSM) // LANES
    for c in range(n_slabs):
        col = z[:, c * LANES:(c + 1) * LANES]
        zs_ref[c] = col * (HEAD_DIM ** -0.5 * LOG2E) if c < slabs_per else col
    assert DILATIONS == (1, 4, 16)
    q4rows, q16rows = tm // 4, tm // 16
    for c in range(n_qkv):
        which, lanes = c // slabs_per, slice((c % slabs_per) * LANES, (c % slabs_per + 1) * LANES)
        qkv_refs[which][0, :, lanes] = zs_ref[c].astype(BF16)
        for r4 in range(4):
            blk = zs_ref[c, pl.ds(r4, q4rows, stride=4), :]
            z4_ref[c, r4 * q4rows:(r4 + 1) * q4rows, :] = blk
            qkv_refs[3 + which][r4, :, lanes] = blk.astype(BF16)
        for r4 in range(4):
            for c4 in range(4):
                blk = z4_ref[c, pl.ds(r4 * q4rows + c4, q16rows, stride=4), :]
                qkv_refs[6 + which][r4 + 4 * c4, :, lanes] = blk.astype(BF16)
    for c in range(n_qkv, n_slabs):
        for j in range(SSM_CHUNK):
            lo = j * D_SSM + (c - n_qkv) * LANES
            us_ref[:, lo:lo + LANES] = zs_ref[c, pl.ds(j, tm // SSM_CHUNK, stride=SSM_CHUNK), :]

    e1, e2, e4, e8 = e_refs
    hl = POOL_HALO
    assert POOL_WINDOWS == (2, 4, 8, 16) and hl == 32

    @pl.when(i == 0)
    def _():
        e1[0:hl, :] = jnp.zeros((hl, D_POOL), F32)

    @pl.when(i > 0)
    def _():
        e1[0:hl, :] = e1[tm:tm + hl, :]

    e1[hl:hl + tm, :] = up
    e2[8:, :] = e1[8:, :] + e1[7:tm + hl - 1, :]
    e4[16:, :] = e2[16:, :] + e2[14:tm + hl - 2, :]
    e8[24:, :] = e4[24:, :] + e4[20:tm + hl - 4, :]
    sums = {2: e2[hl:, :], 4: e4[hl:, :], 8: e8[hl:, :], 16: e8[hl:, :] + e8[hl - 8:tm + hl - 8, :]}
    pos = (i * tm + lax.broadcasted_iota(jnp.int32, (tm, 1), 0) + 1).astype(F32)
    group = lax.broadcasted_iota(jnp.int32, (1, D_POOL), 1) // POOL_GROUP
    mean = sums[POOL_WINDOWS[-1]] / jnp.minimum(pos, float(POOL_WINDOWS[-1]))
    for gi in range(len(POOL_WINDOWS) - 2, -1, -1):
        w = POOL_WINDOWS[gi]
        mean = jnp.where(group == gi, sums[w] / jnp.minimum(pos, float(w)), mean)
    pooled = (mean - up).astype(BF16)
    yp = jnp.dot(pooled, pw_ref[...], preferred_element_type=F32) * ps_ref[...]
    yp_ref[...] = yp.astype(BF16)


def _inproj(x, mod_all, w_in, pool_w_bd, pool_scale, l):
    bsz, s, d = x.shape
    tm = TOKEN_TILE
    tok = lambda width: pl.BlockSpec((None, tm, width), lambda b, i: (b, i, 0))
    qkv_shapes, qkv_specs = [], []
    for dil in DILATIONS:
        for _ in range(3):
            qkv_shapes.append(jax.ShapeDtypeStruct((bsz, dil, s // dil, D_ATT), BF16))
            qkv_specs.append(pl.BlockSpec((None, dil, tm // dil, D_ATT), lambda b, i: (b, 0, i, 0)))
    res = pl.pallas_call(
        _inproj_kernel,
        out_shape=tuple(qkv_shapes) + (
            jax.ShapeDtypeStruct((bsz, s // SSM_CHUNK, SSM_CHUNK * D_SSM), F32),
            jax.ShapeDtypeStruct((bsz, s, D_POOL), BF16),
        ),
        grid=(bsz, s // tm),
        in_specs=[
            tok(d),
            _mod_spec(l, d),
            _stacked(w_in.shape[1:], l),
            _stacked(pool_w_bd.shape[1:], l),
            _stacked((1, D_POOL), l),
        ],
        out_specs=tuple(qkv_specs) + (
            pl.BlockSpec((None, tm // SSM_CHUNK, SSM_CHUNK * D_SSM), lambda b, i: (b, i, 0)), tok(D_POOL)),
        scratch_shapes=[pltpu.VMEM(((3 * D_ATT + D_SSM) // LANES, tm, LANES), F32),
                        pltpu.VMEM((3 * D_ATT // LANES, tm, LANES), F32)]
        + [pltpu.VMEM((tm + POOL_HALO, D_POOL), F32) for _ in range(4)],
        compiler_params=pltpu.CompilerParams(
            dimension_semantics=("arbitrary", "arbitrary"), vmem_limit_bytes=VMEM_LIMIT),
        name="inproj",
    )(x, mod_all, w_in, pool_w_bd, pool_scale)
    qkv = [tuple(res[3 * di:3 * di + 3]) for di in range(len(DILATIONS))]
    return qkv, res[-2], res[-1]


def _t5_bucket(dist):
    max_exact = N_BUCKETS // 2
    dd = np.maximum(dist, 1).astype(np.float32)
    large = max_exact + (np.log(dd / max_exact) / math.log(MAX_DISTANCE / max_exact)
                         * (N_BUCKETS - max_exact)).astype(np.int32)
    large = np.minimum(large, N_BUCKETS - 1)
    return np.where(dist < max_exact, dist, large).astype(np.int32)


def _branch_bias(rel_bias, window, dilation, has_prev):
    qb = ATT_BLOCK
    n_keys = window // dilation
    assert n_keys == qb
    period = 3 * qb
    dist = np.arange(n_keys, -1, -1)
    row = rel_bias[_t5_bucket(dist * dilation)].T.astype(F32) * LOG2E
    row = jnp.concatenate([row, jnp.full((N_HEADS, period - n_keys - 1), NEG, F32)], axis=1)
    flat = jnp.tile(row, (1, qb))[:, :qb * (period - 1)]
    bias = flat.reshape(N_HEADS, qb, period - 1)[:, :, :2 * qb]
    if not has_prev:
        return bias[:, :, qb:]
    first = jnp.concatenate([jnp.full((N_HEADS, qb, qb), NEG, F32), bias[:, :, qb:]], axis=2)
    return jnp.stack([bias, first], 0)


def _stat_lane(h):
    return h if h % 2 == 1 else HEAD_DIM + h


def _attn_kernel(*refs, has_prev):
    qb = ATT_BLOCK
    if has_prev:
        q_ref, kc_ref, kh_ref, vc_ref, vh_ref, bias_ref, o_ref, st_ref, kbuf, vbuf = refs
        nsub = q_ref.shape[0] // qb
        kbuf[0:qb, :] = kh_ref[...]
        kbuf[qb:, :] = kc_ref[...]
        vbuf[0:qb, :] = vh_ref[...]
        vbuf[qb:, :] = vc_ref[...]
    else:
        q_ref, kc_ref, vc_ref, bias_ref, o_ref, st_ref = refs
        nsub = q_ref.shape[0]

    pair = 2 * HEAD_DIM
    assert pair == LANES
    lane = lax.broadcasted_iota(jnp.int32, (1, pair), 1)
    low = lane < HEAD_DIM
    stat_lane = lane

    def block(jj, carry):
        if has_prev:
            row0 = pl.multiple_of(jj * qb, qb)
            first = jnp.logical_and(pl.program_id(2) == 0, jj == 0).astype(jnp.int32)
            q_at = lambda cols: q_ref[pl.ds(row0, qb), cols]
            k_at = lambda cols: kbuf[pl.ds(row0, 2 * qb), cols]
            v_at = lambda cols: vbuf[pl.ds(row0, 2 * qb), cols]
            bias_at = lambda h: bias_ref[first, h]
        else:
            q_at = lambda cols: q_ref[jj, :, cols]
            k_at = lambda cols: kc_ref[jj, :, cols]
            v_at = lambda cols: vc_ref[jj, :, cols]
            bias_at = lambda h: bias_ref[h]

        def pair_cols(h):
            return slice((h // 2) * pair, (h // 2 + 1) * pair)

        def scores(h):
            q2 = q_at(pair_cols(h))
            qh = jnp.where(low, q2, jnp.zeros_like(q2)) if h % 2 == 0 else jnp.where(low, jnp.zeros_like(q2), q2)
            s = lax.dot_general(qh, k_at(pair_cols(h)), (((1,), (1,)), ((), ())), preferred_element_type=F32)
            return s + bias_at(h)

        pending = {h: scores(h) for h in range(min(ATT_LOOKAHEAD, N_HEADS))}
        outs = {}
        stats = jnp.ones((qb, LANES), F32)
        for h in range(N_HEADS):
            s = pending.pop(h)
            m = jnp.max(s, axis=-1, keepdims=True)
            p = jnp.exp2(s - m)
            if h + ATT_LOOKAHEAD < N_HEADS:
                pending[h + ATT_LOOKAHEAD] = scores(h + ATT_LOOKAHEAD)
            cols = pair_cols(h)
            v2 = v_at(cols)
            v2 = jnp.where(low, v2, jnp.ones_like(v2)) if h % 2 == 0 else jnp.where(low, jnp.ones_like(v2), v2)
            pv = jnp.dot(p.astype(BF16), v2, preferred_element_type=F32)
            outs[h] = pv
            stats = jnp.where(stat_lane == _stat_lane(h), m, stats)
            stats = jnp.where(stat_lane == _stat_lane(h) + STAT_SUM_OFFSET, pv, stats)
            if h % 2 == 1:
                o2 = jnp.where(low, outs.pop(h - 1), outs.pop(h)).astype(BF16)
                if has_prev:
                    o_ref[pl.ds(row0, qb), cols] = o2
                else:
                    o_ref[jj, :, cols] = o2
        if has_prev:
            st_ref[pl.ds(row0, qb), :] = stats
        else:
            st_ref[jj] = stats
        return carry

    lax.fori_loop(0, nsub, block, 0, unroll=True)


def _attn_branch(q, k, v, bias):
    bsz, d, ln, _ = q.shape
    qb = ATT_BLOCK
    has_prev = ln > qb
    out_shape = (jax.ShapeDtypeStruct((bsz, d, ln, D_ATT), BF16),
                 jax.ShapeDtypeStruct((bsz, d, ln, LANES), F32))
    if has_prev:
        rows = min(ATT_STEP_BLOCKS * qb, ln)
        per = rows // qb
        cur = lambda width: pl.BlockSpec((None, None, rows, width), lambda b, r, j: (b, r, j, 0))
        halo = pl.BlockSpec((None, None, qb, D_ATT), lambda b, r, j: (b, r, jnp.maximum(j * per - 1, 0), 0))
        grid = (bsz, d, ln // rows)
        args = [q, k, k, v, v, bias]
        specs = [cur(D_ATT), cur(D_ATT), halo, cur(D_ATT), halo, _resident(bias.shape)]
        scratch = [pltpu.VMEM((rows + qb, D_ATT), BF16), pltpu.VMEM((rows + qb, D_ATT), BF16)]
    else:
        per = min(ATT_STEP_BLOCKS, d)
        cur = lambda width: pl.BlockSpec((None, per, qb, width), lambda b, r: (b, r, 0, 0))
        grid = (bsz, d // per)
        args = [q, k, v, bias]
        specs = [cur(D_ATT), cur(D_ATT), cur(D_ATT), _resident(bias.shape)]
        scratch = []
    return pl.pallas_call(
        functools.partial(_attn_kernel, has_prev=has_prev),
        out_shape=out_shape,
        grid=grid,
        in_specs=specs,
        out_specs=(cur(D_ATT), cur(LANES)),
        scratch_shapes=scratch,
        compiler_params=pltpu.CompilerParams(vmem_limit_bytes=VMEM_LIMIT),
        name=f"attn_d{d}",
    )(*args)


def _dilated_attention(qkv, biases):
    return [_attn_branch(q, k, v, bias) for (q, k, v), bias in zip(qkv, biases)]


def _ssm_tables(*params):
    t, g, p, c = SSM_CHUNK, N_SSM_GROUPS, SSM_STATE, SSM_GROUP
    n = t * g * c
    w_in, w_intra, w_out, abar_t, d_row = jax.vmap(_ssm_group_blocks)(*params)
    kk = np.arange(t * c)[:, None]
    cc = np.arange(n)[None, :]
    spread_rp = jnp.asarray((cc // (g * p) == kk // p) & (cc % p == kk % p), BF16)
    spread_jc = jnp.asarray((cc // (g * c) == kk // c) & (cc % c == kk % c), BF16)
    blocks = (w_in.astype(BF16), w_intra.astype(BF16), w_out.astype(BF16))
    return blocks, (spread_rp, spread_jc), abar_t, d_row


def _ssm_group_blocks(a_re, a_im, log_dt, b_re, b_im, c_re, c_im, d_skip):
    hi = lax.Precision.HIGHEST
    t, g, p, c = SSM_CHUNK, N_SSM_GROUPS, SSM_STATE, SSM_GROUP
    dt = jnp.exp(log_dt)[:, None]
    mag = jnp.exp(a_re * dt)
    ar, ai = mag * jnp.cos(a_im * dt), mag * jnp.sin(a_im * dt)
    den = a_re * a_re + a_im * a_im
    fr = ((ar - 1.0) * a_re + ai * a_im) / den
    fi = (ai * a_re - (ar - 1.0) * a_im) / den
    bbr = fr[:, :, None] * b_re - fi[:, :, None] * b_im
    bbi = fr[:, :, None] * b_im + fi[:, :, None] * b_re
    pr, pi_ = [jnp.ones_like(ar)], [jnp.zeros_like(ar)]
    for _ in range(t):
        pr.append(pr[-1] * ar - pi_[-1] * ai)
        pi_.append(pr[-2] * ai + pi_[-1] * ar)
    n = t * g * c
    assert n == 2 * g * p and t * c == 2 * p

    bbr_t, bbi_t = jnp.transpose(bbr, (0, 2, 1)), jnp.transpose(bbi, (0, 2, 1))
    win = []
    for j in range(t):
        qr, qi = pr[t - 1 - j][:, None, :], pi_[t - 1 - j][:, None, :]
        win.append(jnp.concatenate([qr * bbr_t - qi * bbi_t, qr * bbi_t + qi * bbr_t], axis=-1))
    w_in = jnp.stack(win, 0).reshape(n, 2 * p)

    c_re_t, c_im_t = jnp.transpose(c_re, (0, 2, 1)), jnp.transpose(c_im, (0, 2, 1))
    wr_cols, wi_cols, hs = [], [], []
    for j in range(t + 1):
        if j >= 1:
            wr_cols.append(c_re_t * pr[j][:, :, None] - c_im_t * pi_[j][:, :, None])
            wi_cols.append(c_re_t * pi_[j][:, :, None] + c_im_t * pr[j][:, :, None])
        if j < t:
            wr = c_re * pr[j][:, None, :] - c_im * pi_[j][:, None, :]
            wi = c_re * pi_[j][:, None, :] + c_im * pr[j][:, None, :]
            hs.append(jnp.einsum('gcp,gpd->gdc', wr, bbr, precision=hi)
                      - jnp.einsum('gcp,gpd->gdc', wi, bbi, precision=hi))
    w_out = jnp.stack([jnp.concatenate(wr_cols, axis=-1), -jnp.concatenate(wi_cols, axis=-1)],
                      0).reshape(n, t * c)

    hcat = jnp.concatenate(hs, axis=-1)
    rows = [hcat if jp == 0 else
            jnp.concatenate([jnp.zeros((g, c, jp * c), F32), hcat[:, :, :(t - jp) * c]], axis=-1)
            for jp in range(t)]
    w_intra = jnp.stack(rows, 0).reshape(n, t * c)

    abar_t = jnp.concatenate([pr[t].reshape(1, g * p), pi_[t].reshape(1, g * p)], axis=1)
    d_row = jnp.tile(d_skip.reshape(1, D_SSM), (1, t))
    return w_in, w_intra, w_out, abar_t, d_row


def _ssm_kernel(u_ref, win_b, wintra_b, wout_b, srp_ref, sjc_ref, abar_ref, d_ref, y_ref,
                z_ref, xp_ref, win_ref, wintra_ref, wout_ref):
    n = u_ref.shape[0]
    half = N_SSM_GROUPS * SSM_STATE
    c, p, g = SSM_GROUP, SSM_STATE, N_SSM_GROUPS

    @pl.when(pl.program_id(0) == 0)
    def _():
        size = win_ref.shape[0]
        rows_per = SSM_EXPAND_ROWS
        col = lax.broadcasted_iota(jnp.int32, (rows_per, size), 1)
        row = lax.broadcasted_iota(jnp.int32, (rows_per, size), 0)
        for blk, dst, row_div, col_div, spread in ((win_b, win_ref, c, p, srp_ref),
                                                   (wintra_b, wintra_ref, c, c, sjc_ref),
                                                   (wout_b, wout_ref, p, c, sjc_ref)):
            cg = (col // col_div) % g
            for r0 in range(0, size, rows_per):
                full = jnp.dot(blk[r0:r0 + rows_per, :], spread[...], preferred_element_type=F32)
                rg = ((row + r0) // row_div) % g
                dst[r0:r0 + rows_per, :] = jnp.where(rg == cg, full, 0.0).astype(BF16)

    u = u_ref[...]
    ub = u.astype(BF16)
    z_ref[...] = jnp.dot(ub, win_ref[...], preferred_element_type=F32)
    ar = abar_ref[:, 0:half]
    ai = abar_ref[:, half:]

    def step(k, carry):
        xr, xi = carry
        xp_ref[pl.ds(k, 1), 0:half] = xr
        xp_ref[pl.ds(k, 1), half:] = xi
        zr = z_ref[pl.ds(k, 1), 0:half]
        zi = z_ref[pl.ds(k, 1), half:]
        return ar * xr - ai * xi + zr, ar * xi + ai * xr + zi

    zero = jnp.zeros((1, half), F32)
    lax.fori_loop(0, n, step, (zero, zero))
    y = jnp.dot(ub, wintra_ref[...], preferred_element_type=F32)
    y = y + jnp.dot(xp_ref[...].astype(BF16), wout_ref[...], preferred_element_type=F32)
    y_ref[...] = y + d_ref[...] * u


def _ssm(u, tables, l):
    blocks, spreads, abar_t, d_row = tables
    bsz, n, width = u.shape
    size = blocks[0].shape[1]
    row = pl.BlockSpec((None, n, width), lambda b: (b, 0, 0))
    return pl.pallas_call(
        _ssm_kernel,
        out_shape=jax.ShapeDtypeStruct((bsz, n, width), F32),
        grid=(bsz,),
        in_specs=[row] + [_stacked(t.shape[1:], l) for t in blocks] + [_resident(sp.shape) for sp in spreads]
        + [_stacked(abar_t.shape[1:], l), _stacked(d_row.shape[1:], l)],
        out_specs=row,
        scratch_shapes=[pltpu.VMEM((n, size), F32), pltpu.VMEM((n, size), F32)]
        + [pltpu.VMEM((size, size), BF16)] * 3,
        compiler_params=pltpu.CompilerParams(dimension_semantics=("arbitrary",), vmem_limit_bytes=VMEM_LIMIT),
        name="ssm",
    )(u, *blocks, *spreads, abar_t, d_row)


def _outproj_kernel(x_ref, mod_ref, *rest):
    nd = len(DILATIONS)
    o_refs, l_refs = rest[0:nd], rest[nd:2 * nd]
    (ys_ref, yp_ref, gw_ref, gb_ref, wa_ref, ws_ref, wp_ref, lg_ref, lb_ref, o_ref) = rest[2 * nd:2 * nd + 10]
    on4, on16, ot4, ln4, ln16, lt4, ysn_ref, ya_ref = rest[2 * nd + 10:]
    tm = x_ref.shape[0]
    slabs = D_ATT // LANES
    assert DILATIONS == (1, 4, 16)
    r4rows, r16rows = tm // 4, tm // 16

    for j in range(SSM_CHUNK):
        for c in range(D_SSM // LANES):
            lo = j * D_SSM + c * LANES
            ysn_ref[c, pl.ds(j, tm // SSM_CHUNK, stride=SSM_CHUNK), :] = ys_ref[:, lo:lo + LANES]

    ys = jnp.concatenate([ysn_ref[c] for c in range(D_SSM // LANES)], axis=1)
    cdf = 0.5 * (1.0 + jnp.tanh(math.sqrt(2.0 / math.pi) * (ys + 0.044715 * (ys * ys * ys))))
    t = jnp.dot((ys * cdf).astype(BF16), gw_ref[...], preferred_element_type=F32) + gb_ref[...]
    yg = (ys * _sigmoid(t)).astype(BF16)
    y_rest = (jnp.dot(yg, ws_ref[...], preferred_element_type=F32)
              + jnp.dot(yp_ref[...], wp_ref[...], preferred_element_type=F32))

    for r4 in range(4):
        ln4[pl.ds(r4, r4rows, stride=4), :] = l_refs[1][r4]
        for c4 in range(4):
            lt4[pl.ds(r4 * r4rows + c4, r16rows, stride=4), :] = l_refs[2][r4 + 4 * c4]
        for c in range(slabs):
            lanes = slice(c * LANES, (c + 1) * LANES)
            on4[c, pl.ds(r4, r4rows, stride=4), :] = o_refs[1][r4, :, lanes].astype(F32)
            for c4 in range(4):
                ot4[c, pl.ds(r4 * r4rows + c4, r16rows, stride=4), :] = o_refs[2][r4 + 4 * c4, :, lanes].astype(F32)
    for r4 in range(4):
        ln16[pl.ds(r4, r4rows, stride=4), :] = lt4[r4 * r4rows:(r4 + 1) * r4rows, :]
        for c in range(slabs):
            on16[c, pl.ds(r4, r4rows, stride=4), :] = ot4[c, r4 * r4rows:(r4 + 1) * r4rows, :]

    head = lax.broadcasted_iota(jnp.int32, (LANES, D_ATT), 1) // HEAD_DIM
    stat_of_head = jnp.where(head % 2 == 1, head, HEAD_DIM + head)
    spread = (stat_of_head == lax.broadcasted_iota(jnp.int32, (LANES, D_ATT), 0)).astype(BF16)
    st_lane = lax.broadcasted_iota(jnp.int32, (1, LANES), 1)
    is_max_lane = functools.reduce(jnp.logical_or, [st_lane == _stat_lane(h) for h in range(N_HEADS)])
    for rc in range(tm // MERGE_ROWS):
        rs = slice(rc * MERGE_ROWS, (rc + 1) * MERGE_ROWS)
        st_nat = [l_refs[0][0, rs, :], ln4[rs, :], ln16[rs, :]]
        o_nat = [o_refs[0][0, rs, :].astype(F32),
                 jnp.concatenate([on4[c, rs, :] for c in range(slabs)], axis=1),
                 jnp.concatenate([on16[c, rs, :] for c in range(slabs)], axis=1)]
        mx = functools.reduce(jnp.maximum, st_nat)
        es = [jnp.exp2(st - mx) for st in st_nat]
        sums = [pltpu.roll(st, LANES - STAT_SUM_OFFSET, axis=1) for st in st_nat]
        inv = 1.0 / functools.reduce(lambda a, b: a + b, [e * l for e, l in zip(es, sums)])
        ya = None
        for e, ov in zip(es, o_nat):
            w = jnp.where(is_max_lane, e * inv, 0.0)
            wx = jnp.dot(w.astype(BF16), spread, preferred_element_type=F32)
            ya = wx * ov if ya is None else ya + wx * ov
        ya_ref[rs, :] = ya.astype(BF16)

    x = x_ref[...]
    gate = mod_ref[5:6, :]
    y = jnp.dot(ya_ref[...], wa_ref[...], preferred_element_type=F32) + y_rest
    r = ALPHA * x + gate * y
    o_ref[...] = _layernorm(r) * lg_ref[...] + lb_ref[...]


def _outproj(x, mod_all, att, y_ssm, y_pool, glu_w, glu_b, w_out, ln_g, ln_b, l):
    bsz, s, d = x.shape
    tm = OUTPROJ_TOKEN_TILE
    tok = lambda width: pl.BlockSpec((None, tm, width), lambda b, i: (b, i, 0))
    res = lambda dil, width: pl.BlockSpec((None, dil, tm // dil, width), lambda b, i: (b, 0, i, 0))
    rows_of = lambda r0, n: pl.BlockSpec((None, n, d), lambda *_: (l, r0 // n, 0), pipeline_mode=pl.Buffered(1))
    assert D_ATT % D_SSM == 0 and D_SSM == D_POOL
    slab = lambda n: pltpu.VMEM((n, tm, LANES), F32)
    scratch = [slab(D_ATT // LANES)] * 3 + [pltpu.VMEM((tm, LANES), F32)] * 3
    scratch += [slab(D_SSM // LANES), pltpu.VMEM((tm, D_ATT), BF16)]
    return pl.pallas_call(
        _outproj_kernel,
        out_shape=jax.ShapeDtypeStruct(x.shape, F32),
        grid=(bsz, s // tm),
        in_specs=[tok(d), _mod_spec(l, d)]
        + [res(dil, D_ATT) for dil in DILATIONS]
        + [res(dil, LANES) for dil in DILATIONS]
        + [pl.BlockSpec((None, tm // SSM_CHUNK, SSM_CHUNK * D_SSM), lambda b, i: (b, i, 0)), tok(D_POOL),
           _stacked(glu_w.shape[1:], l), _stacked((1, D_SSM), l),
           rows_of(0, D_ATT), rows_of(D_ATT, D_SSM), rows_of(D_ATT + D_SSM, D_POOL),
           _stacked((1, d), l, 1), _stacked((1, d), l, 1)],
        out_specs=tok(d),
        scratch_shapes=scratch,
        compiler_params=pltpu.CompilerParams(vmem_limit_bytes=VMEM_LIMIT),
        name="outproj",
    )(x, mod_all, *[o for o, _ in att], *[st for _, st in att], y_ssm, y_pool,
      glu_w, glu_b, w_out, w_out, w_out, ln_g, ln_b)


def _block_diag(w):
    g, n, _ = w.shape
    return jnp.einsum('gab,gh->gahb', w, jnp.eye(g, dtype=w.dtype)).reshape(g * n, g * n)


def kernel(x, c, rel_bias, ada_w, ada_b, ln_g, ln_b, ffn_w_gate, ffn_w_up, ffn_w_down, w_in, w_out,
           ssm_a_re, ssm_a_im, ssm_log_dt, ssm_b_re, ssm_b_im, ssm_c_re, ssm_c_im, ssm_d, glu_w, glu_b,
           pool_w, pool_scale):
    bsz = x.shape[0]
    mod_all = _adaln(c, ada_w, ada_b).reshape(DEPTH, bsz, 9, D_MODEL)
    wg, wu, wd = ffn_w_gate, ffn_w_up, ffn_w_down
    w_in_b, w_out_b, glu_w_b = w_in.astype(BF16), w_out.astype(BF16), glu_w.astype(BF16)
    ln_g4, ln_b4 = ln_g.reshape(DEPTH, 3, 1, D_MODEL), ln_b.reshape(DEPTH, 3, 1, D_MODEL)
    glu_b3 = glu_b.reshape(DEPTH, 1, D_SSM)
    pool_scale3 = pool_scale.reshape(DEPTH, 1, D_POOL)
    pool_w_bd = jax.vmap(_block_diag)(pool_w).astype(BF16)
    biases = [_branch_bias(rel_bias, window, dilation, x.shape[1] // dilation > ATT_BLOCK)
              for window, dilation in DILATED_PATTERNS]
    tables = _ssm_tables(ssm_a_re, ssm_a_im, ssm_log_dt, ssm_b_re, ssm_b_im, ssm_c_re, ssm_c_im, ssm_d)
    for l in range(DEPTH):
        x = _ffn(x, mod_all, wg, wu, wd, ln_g4, ln_b4, l, 0)
        qkv, u_ssm, y_pool = _inproj(x, mod_all, w_in_b, pool_w_bd, pool_scale3, l)
        att = _dilated_attention(qkv, biases)
        y_ssm = _ssm(u_ssm, tables, l)
        x = _outproj(x, mod_all, att, y_ssm, y_pool, glu_w_b, glu_b3, w_out_b, ln_g4, ln_b4, l)
        x = _ffn(x, mod_all, wg, wu, wd, ln_g4, ln_b4, l, 1)
    return x
```

```python
import functools
import math

import jax
import jax.numpy as jnp
import numpy as np
from jax import lax
from jax.experimental import pallas as pl
from jax.experimental.pallas import tpu as pltpu

F32 = jnp.float32
BF16 = jnp.bfloat16

D_MODEL = 1024
DEPTH = 2
HEAD_DIM = 64
N_HEADS = 8
D_ATT = N_HEADS * HEAD_DIM
DILATED_PATTERNS = ((128, 1), (512, 4), (2048, 16))
DILATIONS = tuple(d for _, d in DILATED_PATTERNS)
ATT_BLOCK = 128
SSM_GROUP = 16
D_SSM = 256
N_SSM_GROUPS = D_SSM // SSM_GROUP
SSM_STATE = 64
POOL_WINDOWS = (2, 4, 8, 16)
D_POOL = 256
POOL_GROUP = D_POOL // len(POOL_WINDOWS)
D_IN = 3 * D_ATT + D_SSM + D_POOL
D_FF = 2816
N_BUCKETS = 32
MAX_DISTANCE = 2048
ALPHA = (2 * DEPTH) ** 0.25
FFN_RES = 0.5
LN_EPS = 1e-5
NEG = -1e30

LANES = 128
TOKEN_TILE = 512
OUTPROJ_TOKEN_TILE = 1024
FF_CHUNK = 256
FFN_TOKEN_TILE = 512
ADA_COL_TILE = 2304
SSM_CHUNK = 8
SSM_EXPAND_ROWS = 256
POOL_HALO = 2 * max(POOL_WINDOWS)
ATT_LOOKAHEAD = 8
ATT_STEP_BLOCKS = 16
MERGE_ROWS = 128
LOG2E = math.log2(math.e)
STAT_SUM_OFFSET = 16
VMEM_LIMIT = 56 * 1024 * 1024


def _sigmoid(x):
    return 1.0 / (1.0 + jnp.exp(-x))


def _layernorm(x):
    mu = jnp.mean(x, axis=-1, keepdims=True)
    xc = x - mu
    var = jnp.mean(xc * xc, axis=-1, keepdims=True)
    return xc * lax.rsqrt(var + LN_EPS)


def _resident(shape):
    zeros = (0,) * len(shape)
    return pl.BlockSpec(shape, lambda *_: zeros, pipeline_mode=pl.Buffered(1))


def _stacked(tail, *lead):
    idx = tuple(lead) + (0,) * len(tail)
    return pl.BlockSpec((None,) * len(lead) + tuple(tail), lambda *_: idx, pipeline_mode=pl.Buffered(1))


def _adaln_kernel(c_ref, w_ref, b_ref, o_ref):
    c = c_ref[...]
    cond = (c * _sigmoid(c)).astype(BF16)
    o_ref[...] = jnp.dot(cond, w_ref[...].astype(BF16), preferred_element_type=F32) + b_ref[...]


def _adaln(c, ada_w, ada_b):
    nl, d, n = ada_w.shape
    bsz = c.shape[0]
    return pl.pallas_call(
        _adaln_kernel,
        out_shape=jax.ShapeDtypeStruct((nl, bsz, n), F32),
        grid=(nl, n // ADA_COL_TILE),
        in_specs=[
            pl.BlockSpec((bsz, d), lambda l, j: (0, 0)),
            pl.BlockSpec((None, d, ADA_COL_TILE), lambda l, j: (l, 0, j)),
            pl.BlockSpec((None, 1, ADA_COL_TILE), lambda l, j: (l, 0, j)),
        ],
        out_specs=pl.BlockSpec((None, bsz, ADA_COL_TILE), lambda l, j: (l, 0, j)),
        compiler_params=pltpu.CompilerParams(vmem_limit_bytes=VMEM_LIMIT),
        name="adaln",
    )(c, ada_w, ada_b.reshape(nl, 1, n))


def _ff_chunks():
    chunks, c0 = [], 0
    while c0 < D_FF:
        cw = min(FF_CHUNK, D_FF - c0)
        chunks.append((c0, cw))
        c0 += cw
    return tuple(chunks)


def _ffn_kernel(x_ref, mod_ref, wg_ref, wu_ref, wd_ref, lg_ref, lb_ref, o_ref, acc_ref, *, sub):
    x = x_ref[...]
    shift = mod_ref[3 * sub + 0:3 * sub + 1, :]
    scale = mod_ref[3 * sub + 1:3 * sub + 2, :]
    gate = mod_ref[3 * sub + 2:3 * sub + 3, :]
    h = (_layernorm(x) * (1.0 + scale) + shift).astype(BF16)
    for idx, (c0, cw) in enumerate(_ff_chunks()):
        g = jnp.dot(h, wg_ref[:, c0:c0 + cw].astype(BF16), preferred_element_type=F32)
        u = jnp.dot(h, wu_ref[:, c0:c0 + cw].astype(BF16), preferred_element_type=F32)
        a = (g * _sigmoid(g) * u).astype(BF16)
        d = jnp.dot(a, wd_ref[c0:c0 + cw, :].astype(BF16), preferred_element_type=F32)
        if idx == 0:
            acc_ref[...] = d
        else:
            acc_ref[...] += d
    y = ALPHA * x + (FFN_RES * gate) * acc_ref[...]
    o_ref[...] = _layernorm(y) * lg_ref[...] + lb_ref[...]


def _mod_spec(l, d):
    return pl.BlockSpec((None, None, 9, d), lambda b, i: (l, b, 0, 0))


def _ffn(x, mod_all, wg, wu, wd, ln_g, ln_b, l, which):
    bsz, s, d = x.shape
    tm = FFN_TOKEN_TILE
    sub = 2 * which
    return pl.pallas_call(
        functools.partial(_ffn_kernel, sub=sub),
        out_shape=jax.ShapeDtypeStruct(x.shape, F32),
        grid=(bsz, s // tm),
        in_specs=[
            pl.BlockSpec((None, tm, d), lambda b, i: (b, i, 0)),
            _mod_spec(l, d),
            _stacked(wg.shape[2:], l, which),
            _stacked(wu.shape[2:], l, which),
            _stacked(wd.shape[2:], l, which),
            _stacked((1, d), l, sub),
            _stacked((1, d), l, sub),
        ],
        out_specs=pl.BlockSpec((None, tm, d), lambda b, i: (b, i, 0)),
        scratch_shapes=[pltpu.VMEM((tm, d), F32)],
        compiler_params=pltpu.CompilerParams(vmem_limit_bytes=VMEM_LIMIT),
        name=f"ffn{sub}",
    )(x, mod_all, wg, wu, wd, ln_g, ln_b)


def _inproj_kernel(x_ref, mod_ref, w_ref, pw_ref, ps_ref, *rest):
    qkv_refs = rest[:3 * len(DILATIONS)]
    us_ref, yp_ref, zs_ref, z4_ref = rest[3 * len(DILATIONS):3 * len(DILATIONS) + 4]
    e_refs = rest[3 * len(DILATIONS) + 4:]
    i = pl.program_id(1)
    tm = x_ref.shape[0]
    x = x_ref[...]
    shift = mod_ref[3:4, :]
    scale = mod_ref[4:5, :]
    h = (_layernorm(x) * (1.0 + scale) + shift).astype(BF16)
    z = jnp.dot(h, w_ref[...], preferred_element_type=F32)
    up = z[:, 3 * D_ATT + D_SSM:]

    slabs_per = D_ATT // LANES
    n_qkv = 3 * slabs_per
    n_slabs = (3 * D_ATT + D_SSM) // LANES
    for c in range(n_slabs):
        col = z[:, c * LANES:(c + 1) * LANES]
        zs_ref[c] = col * (HEAD_DIM ** -0.5 * LOG2E) if c < slabs_per else col
    assert DILATIONS == (1, 4, 16)
    q4rows, q16rows = tm // 4, tm // 16
    for c in range(n_qkv):
        which, lanes = c // slabs_per, slice((c % slabs_per) * LANES, (c % slabs_per + 1) * LANES)
        qkv_refs[which][0, :, lanes] = zs_ref[c].astype(BF16)
        for r4 in range(4):
            blk = zs_ref[c, pl.ds(r4, q4rows, stride=4), :]
            z4_ref[c, r4 * q4rows:(r4 + 1) * q4rows, :] = blk
            qkv_refs[3 + which][r4, :, lanes] = blk.astype(BF16)
        for r4 in range(4):
            for c4 in range(4):
                blk = z4_ref[c, pl.ds(r4 * q4rows + c4, q16rows, stride=4), :]
                qkv_refs[6 + which][r4 + 4 * c4, :, lanes] = blk.astype(BF16)
    for c in range(n_qkv, n_slabs):
        for j in range(SSM_CHUNK):
            lo = j * D_SSM + (c - n_qkv) * LANES
            us_ref[:, lo:lo + LANES] = zs_ref[c, pl.ds(j, tm // SSM_CHUNK, stride=SSM_CHUNK), :]

    e1, e2, e4, e8 = e_refs
    hl = POOL_HALO
    assert POOL_WINDOWS == (2, 4, 8, 16) and hl == 32

    @pl.when(i == 0)
    def _():
        e1[0:hl, :] = jnp.zeros((hl, D_POOL), F32)

    @pl.when(i > 0)
    def _():
        e1[0:hl, :] = e1[tm:tm + hl, :]

    e1[hl:hl + tm, :] = up
    e2[8:, :] = e1[8:, :] + e1[7:tm + hl - 1, :]
    e4[16:, :] = e2[16:, :] + e2[14:tm + hl - 2, :]
    e8[24:, :] = e4[24:, :] + e4[20:tm + hl - 4, :]
    sums = {2: e2[hl:, :], 4: e4[hl:, :], 8: e8[hl:, :], 16: e8[hl:, :] + e8[hl - 8:tm + hl - 8, :]}
    pos = (i * tm + lax.broadcasted_iota(jnp.int32, (tm, 1), 0) + 1).astype(F32)
    group = lax.broadcasted_iota(jnp.int32, (1, D_POOL), 1) // POOL_GROUP
    mean = sums[POOL_WINDOWS[-1]] / jnp.minimum(pos, float(POOL_WINDOWS[-1]))
    for gi in range(len(POOL_WINDOWS) - 2, -1, -1):
        w = POOL_WINDOWS[gi]
        mean = jnp.where(group == gi, sums[w] / jnp.minimum(pos, float(w)), mean)
    pooled = (mean - up).astype(BF16)
    yp = jnp.dot(pooled, pw_ref[...], preferred_element_type=F32) * ps_ref[...]
    yp_ref[...] = yp.astype(BF16)


def _inproj(x, mod_all, w_in, pool_w_bd, pool_scale, l):
    bsz, s, d = x.shape
    tm = TOKEN_TILE
    tok = lambda width: pl.BlockSpec((None, tm, width), lambda b, i: (b, i, 0))
    qkv_shapes, qkv_specs = [], []
    for dil in DILATIONS:
        for _ in range(3):
            qkv_shapes.append(jax.ShapeDtypeStruct((bsz, dil, s // dil, D_ATT), BF16))
            qkv_specs.append(pl.BlockSpec((None, dil, tm // dil, D_ATT), lambda b, i: (b, 0, i, 0)))
    res = pl.pallas_call(
        _inproj_kernel,
        out_shape=tuple(qkv_shapes) + (
            jax.ShapeDtypeStruct((bsz, s // SSM_CHUNK, SSM_CHUNK * D_SSM), F32),
            jax.ShapeDtypeStruct((bsz, s, D_POOL), BF16),
        ),
        grid=(bsz, s // tm),
        in_specs=[
            tok(d),
            _mod_spec(l, d),
            _stacked(w_in.shape[1:], l),
            _stacked(pool_w_bd.shape[1:], l),
            _stacked((1, D_POOL), l),
        ],
        out_specs=tuple(qkv_specs) + (
            pl.BlockSpec((None, tm // SSM_CHUNK, SSM_CHUNK * D_SSM), lambda b, i: (b, i, 0)), tok(D_POOL)),
        scratch_shapes=[pltpu.VMEM(((3 * D_ATT + D_SSM) // LANES, tm, LANES), F32),
                        pltpu.VMEM((3 * D_ATT // LANES, tm, LANES), F32)]
        + [pltpu.VMEM((tm + POOL_HALO, D_POOL), F32) for _ in range(4)],
        compiler_params=pltpu.CompilerParams(
            dimension_semantics=("arbitrary", "arbitrary"), vmem_limit_bytes=VMEM_LIMIT),
        name="inproj",
    )(x, mod_all, w_in, pool_w_bd, pool_scale)
    qkv = [tuple(res[3 * di:3 * di + 3]) for di in range(len(DILATIONS))]
    return qkv, res[-2], res[-1]


def _t5_bucket(dist):
    max_exact = N_BUCKETS // 2
    dd = np.maximum(dist, 1).astype(np.float32)
    large = max_exact + (np.log(dd / max_exact) / math.log(MAX_DISTANCE / max_exact)
                         * (N_BUCKETS - max_exact)).astype(np.int32)
    large = np.minimum(large, N_BUCKETS - 1)
    return np.where(dist < max_exact, dist, large).astype(np.int32)


def _branch_bias(rel_bias, window, dilation, has_prev):
    qb = ATT_BLOCK
    n_keys = window // dilation
    assert n_keys == qb
    period = 3 * qb
    dist = np.arange(n_keys, -1, -1)
    row = rel_bias[_t5_bucket(dist * dilation)].T.astype(F32) * LOG2E
    row = jnp.concatenate([row, jnp.full((N_HEADS, period - n_keys - 1), NEG, F32)], axis=1)
    flat = jnp.tile(row, (1, qb))[:, :qb * (period - 1)]
    bias = flat.reshape(N_HEADS, qb, period - 1)[:, :, :2 * qb]
    if not has_prev:
        return bias[:, :, qb:]
    first = jnp.concatenate([jnp.full((N_HEADS, qb, qb), NEG, F32), bias[:, :, qb:]], axis=2)
    return jnp.stack([bias, first], 0)


def _stat_lane(h):
    return h if h % 2 == 1 else HEAD_DIM + h


def _attn_kernel(*refs, has_prev):
    qb = ATT_BLOCK
    if has_prev:
        q_ref, kc_ref, kh_ref, vc_ref, vh_ref, bias_ref, o_ref, st_ref, kbuf, vbuf = refs
        nsub = q_ref.shape[0] // qb
        kbuf[0:qb, :] = kh_ref[...]
        kbuf[qb:, :] = kc_ref[...]
        vbuf[0:qb, :] = vh_ref[...]
        vbuf[qb:, :] = vc_ref[...]
    else:
        q_ref, kc_ref, vc_ref, bias_ref, o_ref, st_ref = refs
        nsub = q_ref.shape[0]

    pair = 2 * HEAD_DIM
    assert pair == LANES
    lane = lax.broadcasted_iota(jnp.int32, (1, pair), 1)
    low = lane < HEAD_DIM
    stat_lane = lane

    def block(jj, carry):
        if has_prev:
            row0 = pl.multiple_of(jj * qb, qb)
            first = jnp.logical_and(pl.program_id(2) == 0, jj == 0).astype(jnp.int32)
            q_at = lambda cols: q_ref[pl.ds(row0, qb), cols]
            k_at = lambda cols: kbuf[pl.ds(row0, 2 * qb), cols]
            v_at = lambda cols: vbuf[pl.ds(row0, 2 * qb), cols]
            bias_at = lambda h: bias_ref[first, h]
        else:
            q_at = lambda cols: q_ref[jj, :, cols]
            k_at = lambda cols: kc_ref[jj, :, cols]
            v_at = lambda cols: vc_ref[jj, :, cols]
            bias_at = lambda h: bias_ref[h]

        def pair_cols(h):
            return slice((h // 2) * pair, (h // 2 + 1) * pair)

        def scores(h):
            q2 = q_at(pair_cols(h))
            qh = jnp.where(low, q2, jnp.zeros_like(q2)) if h % 2 == 0 else jnp.where(low, jnp.zeros_like(q2), q2)
            s = lax.dot_general(qh, k_at(pair_cols(h)), (((1,), (1,)), ((), ())), preferred_element_type=F32)
            return s + bias_at(h)

        pending = {h: scores(h) for h in range(min(ATT_LOOKAHEAD, N_HEADS))}
        outs = {}
        stats = jnp.ones((qb, LANES), F32)
        for h in range(N_HEADS):
            s = pending.pop(h)
            m = jnp.max(s, axis=-1, keepdims=True)
            p = jnp.exp2(s - m)
            if h + ATT_LOOKAHEAD < N_HEADS:
                pending[h + ATT_LOOKAHEAD] = scores(h + ATT_LOOKAHEAD)
            cols = pair_cols(h)
            v2 = v_at(cols)
            v2 = jnp.where(low, v2, jnp.ones_like(v2)) if h % 2 == 0 else jnp.where(low, jnp.ones_like(v2), v2)
            pv = jnp.dot(p.astype(BF16), v2, preferred_element_type=F32)
            outs[h] = pv
            stats = jnp.where(stat_lane == _stat_lane(h), m, stats)
            stats = jnp.where(stat_lane == _stat_lane(h) + STAT_SUM_OFFSET, pv, stats)
            if h % 2 == 1:
                o2 = jnp.where(low, outs.pop(h - 1), outs.pop(h)).astype(BF16)
                if has_prev:
                    o_ref[pl.ds(row0, qb), cols] = o2
                else:
                    o_ref[jj, :, cols] = o2
        if has_prev:
            st_ref[pl.ds(row0, qb), :] = stats
        else:
            st_ref[jj] = stats
        return carry

    lax.fori_loop(0, nsub, block, 0, unroll=True)


def _attn_branch(q, k, v, bias):
    bsz, d, ln, _ = q.shape
    qb = ATT_BLOCK
    has_prev = ln > qb
    out_shape = (jax.ShapeDtypeStruct((bsz, d, ln, D_ATT), BF16),
                 jax.ShapeDtypeStruct((bsz, d, ln, LANES), F32))
    if has_prev:
        rows = min(ATT_STEP_BLOCKS * qb, ln)
        per = rows // qb
        cur = lambda width: pl.BlockSpec((None, None, rows, width), lambda b, r, j: (b, r, j, 0))
        halo = pl.BlockSpec((None, None, qb, D_ATT), lambda b, r, j: (b, r, jnp.maximum(j * per - 1, 0), 0))
        grid = (bsz, d, ln // rows)
        args = [q, k, k, v, v, bias]
        specs = [cur(D_ATT), cur(D_ATT), halo, cur(D_ATT), halo, _resident(bias.shape)]
        scratch = [pltpu.VMEM((rows + qb, D_ATT), BF16), pltpu.VMEM((rows + qb, D_ATT), BF16)]
    else:
        per = min(ATT_STEP_BLOCKS, d)
        cur = lambda width: pl.BlockSpec((None, per, qb, width), lambda b, r: (b, r, 0, 0))
        grid = (bsz, d // per)
        args = [q, k, v, bias]
        specs = [cur(D_ATT), cur(D_ATT), cur(D_ATT), _resident(bias.shape)]
        scratch = []
    return pl.pallas_call(
        functools.partial(_attn_kernel, has_prev=has_prev),
        out_shape=out_shape,
        grid=grid,
        in_specs=specs,
        out_specs=(cur(D_ATT), cur(LANES)),
        scratch_shapes=scratch,
        compiler_params=pltpu.CompilerParams(vmem_limit_bytes=VMEM_LIMIT),
        name=f"attn_d{d}",
    )(*args)


def _dilated_attention(qkv, biases):
    return [_attn_branch(q, k, v, bias) for (q, k, v), bias in zip(qkv, biases)]


def _ssm_tables(*params):
    t, g, p, c = SSM_CHUNK, N_SSM_GROUPS, SSM_STATE, SSM_GROUP
    n = t * g * c
    w_in, w_intra, w_out, abar_t, d_row = jax.vmap(_ssm_group_blocks)(*params)
    kk = np.arange(t * c)[:, None]
    cc = np.arange(n)[None, :]
    spread_rp = jnp.asarray((cc // (g * p) == kk // p) & (cc % p == kk % p), BF16)
    spread_jc = jnp.asarray((cc // (g * c) == kk // c) & (cc % c == kk % c), BF16)
    blocks = (w_in.astype(BF16), w_intra.astype(BF16), w_out.astype(BF16))
    return blocks, (spread_rp, spread_jc), abar_t, d_row


def _ssm_group_blocks(a_re, a_im, log_dt, b_re, b_im, c_re, c_im, d_skip):
    hi = lax.Precision.HIGHEST
    t, g, p, c = SSM_CHUNK, N_SSM_GROUPS, SSM_STATE, SSM_GROUP
    dt = jnp.exp(log_dt)[:, None]
    mag = jnp.exp(a_re * dt)
    ar, ai = mag * jnp.cos(a_im * dt), mag * jnp.sin(a_im * dt)
    den = a_re * a_re + a_im * a_im
    fr = ((ar - 1.0) * a_re + ai * a_im) / den
    fi = (ai * a_re - (ar - 1.0) * a_im) / den
    bbr = fr[:, :, None] * b_re - fi[:, :, None] * b_im
    bbi = fr[:, :, None] * b_im + fi[:, :, None] * b_re
    pr, pi_ = [jnp.ones_like(ar)], [jnp.zeros_like(ar)]
    for _ in range(t):
        pr.append(pr[-1] * ar - pi_[-1] * ai)
        pi_.append(pr[-2] * ai + pi_[-1] * ar)
    n = t * g * c
    assert n == 2 * g * p and t * c == 2 * p

    bbr_t, bbi_t = jnp.transpose(bbr, (0, 2, 1)), jnp.transpose(bbi, (0, 2, 1))
    win = []
    for j in range(t):
        qr, qi = pr[t - 1 - j][:, None, :], pi_[t - 1 - j][:, None, :]
        win.append(jnp.concatenate([qr * bbr_t - qi * bbi_t, qr * bbi_t + qi * bbr_t], axis=-1))
    w_in = jnp.stack(win, 0).reshape(n, 2 * p)

    c_re_t, c_im_t = jnp.transpose(c_re, (0, 2, 1)), jnp.transpose(c_im, (0, 2, 1))
    wr_cols, wi_cols, hs = [], [], []
    for j in range(t + 1):
        if j >= 1:
            wr_cols.append(c_re_t * pr[j][:, :, None] - c_im_t * pi_[j][:, :, None])
            wi_cols.append(c_re_t * pi_[j][:, :, None] + c_im_t * pr[j][:, :, None])
        if j < t:
            wr = c_re * pr[j][:, None, :] - c_im * pi_[j][:, None, :]
            wi = c_re * pi_[j][:, None, :] + c_im * pr[j][:, None, :]
            hs.append(jnp.einsum('gcp,gpd->gdc', wr, bbr, precision=hi)
                      - jnp.einsum('gcp,gpd->gdc', wi, bbi, precision=hi))
    w_out = jnp.stack([jnp.concatenate(wr_cols, axis=-1), -jnp.concatenate(wi_cols, axis=-1)],
                      0).reshape(n, t * c)

    hcat = jnp.concatenate(hs, axis=-1)
    rows = [hcat if jp == 0 else
            jnp.concatenate([jnp.zeros((g, c, jp * c), F32), hcat[:, :, :(t - jp) * c]], axis=-1)
            for jp in range(t)]
    w_intra = jnp.stack(rows, 0).reshape(n, t * c)

    abar_t = jnp.concatenate([pr[t].reshape(1, g * p), pi_[t].reshape(1, g * p)], axis=1)
    d_row = jnp.tile(d_skip.reshape(1, D_SSM), (1, t))
    return w_in, w_intra, w_out, abar_t, d_row


def _ssm_kernel(u_ref, win_b, wintra_b, wout_b, srp_ref, sjc_ref, abar_ref, d_ref, y_ref,
                z_ref, xp_ref, win_ref, wintra_ref, wout_ref):
    n = u_ref.shape[0]
    half = N_SSM_GROUPS * SSM_STATE
    c, p, g = SSM_GROUP, SSM_STATE, N_SSM_GROUPS

    @pl.when(pl.program_id(0) == 0)
    def _():
        size = win_ref.shape[0]
        rows_per = SSM_EXPAND_ROWS
        col = lax.broadcasted_iota(jnp.int32, (rows_per, size), 1)
        row = lax.broadcasted_iota(jnp.int32, (rows_per, size), 0)
        for blk, dst, row_div, col_div, spread in ((win_b, win_ref, c, p, srp_ref),
                                                   (wintra_b, wintra_ref, c, c, sjc_ref),
                                                   (wout_b, wout_ref, p, c, sjc_ref)):
            cg = (col // col_div) % g
            for r0 in range(0, size, rows_per):
                full = jnp.dot(blk[r0:r0 + rows_per, :], spread[...], preferred_element_type=F32)
                rg = ((row + r0) // row_div) % g
                dst[r0:r0 + rows_per, :] = jnp.where(rg == cg, full, 0.0).astype(BF16)

    u = u_ref[...]
    ub = u.astype(BF16)
    z_ref[...] = jnp.dot(ub, win_ref[...], preferred_element_type=F32)
    ar = abar_ref[:, 0:half]
    ai = abar_ref[:, half:]

    def step(k, carry):
        xr, xi = carry
        xp_ref[pl.ds(k, 1), 0:half] = xr
        xp_ref[pl.ds(k, 1), half:] = xi
        zr = z_ref[pl.ds(k, 1), 0:half]
        zi = z_ref[pl.ds(k, 1), half:]
        return ar * xr - ai * xi + zr, ar * xi + ai * xr + zi

    zero = jnp.zeros((1, half), F32)
    lax.fori_loop(0, n, step, (zero, zero))
    y = jnp.dot(ub, wintra_ref[...], preferred_element_type=F32)
    y = y + jnp.dot(xp_ref[...].astype(BF16), wout_ref[...], preferred_element_type=F32)
    y_ref[...] = y + d_ref[...] * u


def _ssm(u, tables, l):
    blocks, spreads, abar_t, d_row = tables
    bsz, n, width = u.shape
    size = blocks[0].shape[1]
    row = pl.BlockSpec((None, n, width), lambda b: (b, 0, 0))
    return pl.pallas_call(
        _ssm_kernel,
        out_shape=jax.ShapeDtypeStruct((bsz, n, width), F32),
        grid=(bsz,),
        in_specs=[row] + [_stacked(t.shape[1:], l) for t in blocks] + [_resident(sp.shape) for sp in spreads]
        + [_stacked(abar_t.shape[1:], l), _stacked(d_row.shape[1:], l)],
        out_specs=row,
        scratch_shapes=[pltpu.VMEM((n, size), F32), pltpu.VMEM((n, size), F32)]
        + [pltpu.VMEM((size, size), BF16)] * 3,
        compiler_params=pltpu.CompilerParams(dimension_semantics=("arbitrary",), vmem_limit_bytes=VMEM_LIMIT),
        name="ssm",
    )(u, *blocks, *spreads, abar_t, d_row)


def _outproj_kernel(x_ref, mod_ref, *rest):
    nd = len(DILATIONS)
    o_refs, l_refs = rest[0:nd], rest[nd:2 * nd]
    (ys_ref, yp_ref, gw_ref, gb_ref, wa_ref, ws_ref, wp_ref, lg_ref, lb_ref, o_ref) = rest[2 * nd:2 * nd + 10]
    on4, on16, ot4, ln4, ln16, lt4, ysn_ref, ya_ref = rest[2 * nd + 10:]
    tm = x_ref.shape[0]
    slabs = D_ATT // LANES
    assert DILATIONS == (1, 4, 16)
    r4rows, r16rows = tm // 4, tm // 16

    for j in range(SSM_CHUNK):
        for c in range(D_SSM // LANES):
            lo = j * D_SSM + c * LANES
            ysn_ref[c, pl.ds(j, tm // SSM_CHUNK, stride=SSM_CHUNK), :] = ys_ref[:, lo:lo + LANES]

    ys = jnp.concatenate([ysn_ref[c] for c in range(D_SSM // LANES)], axis=1)
    cdf = 0.5 * (1.0 + jnp.tanh(math.sqrt(2.0 / math.pi) * (ys + 0.044715 * (ys * ys * ys))))
    t = jnp.dot((ys * cdf).astype(BF16), gw_ref[...], preferred_element_type=F32) + gb_ref[...]
    yg = (ys * _sigmoid(t)).astype(BF16)
    y_rest = (jnp.dot(yg, ws_ref[...], preferred_element_type=F32)
              + jnp.dot(yp_ref[...], wp_ref[...], preferred_element_type=F32))

    for r4 in range(4):
        ln4[pl.ds(r4, r4rows, stride=4), :] = l_refs[1][r4]
        for c4 in range(4):
            lt4[pl.ds(r4 * r4rows + c4, r16rows, stride=4), :] = l_refs[2][r4 + 4 * c4]
        for c in range(slabs):
            lanes = slice(c * LANES, (c + 1) * LANES)
            on4[c, pl.ds(r4, r4rows, stride=4), :] = o_refs[1][r4, :, lanes].astype(F32)
            for c4 in range(4):
                ot4[c, pl.ds(r4 * r4rows + c4, r16rows, stride=4), :] = o_refs[2][r4 + 4 * c4, :, lanes].astype(F32)
    for r4 in range(4):
        ln16[pl.ds(r4, r4rows, stride=4), :] = lt4[r4 * r4rows:(r4 + 1) * r4rows, :]
        for c in range(slabs):
            on16[c, pl.ds(r4, r4rows, stride=4), :] = ot4[c, r4 * r4rows:(r4 + 1) * r4rows, :]

    head = lax.broadcasted_iota(jnp.int32, (LANES, D_ATT), 1) // HEAD_DIM
    stat_of_head = jnp.where(head % 2 == 1, head, HEAD_DIM + head)
    spread = (stat_of_head == lax.broadcasted_iota(jnp.int32, (LANES, D_ATT), 0)).astype(BF16)
    st_lane = lax.broadcasted_iota(jnp.int32, (1, LANES), 1)
    is_max_lane = functools.reduce(jnp.logical_or, [st_lane == _stat_lane(h) for h in range(N_HEADS)])
    for rc in range(tm // MERGE_ROWS):
        rs = slice(rc * MERGE_ROWS, (rc + 1) * MERGE_ROWS)
        st_nat = [l_refs[0][0, rs, :], ln4[rs, :], ln16[rs, :]]
        o_nat = [o_refs[0][0, rs, :].astype(F32),
                 jnp.concatenate([on4[c, rs, :] for c in range(slabs)], axis=1),
                 jnp.concatenate([on16[c, rs, :] for c in range(slabs)], axis=1)]
        mx = functools.reduce(jnp.maximum, st_nat)
        es = [jnp.exp2(st - mx) for st in st_nat]
        sums = [pltpu.roll(st, LANES - STAT_SUM_OFFSET, axis=1) for st in st_nat]
        inv = 1.0 / functools.reduce(lambda a, b: a + b, [e * l for e, l in zip(es, sums)])
        ya = None
        for e, ov in zip(es, o_nat):
            w = jnp.where(is_max_lane, e * inv, 0.0)
            wx = jnp.dot(w.astype(BF16), spread, preferred_element_type=F32)
            ya = wx * ov if ya is None else ya + wx * ov
        ya_ref[rs, :] = ya.astype(BF16)

    x = x_ref[...]
    gate = mod_ref[5:6, :]
    y = jnp.dot(ya_ref[...], wa_ref[...], preferred_element_type=F32) + y_rest
    r = ALPHA * x + gate * y
    o_ref[...] = _layernorm(r) * lg_ref[...] + lb_ref[...]


def _outproj(x, mod_all, att, y_ssm, y_pool, glu_w, glu_b, w_out, ln_g, ln_b, l):
    bsz, s, d = x.shape
    tm = OUTPROJ_TOKEN_TILE
    tok = lambda width: pl.BlockSpec((None, tm, width), lambda b, i: (b, i, 0))
    res = lambda dil, width: pl.BlockSpec((None, dil, tm // dil, width), lambda b, i: (b, 0, i, 0))
    rows_of = lambda r0, n: pl.BlockSpec((None, n, d), lambda *_: (l, r0 // n, 0), pipeline_mode=pl.Buffered(1))
    assert D_ATT % D_SSM == 0 and D_SSM == D_POOL
    slab = lambda n: pltpu.VMEM((n, tm, LANES), F32)
    scratch = [slab(D_ATT // LANES)] * 3 + [pltpu.VMEM((tm, LANES), F32)] * 3
    scratch += [slab(D_SSM // LANES), pltpu.VMEM((tm, D_ATT), BF16)]
    return pl.pallas_call(
        _outproj_kernel,
        out_shape=jax.ShapeDtypeStruct(x.shape, F32),
        grid=(bsz, s // tm),
        in_specs=[tok(d), _mod_spec(l, d)]
        + [res(dil, D_ATT) for dil in DILATIONS]
        + [res(dil, LANES) for dil in DILATIONS]
        + [pl.BlockSpec((None, tm // SSM_CHUNK, SSM_CHUNK * D_SSM), lambda b, i: (b, i, 0)), tok(D_POOL),
           _stacked(glu_w.shape[1:], l), _stacked((1, D_SSM), l),
           rows_of(0, D_ATT), rows_of(D_ATT, D_SSM), rows_of(D_ATT + D_SSM, D_POOL),
           _stacked((1, d), l, 1), _stacked((1, d), l, 1)],
        out_specs=tok(d),
        scratch_shapes=scratch,
        compiler_params=pltpu.CompilerParams(vmem_limit_bytes=VMEM_LIMIT),
        name="outproj",
    )(x, mod_all, *[o for o, _ in att], *[st for _, st in att], y_ssm, y_pool,
      glu_w, glu_b, w_out, w_out, w_out, ln_g, ln_b)


def _block_diag(w):
    g, n, _ = w.shape
    return jnp.einsum('gab,gh->gahb', w, jnp.eye(g, dtype=w.dtype)).reshape(g * n, g * n)


def kernel(x, c, rel_bias, ada_w, ada_b, ln_g, ln_b, ffn_w_gate, ffn_w_up, ffn_w_down, w_in, w_out,
           ssm_a_re, ssm_a_im, ssm_log_dt, ssm_b_re, ssm_b_im, ssm_c_re, ssm_c_im, ssm_d, glu_w, glu_b,
           pool_w, pool_scale):
    bsz = x.shape[0]
    mod_all = _adaln(c, ada_w, ada_b).reshape(DEPTH, bsz, 9, D_MODEL)
    wg, wu, wd = ffn_w_gate, ffn_w_up, ffn_w_down
    w_in_b, w_out_b, glu_w_b = w_in.astype(BF16), w_out.astype(BF16), glu_w.astype(BF16)
    ln_g4, ln_b4 = ln_g.reshape(DEPTH, 3, 1, D_MODEL), ln_b.reshape(DEPTH, 3, 1, D_MODEL)
    glu_b3 = glu_b.reshape(DEPTH, 1, D_SSM)
    pool_scale3 = pool_scale.reshape(DEPTH, 1, D_POOL)
    pool_w_bd = jax.vmap(_block_diag)(pool_w).astype(BF16)
    biases = [_branch_bias(rel_bias, window, dilation, x.shape[1] // dilation > ATT_BLOCK)
              for window, dilation in DILATED_PATTERNS]
    tables = _ssm_tables(ssm_a_re, ssm_a_im, ssm_log_dt, ssm_b_re, ssm_b_im, ssm_c_re, ssm_c_im, ssm_d)
    for l in range(DEPTH):
        x = _ffn(x, mod_all, wg, wu, wd, ln_g4, ln_b4, l, 0)
        qkv, u_ssm, y_pool = _inproj(x, mod_all, w_in_b, pool_w_bd, pool_scale3, l)
        att = _dilated_attention(qkv, biases)
        y_ssm = _ssm(u_ssm, tables, l)
        x = _outproj(x, mod_all, att, y_ssm, y_pool, glu_w_b, glu_b3, w_out_b, ln_g4, ln_b4, l)
        x = _ffn(x, mod_all, wg, wu, wd, ln_g4, ln_b4, l, 1)
    return x
```

```python
import functools
import math

import jax
import jax.numpy as jnp
import numpy as np
from jax import lax
from jax.experimental import pallas as pl
from jax.experimental.pallas import tpu as pltpu

F32 = jnp.float32
BF16 = jnp.bfloat16

D_MODEL = 1024
DEPTH = 2
HEAD_DIM = 64
N_HEADS = 8
D_ATT = N_HEADS * HEAD_DIM
DILATED_PATTERNS = ((128, 1), (512, 4), (2048, 16))
DILATIONS = tuple(d for _, d in DILATED_PATTERNS)
ATT_BLOCK = 128
SSM_GROUP = 16
D_SSM = 256
N_SSM_GROUPS = D_SSM // SSM_GROUP
SSM_STATE = 64
POOL_WINDOWS = (2, 4, 8, 16)
D_POOL = 256
POOL_GROUP = D_POOL // len(POOL_WINDOWS)
D_IN = 3 * D_ATT + D_SSM + D_POOL
D_FF = 2816
N_BUCKETS = 32
MAX_DISTANCE = 2048
ALPHA = (2 * DEPTH) ** 0.25
FFN_RES = 0.5
LN_EPS = 1e-5
NEG = -1e30

LANES = 128
TOKEN_TILE = 512
OUTPROJ_TOKEN_TILE = 1024
FF_CHUNK = 256
FFN_TOKEN_TILE = 512
ADA_COL_TILE = 2304
SSM_CHUNK = 8
SSM_EXPAND_ROWS = 256
POOL_HALO = 2 * max(POOL_WINDOWS)
ATT_LOOKAHEAD = 8
ATT_STEP_BLOCKS = 16
MERGE_ROWS = 128
LOG2E = math.log2(math.e)
STAT_SUM_OFFSET = 16
VMEM_LIMIT = 56 * 1024 * 1024


def _sigmoid(x):
    return 1.0 / (1.0 + jnp.exp(-x))


def _layernorm(x):
    mu = jnp.mean(x, axis=-1, keepdims=True)
    xc = x - mu
    var = jnp.mean(xc * xc, axis=-1, keepdims=True)
    return xc * lax.rsqrt(var + LN_EPS)


def _resident(shape):
    zeros = (0,) * len(shape)
    return pl.BlockSpec(shape, lambda *_: zeros, pipeline_mode=pl.Buffered(1))


def _stacked(tail, *lead):
    idx = tuple(lead) + (0,) * len(tail)
    return pl.BlockSpec((None,) * len(lead) + tuple(tail), lambda *_: idx, pipeline_mode=pl.Buffered(1))


def _adaln_kernel(c_ref, w_ref, b_ref, o_ref):
    c = c_ref[...]
    cond = (c * _sigmoid(c)).astype(BF16)
    o_ref[...] = jnp.dot(cond, w_ref[...].astype(BF16), preferred_element_type=F32) + b_ref[...]


def _adaln(c, ada_w, ada_b):
    nl, d, n = ada_w.shape
    bsz = c.shape[0]
    return pl.pallas_call(
        _adaln_kernel,
        out_shape=jax.ShapeDtypeStruct((nl, bsz, n), F32),
        grid=(nl, n // ADA_COL_TILE),
        in_specs=[
            pl.BlockSpec((bsz, d), lambda l, j: (0, 0)),
            pl.BlockSpec((None, d, ADA_COL_TILE), lambda l, j: (l, 0, j)),
            pl.BlockSpec((None, 1, ADA_COL_TILE), lambda l, j: (l, 0, j)),
        ],
        out_specs=pl.BlockSpec((None, bsz, ADA_COL_TILE), lambda l, j: (l, 0, j)),
        compiler_params=pltpu.CompilerParams(vmem_limit_bytes=VMEM_LIMIT),
        name="adaln",
    )(c, ada_w, ada_b.reshape(nl, 1, n))


def _ff_chunks():
    chunks, c0 = [], 0
    while c0 < D_FF:
        cw = min(FF_CHUNK, D_FF - c0)
        chunks.append((c0, cw))
        c0 += cw
    return tuple(chunks)


def _ffn_kernel(x_ref, mod_ref, wg_ref, wu_ref, wd_ref, lg_ref, lb_ref, o_ref, acc_ref, *, sub):
    x = x_ref[...]
    shift = mod_ref[3 * sub + 0:3 * sub + 1, :]
    scale = mod_ref[3 * sub + 1:3 * sub + 2, :]
    gate = mod_ref[3 * sub + 2:3 * sub + 3, :]
    h = (_layernorm(x) * (1.0 + scale) + shift).astype(BF16)
    for idx, (c0, cw) in enumerate(_ff_chunks()):
        g = jnp.dot(h, wg_ref[:, c0:c0 + cw].astype(BF16), preferred_element_type=F32)
        u = jnp.dot(h, wu_ref[:, c0:c0 + cw].astype(BF16), preferred_element_type=F32)
        a = (g * _sigmoid(g) * u).astype(BF16)
        d = jnp.dot(a, wd_ref[c0:c0 + cw, :].astype(BF16), preferred_element_type=F32)
        if idx == 0:
            acc_ref[...] = d
        else:
            acc_ref[...] += d
    y = ALPHA * x + (FFN_RES * gate) * acc_ref[...]
    o_ref[...] = _layernorm(y) * lg_ref[...] + lb_ref[...]


def _mod_spec(l, d):
    return pl.BlockSpec((None, None, 9, d), lambda b, i: (l, b, 0, 0))


def _ffn(x, mod_all, wg, wu, wd, ln_g, ln_b, l, which):
    bsz, s, d = x.shape
    tm = FFN_TOKEN_TILE
    sub = 2 * which
    return pl.pallas_call(
        functools.partial(_ffn_kernel, sub=sub),
        out_shape=jax.ShapeDtypeStruct(x.shape, F32),
        grid=(bsz, s // tm),
        in_specs=[
            pl.BlockSpec((None, tm, d), lambda b, i: (b, i, 0)),
            _mod_spec(l, d),
            _stacked(wg.shape[2:], l, which),
            _stacked(wu.shape[2:], l, which),
            _stacked(wd.shape[2:], l, which),
            _stacked((1, d), l, sub),
            _stacked((1, d), l, sub),
        ],
        out_specs=pl.BlockSpec((None, tm, d), lambda b, i: (b, i, 0)),
        scratch_shapes=[pltpu.VMEM((tm, d), F32)],
        compiler_params=pltpu.CompilerParams(vmem_limit_bytes=VMEM_LIMIT),
        name=f"ffn{sub}",
    )(x, mod_all, wg, wu, wd, ln_g, ln_b)


def _inproj_kernel(x_ref, mod_ref, w_ref, pw_ref, ps_ref, *rest):
    qkv_refs = rest[:3 * len(DILATIONS)]
    us_ref, yp_ref, zs_ref, z4_ref = rest[3 * len(DILATIONS):3 * len(DILATIONS) + 4]
    e_refs = rest[3 * len(DILATIONS) + 4:]
    i = pl.program_id(1)
    tm = x_ref.shape[0]
    x = x_ref[...]
    shift = mod_ref[3:4, :]
    scale = mod_ref[4:5, :]
    h = (_layernorm(x) * (1.0 + scale) + shift).astype(BF16)
    z = jnp.dot(h, w_ref[...], preferred_element_type=F32)
    up = z[:, 3 * D_ATT + D_SSM:]

    slabs_per = D_ATT // LANES
    n_qkv = 3 * slabs_per
    n_slabs = (3 * D_ATT + D_SSM) // LANES
    for c in range(n_slabs):
        col = z[:, c * LANES:(c + 1) * LANES]
        zs_ref[c] = col * (HEAD_DIM ** -0.5 * LOG2E) if c < slabs_per else col
    assert DILATIONS == (1, 4, 16)
    q4rows, q16rows = tm // 4, tm // 16
    for c in range(n_qkv):
        which, lanes = c // slabs_per, slice((c % slabs_per) * LANES, (c % slabs_per + 1) * LANES)
        qkv_refs[which][0, :, lanes] = zs_ref[c].astype(BF16)
        for r4 in range(4):
            blk = zs_ref[c, pl.ds(r4, q4rows, stride=4), :]
            z4_ref[c, r4 * q4rows:(r4 + 1) * q4rows, :] = blk
            qkv_refs[3 + which][r4, :, lanes] = blk.astype(BF16)
        for r4 in range(4):
            for c4 in range(4):
                blk = z4_ref[c, pl.ds(r4 * q4rows + c4, q16rows, stride=4), :]
                qkv_refs[6 + which][r4 + 4 * c4, :, lanes] = blk.astype(BF16)
    for c in range(n_qkv, n_slabs):
        for j in range(SSM_CHUNK):
            lo = j * D_SSM + (c - n_qkv) * LANES
            us_ref[:, lo:lo + LANES] = zs_ref[c, pl.ds(j, tm // SSM_CHUNK, stride=SSM_CHUNK), :]

    e1, e2, e4, e8 = e_refs
    hl = POOL_HALO
    assert POOL_WINDOWS == (2, 4, 8, 16) and hl == 32

    @pl.when(i == 0)
    def _():
        e1[0:hl, :] = jnp.zeros((hl, D_POOL), F32)

    @pl.when(i > 0)
    def _():
        e1[0:hl, :] = e1[tm:tm + hl, :]

    e1[hl:hl + tm, :] = up
    e2[8:, :] = e1[8:, :] + e1[7:tm + hl - 1, :]
    e4[16:, :] = e2[16:, :] + e2[14:tm + hl - 2, :]
    e8[24:, :] = e4[24:, :] + e4[20:tm + hl - 4, :]
    sums = {2: e2[hl:, :], 4: e4[hl:, :], 8: e8[hl:, :], 16: e8[hl:, :] + e8[hl - 8:tm + hl - 8, :]}
    pos = (i * tm + lax.broadcasted_iota(jnp.int32, (tm, 1), 0) + 1).astype(F32)
    group = lax.broadcasted_iota(jnp.int32, (1, D_POOL), 1) // POOL_GROUP
    mean = sums[POOL_WINDOWS[-1]] / jnp.minimum(pos, float(POOL_WINDOWS[-1]))
    for gi in range(len(POOL_WINDOWS) - 2, -1, -1):
        w = POOL_WINDOWS[gi]
        mean = jnp.where(group == gi, sums[w] / jnp.minimum(pos, float(w)), mean)
    pooled = (mean - up).astype(BF16)
    yp = jnp.dot(pooled, pw_ref[...], preferred_element_type=F32) * ps_ref[...]
    yp_ref[...] = yp.astype(BF16)


def _inproj(x, mod_all, w_in, pool_w_bd, pool_scale, l):
    bsz, s, d = x.shape
    tm = TOKEN_TILE
    tok = lambda width: pl.BlockSpec((None, tm, width), lambda b, i: (b, i, 0))
    qkv_shapes, qkv_specs = [], []
    for dil in DILATIONS:
        for _ in range(3):
            qkv_shapes.append(jax.ShapeDtypeStruct((bsz, dil, s // dil, D_ATT), BF16))
            qkv_specs.append(pl.BlockSpec((None, dil, tm // dil, D_ATT), lambda b, i: (b, 0, i, 0)))
    res = pl.pallas_call(
        _inproj_kernel,
        out_shape=tuple(qkv_shapes) + (
            jax.ShapeDtypeStruct((bsz, s // SSM_CHUNK, SSM_CHUNK * D_SSM), F32),
            jax.ShapeDtypeStruct((bsz, s, D_POOL), BF16),
        ),
        grid=(bsz, s // tm),
        in_specs=[
            tok(d),
            _mod_spec(l, d),
            _stacked(w_in.shape[1:], l),
            _stacked(pool_w_bd.shape[1:], l),
            _stacked((1, D_POOL), l),
        ],
        out_specs=tuple(qkv_specs) + (
            pl.BlockSpec((None, tm // SSM_CHUNK, SSM_CHUNK * D_SSM), lambda b, i: (b, i, 0)), tok(D_POOL)),
        scratch_shapes=[pltpu.VMEM(((3 * D_ATT + D_SSM) // LANES, tm, LANES), F32),
                        pltpu.VMEM((3 * D_ATT // LANES, tm, LANES), F32)]
        + [pltpu.VMEM((tm + POOL_HALO, D_POOL), F32) for _ in range(4)],
        compiler_params=pltpu.CompilerParams(
            dimension_semantics=("arbitrary", "arbitrary"), vmem_limit_bytes=VMEM_LIMIT),
        name="inproj",
    )(x, mod_all, w_in, pool_w_bd, pool_scale)
    qkv = [tuple(res[3 * di:3 * di + 3]) for di in range(len(DILATIONS))]
    return qkv, res[-2], res[-1]


def _t5_bucket(dist):
    max_exact = N_BUCKETS // 2
    dd = np.maximum(dist, 1).astype(np.float32)
    large = max_exact + (np.log(dd / max_exact) / math.log(MAX_DISTANCE / max_exact)
                         * (N_BUCKETS - max_exact)).astype(np.int32)
    large = np.minimum(large, N_BUCKETS - 1)
    return np.where(dist < max_exact, dist, large).astype(np.int32)


def _branch_bias(rel_bias, window, dilation, has_prev):
    qb = ATT_BLOCK
    n_keys = window // dilation
    assert n_keys == qb
    period = 3 * qb
    dist = np.arange(n_keys, -1, -1)
    row = rel_bias[_t5_bucket(dist * dilation)].T.astype(F32) * LOG2E
    row = jnp.concatenate([row, jnp.full((N_HEADS, period - n_keys - 1), NEG, F32)], axis=1)
    flat = jnp.tile(row, (1, qb))[:, :qb * (period - 1)]
    bias = flat.reshape(N_HEADS, qb, period - 1)[:, :, :2 * qb]
    if not has_prev:
        return bias[:, :, qb:]
    first = jnp.concatenate([jnp.full((N_HEADS, qb, qb), NEG, F32), bias[:, :, qb:]], axis=2)
    return jnp.stack([bias, first], 0)


def _stat_lane(h):
    return h if h % 2 == 1 else HEAD_DIM + h


def _attn_kernel(*refs, has_prev, seq_blocks):
    qb = ATT_BLOCK
    if has_prev:
        q_ref, kc_ref, kh_ref, vc_ref, vh_ref, bias_ref, o_ref, st_ref, kbuf, vbuf = refs
        nsub = q_ref.shape[0] // qb
        kbuf[0:qb, :] = kh_ref[...]
        kbuf[qb:, :] = kc_ref[...]
        vbuf[0:qb, :] = vh_ref[...]
        vbuf[qb:, :] = vc_ref[...]
    else:
        q_ref, kc_ref, vc_ref, bias_ref, o_ref, st_ref = refs
        nsub = q_ref.shape[0]

    pair = 2 * HEAD_DIM
    assert pair == LANES
    lane = lax.broadcasted_iota(jnp.int32, (1, pair), 1)
    low = lane < HEAD_DIM
    stat_lane = lane

    def block(jj, carry):
        if has_prev:
            row0 = pl.multiple_of(jj * qb, qb)
            first = ((pl.program_id(2) * nsub + jj) % seq_blocks == 0).astype(jnp.int32)
            q_at = lambda cols: q_ref[pl.ds(row0, qb), cols]
            k_at = lambda cols: kbuf[pl.ds(row0, 2 * qb), cols]
            v_at = lambda cols: vbuf[pl.ds(row0, 2 * qb), cols]
            bias_at = lambda h: bias_ref[first, h]
        else:
            q_at = lambda cols: q_ref[jj, :, cols]
            k_at = lambda cols: kc_ref[jj, :, cols]
            v_at = lambda cols: vc_ref[jj, :, cols]
            bias_at = lambda h: bias_ref[h]

        def pair_cols(h):
            return slice((h // 2) * pair, (h // 2 + 1) * pair)

        def scores(h):
            q2 = q_at(pair_cols(h))
            qh = jnp.where(low, q2, jnp.zeros_like(q2)) if h % 2 == 0 else jnp.where(low, jnp.zeros_like(q2), q2)
            s = lax.dot_general(qh, k_at(pair_cols(h)), (((1,), (1,)), ((), ())), preferred_element_type=F32)
            return s + bias_at(h)

        pending = {h: scores(h) for h in range(min(ATT_LOOKAHEAD, N_HEADS))}
        outs = {}
        stats = jnp.ones((qb, LANES), F32)
        for h in range(N_HEADS):
            s = pending.pop(h)
            m = jnp.max(s, axis=-1, keepdims=True)
            p = jnp.exp2(s - m)
            if h + ATT_LOOKAHEAD < N_HEADS:
                pending[h + ATT_LOOKAHEAD] = scores(h + ATT_LOOKAHEAD)
            cols = pair_cols(h)
            v2 = v_at(cols)
            v2 = jnp.where(low, v2, jnp.ones_like(v2)) if h % 2 == 0 else jnp.where(low, jnp.ones_like(v2), v2)
            pv = jnp.dot(p.astype(BF16), v2, preferred_element_type=F32)
            outs[h] = pv
            stats = jnp.where(stat_lane == _stat_lane(h), m, stats)
            stats = jnp.where(stat_lane == _stat_lane(h) + STAT_SUM_OFFSET, pv, stats)
            if h % 2 == 1:
                o2 = jnp.where(low, outs.pop(h - 1), outs.pop(h)).astype(BF16)
                if has_prev:
                    o_ref[pl.ds(row0, qb), cols] = o2
                else:
                    o_ref[jj, :, cols] = o2
        if has_prev:
            st_ref[pl.ds(row0, qb), :] = stats
        else:
            st_ref[jj] = stats
        return carry

    lax.fori_loop(0, nsub, block, 0, unroll=True)


def _attn_branch(q, k, v, bias):
    bsz, d_res, ln_res, _ = q.shape
    qb = ATT_BLOCK
    has_prev = ln_res > qb
    seq_blocks = ln_res // qb
    if has_prev:
        q, k, v = (a.reshape(bsz, 1, d_res * ln_res, D_ATT) for a in (q, k, v))
    _, d, ln, _ = q.shape
    out_shape = (jax.ShapeDtypeStruct((bsz, d, ln, D_ATT), BF16),
                 jax.ShapeDtypeStruct((bsz, d, ln, LANES), F32))
    if has_prev:
        rows = min(ATT_STEP_BLOCKS * qb, ln)
        per = rows // qb
        cur = lambda width: pl.BlockSpec((None, None, rows, width), lambda b, r, j: (b, r, j, 0))
        halo = pl.BlockSpec((None, None, qb, D_ATT), lambda b, r, j: (b, r, jnp.maximum(j * per - 1, 0), 0))
        grid = (bsz, d, ln // rows)
        args = [q, k, k, v, v, bias]
        specs = [cur(D_ATT), cur(D_ATT), halo, cur(D_ATT), halo, _resident(bias.shape)]
        scratch = [pltpu.VMEM((rows + qb, D_ATT), BF16), pltpu.VMEM((rows + qb, D_ATT), BF16)]
    else:
        per = min(ATT_STEP_BLOCKS, d)
        cur = lambda width: pl.BlockSpec((None, per, qb, width), lambda b, r: (b, r, 0, 0))
        grid = (bsz, d // per)
        args = [q, k, v, bias]
        specs = [cur(D_ATT), cur(D_ATT), cur(D_ATT), _resident(bias.shape)]
        scratch = []
    acc, stats = pl.pallas_call(
        functools.partial(_attn_kernel, has_prev=has_prev, seq_blocks=seq_blocks),
        out_shape=out_shape,
        grid=grid,
        in_specs=specs,
        out_specs=(cur(D_ATT), cur(LANES)),
        scratch_shapes=scratch,
        compiler_params=pltpu.CompilerParams(vmem_limit_bytes=VMEM_LIMIT),
        name=f"attn_d{d_res}",
    )(*args)
    return acc.reshape(bsz, d_res, ln_res, D_ATT), stats.reshape(bsz, d_res, ln_res, LANES)


def _dilated_attention(qkv, biases):
    return [_attn_branch(q, k, v, bias) for (q, k, v), bias in zip(qkv, biases)]


def _ssm_tables(*params):
    t, g, p, c = SSM_CHUNK, N_SSM_GROUPS, SSM_STATE, SSM_GROUP
    n = t * g * c
    w_in, w_intra, w_out, abar_t, d_row = jax.vmap(_ssm_group_blocks)(*params)
    kk = np.arange(t * c)[:, None]
    cc = np.arange(n)[None, :]
    spread_rp = jnp.asarray((cc // (g * p) == kk // p) & (cc % p == kk % p), BF16)
    spread_jc = jnp.asarray((cc // (g * c) == kk // c) & (cc % c == kk % c), BF16)
    blocks = (w_in.astype(BF16), w_intra.astype(BF16), w_out.astype(BF16))
    return blocks, (spread_rp, spread_jc), abar_t, d_row


def _ssm_group_blocks(a_re, a_im, log_dt, b_re, b_im, c_re, c_im, d_skip):
    hi = lax.Precision.HIGHEST
    t, g, p, c = SSM_CHUNK, N_SSM_GROUPS, SSM_STATE, SSM_GROUP
    dt = jnp.exp(log_dt)[:, None]
    mag = jnp.exp(a_re * dt)
    ar, ai = mag * jnp.cos(a_im * dt), mag * jnp.sin(a_im * dt)
    den = a_re * a_re + a_im * a_im
    fr = ((ar - 1.0) * a_re + ai * a_im) / den
    fi = (ai * a_re - (ar - 1.0) * a_im) / den
    bbr = fr[:, :, None] * b_re - fi[:, :, None] * b_im
    bbi = fr[:, :, None] * b_im + fi[:, :, None] * b_re
    pr, pi_ = [jnp.ones_like(ar)], [jnp.zeros_like(ar)]
    for _ in range(t):
        pr.append(pr[-1] * ar - pi_[-1] * ai)
        pi_.append(pr[-2] * ai + pi_[-1] * ar)
    n = t * g * c
    assert n == 2 * g * p and t * c == 2 * p

    bbr_t, bbi_t = jnp.transpose(bbr, (0, 2, 1)), jnp.transpose(bbi, (0, 2, 1))
    win = []
    for j in range(t):
        qr, qi = pr[t - 1 - j][:, None, :], pi_[t - 1 - j][:, None, :]
        win.append(jnp.concatenate([qr * bbr_t - qi * bbi_t, qr * bbi_t + qi * bbr_t], axis=-1))
    w_in = jnp.stack(win, 0).reshape(n, 2 * p)

    c_re_t, c_im_t = jnp.transpose(c_re, (0, 2, 1)), jnp.transpose(c_im, (0, 2, 1))
    wr_cols, wi_cols, hs = [], [], []
    for j in range(t + 1):
        if j >= 1:
            wr_cols.append(c_re_t * pr[j][:, :, None] - c_im_t * pi_[j][:, :, None])
            wi_cols.append(c_re_t * pi_[j][:, :, None] + c_im_t * pr[j][:, :, None])
        if j < t:
            wr = c_re * pr[j][:, None, :] - c_im * pi_[j][:, None, :]
            wi = c_re * pi_[j][:, None, :] + c_im * pr[j][:, None, :]
            hs.append(jnp.einsum('gcp,gpd->gdc', wr, bbr, precision=hi)
                      - jnp.einsum('gcp,gpd->gdc', wi, bbi, precision=hi))
    w_out = jnp.stack([jnp.concatenate(wr_cols, axis=-1), -jnp.concatenate(wi_cols, axis=-1)],
                      0).reshape(n, t * c)

    hcat = jnp.concatenate(hs, axis=-1)
    rows = [hcat if jp == 0 else
            jnp.concatenate([jnp.zeros((g, c, jp * c), F32), hcat[:, :, :(t - jp) * c]], axis=-1)
            for jp in range(t)]
    w_intra = jnp.stack(rows, 0).reshape(n, t * c)

    abar_t = jnp.concatenate([pr[t].reshape(1, g * p), pi_[t].reshape(1, g * p)], axis=1)
    d_row = jnp.tile(d_skip.reshape(1, D_SSM), (1, t))
    return w_in, w_intra, w_out, abar_t, d_row


def _ssm_kernel(u_ref, win_b, wintra_b, wout_b, srp_ref, sjc_ref, abar_ref, d_ref, y_ref,
                z_ref, xp_ref, win_ref, wintra_ref, wout_ref):
    n = u_ref.shape[0]
    half = N_SSM_GROUPS * SSM_STATE
    c, p, g = SSM_GROUP, SSM_STATE, N_SSM_GROUPS

    @pl.when(pl.program_id(0) == 0)
    def _():
        size = win_ref.shape[0]
        rows_per = SSM_EXPAND_ROWS
        col = lax.broadcasted_iota(jnp.int32, (rows_per, size), 1)
        row = lax.broadcasted_iota(jnp.int32, (rows_per, size), 0)
        for blk, dst, row_div, col_div, spread in ((win_b, win_ref, c, p, srp_ref),
                                                   (wintra_b, wintra_ref, c, c, sjc_ref),
                                                   (wout_b, wout_ref, p, c, sjc_ref)):
            cg = (col // col_div) % g
            for r0 in range(0, size, rows_per):
                full = jnp.dot(blk[r0:r0 + rows_per, :], spread[...], preferred_element_type=F32)
                rg = ((row + r0) // row_div) % g
                dst[r0:r0 + rows_per, :] = jnp.where(rg == cg, full, 0.0).astype(BF16)

    u = u_ref[...]
    ub = u.astype(BF16)
    z_ref[...] = jnp.dot(ub, win_ref[...], preferred_element_type=F32)
    ar = abar_ref[:, 0:half]
    ai = abar_ref[:, half:]

    def step(k, carry):
        xr, xi = carry
        xp_ref[pl.ds(k, 1), 0:half] = xr
        xp_ref[pl.ds(k, 1), half:] = xi
        zr = z_ref[pl.ds(k, 1), 0:half]
        zi = z_ref[pl.ds(k, 1), half:]
        return ar * xr - ai * xi + zr, ar * xi + ai * xr + zi

    zero = jnp.zeros((1, half), F32)
    lax.fori_loop(0, n, step, (zero, zero))
    y = jnp.dot(ub, wintra_ref[...], preferred_element_type=F32)
    y = y + jnp.dot(xp_ref[...].astype(BF16), wout_ref[...], preferred_element_type=F32)
    y_ref[...] = y + d_ref[...] * u


def _ssm(u, tables, l):
    blocks, spreads, abar_t, d_row = tables
    bsz, n, width = u.shape
    size = blocks[0].shape[1]
    row = pl.BlockSpec((None, n, width), lambda b: (b, 0, 0))
    return pl.pallas_call(
        _ssm_kernel,
        out_shape=jax.ShapeDtypeStruct((bsz, n, width), F32),
        grid=(bsz,),
        in_specs=[row] + [_stacked(t.shape[1:], l) for t in blocks] + [_resident(sp.shape) for sp in spreads]
        + [_stacked(abar_t.shape[1:], l), _stacked(d_row.shape[1:], l)],
        out_specs=row,
        scratch_shapes=[pltpu.VMEM((n, size), F32), pltpu.VMEM((n, size), F32)]
        + [pltpu.VMEM((size, size), BF16)] * 3,
        compiler_params=pltpu.CompilerParams(dimension_semantics=("arbitrary",), vmem_limit_bytes=VMEM_LIMIT),
        name="ssm",
    )(u, *blocks, *spreads, abar_t, d_row)


def _outproj_kernel(x_ref, mod_ref, *rest):
    nd = len(DILATIONS)
    o_refs, l_refs = rest[0:nd], rest[nd:2 * nd]
    (ys_ref, yp_ref, gw_ref, gb_ref, wa_ref, ws_ref, wp_ref, lg_ref, lb_ref, o_ref) = rest[2 * nd:2 * nd + 10]
    on4, on16, ot4, ln4, ln16, lt4, ysn_ref, ya_ref = rest[2 * nd + 10:]
    tm = x_ref.shape[0]
    slabs = D_ATT // LANES
    assert DILATIONS == (1, 4, 16)
    r4rows, r16rows = tm // 4, tm // 16

    for j in range(SSM_CHUNK):
        for c in range(D_SSM // LANES):
            lo = j * D_SSM + c * LANES
            ysn_ref[c, pl.ds(j, tm // SSM_CHUNK, stride=SSM_CHUNK), :] = ys_ref[:, lo:lo + LANES]

    ys = jnp.concatenate([ysn_ref[c] for c in range(D_SSM // LANES)], axis=1)
    cdf = 0.5 * (1.0 + jnp.tanh(math.sqrt(2.0 / math.pi) * (ys + 0.044715 * (ys * ys * ys))))
    t = jnp.dot((ys * cdf).astype(BF16), gw_ref[...], preferred_element_type=F32) + gb_ref[...]
    yg = (ys * _sigmoid(t)).astype(BF16)
    y_rest = (jnp.dot(yg, ws_ref[...], preferred_element_type=F32)
              + jnp.dot(yp_ref[...], wp_ref[...], preferred_element_type=F32))

    for r4 in range(4):
        ln4[pl.ds(r4, r4rows, stride=4), :] = l_refs[1][r4]
        for c4 in range(4):
            lt4[pl.ds(r4 * r4rows + c4, r16rows, stride=4), :] = l_refs[2][r4 + 4 * c4]
        for c in range(slabs):
            lanes = slice(c * LANES, (c + 1) * LANES)
            on4[c, pl.ds(r4, r4rows, stride=4), :] = o_refs[1][r4, :, lanes].astype(F32)
            for c4 in range(4):
                ot4[c, pl.ds(r4 * r4rows + c4, r16rows, stride=4), :] = o_refs[2][r4 + 4 * c4, :, lanes].astype(F32)
    for r4 in range(4):
        ln16[pl.ds(r4, r4rows, stride=4), :] = lt4[r4 * r4rows:(r4 + 1) * r4rows, :]
        for c in range(slabs):
            on16[c, pl.ds(r4, r4rows, stride=4), :] = ot4[c, r4 * r4rows:(r4 + 1) * r4rows, :]

    head = lax.broadcasted_iota(jnp.int32, (LANES, D_ATT), 1) // HEAD_DIM
    stat_of_head = jnp.where(head % 2 == 1, head, HEAD_DIM + head)
    spread = (stat_of_head == lax.broadcasted_iota(jnp.int32, (LANES, D_ATT), 0)).astype(BF16)
    st_lane = lax.broadcasted_iota(jnp.int32, (1, LANES), 1)
    is_max_lane = functools.reduce(jnp.logical_or, [st_lane == _stat_lane(h) for h in range(N_HEADS)])
    for rc in range(tm // MERGE_ROWS):
        rs = slice(rc * MERGE_ROWS, (rc + 1) * MERGE_ROWS)
        st_nat = [l_refs[0][0, rs, :], ln4[rs, :], ln16[rs, :]]
        o_nat = [o_refs[0][0, rs, :].astype(F32),
                 jnp.concatenate([on4[c, rs, :] for c in range(slabs)], axis=1),
                 jnp.concatenate([on16[c, rs, :] for c in range(slabs)], axis=1)]
        mx = functools.reduce(jnp.maximum, st_nat)
        es = [jnp.exp2(st - mx) for st in st_nat]
        sums = [pltpu.roll(st, LANES - STAT_SUM_OFFSET, axis=1) for st in st_nat]
        inv = 1.0 / functools.reduce(lambda a, b: a + b, [e * l for e, l in zip(es, sums)])
        ya = None
        for e, ov in zip(es, o_nat):
            w = jnp.where(is_max_lane, e * inv, 0.0)
            wx = jnp.dot(w.astype(BF16), spread, preferred_element_type=F32)
            ya = wx * ov if ya is None else ya + wx * ov
        ya_ref[rs, :] = ya.astype(BF16)

    x = x_ref[...]
    gate = mod_ref[5:6, :]
    y = jnp.dot(ya_ref[...], wa_ref[...], preferred_element_type=F32) + y_rest
    r = ALPHA * x + gate * y
    o_ref[...] = _layernorm(r) * lg_ref[...] + lb_ref[...]


def _outproj(x, mod_all, att, y_ssm, y_pool, glu_w, glu_b, w_out, ln_g, ln_b, l):
    bsz, s, d = x.shape
    tm = OUTPROJ_TOKEN_TILE
    tok = lambda width: pl.BlockSpec((None, tm, width), lambda b, i: (b, i, 0))
    res = lambda dil, width: pl.BlockSpec((None, dil, tm // dil, width), lambda b, i: (b, 0, i, 0))
    rows_of = lambda r0, n: pl.BlockSpec((None, n, d), lambda *_: (l, r0 // n, 0), pipeline_mode=pl.Buffered(1))
    assert D_ATT % D_SSM == 0 and D_SSM == D_POOL
    slab = lambda n: pltpu.VMEM((n, tm, LANES), F32)
    scratch = [slab(D_ATT // LANES)] * 3 + [pltpu.VMEM((tm, LANES), F32)] * 3
    scratch += [slab(D_SSM // LANES), pltpu.VMEM((tm, D_ATT), BF16)]
    return pl.pallas_call(
        _outproj_kernel,
        out_shape=jax.ShapeDtypeStruct(x.shape, F32),
        grid=(bsz, s // tm),
        in_specs=[tok(d), _mod_spec(l, d)]
        + [res(dil, D_ATT) for dil in DILATIONS]
        + [res(dil, LANES) for dil in DILATIONS]
        + [pl.BlockSpec((None, tm // SSM_CHUNK, SSM_CHUNK * D_SSM), lambda b, i: (b, i, 0)), tok(D_POOL),
           _stacked(glu_w.shape[1:], l), _stacked((1, D_SSM), l),
           rows_of(0, D_ATT), rows_of(D_ATT, D_SSM), rows_of(D_ATT + D_SSM, D_POOL),
           _stacked((1, d), l, 1), _stacked((1, d), l, 1)],
        out_specs=tok(d),
        scratch_shapes=scratch,
        compiler_params=pltpu.CompilerParams(vmem_limit_bytes=VMEM_LIMIT),
        name="outproj",
    )(x, mod_all, *[o for o, _ in att], *[st for _, st in att], y_ssm, y_pool,
      glu_w, glu_b, w_out, w_out, w_out, ln_g, ln_b)


def _block_diag(w):
    g, n, _ = w.shape
    return jnp.einsum('gab,gh->gahb', w, jnp.eye(g, dtype=w.dtype)).reshape(g * n, g * n)


def kernel(x, c, rel_bias, ada_w, ada_b, ln_g, ln_b, ffn_w_gate, ffn_w_up, ffn_w_down, w_in, w_out,
           ssm_a_re, ssm_a_im, ssm_log_dt, ssm_b_re, ssm_b_im, ssm_c_re, ssm_c_im, ssm_d, glu_w, glu_b,
           pool_w, pool_scale):
    bsz = x.shape[0]
    mod_all = _adaln(c, ada_w, ada_b).reshape(DEPTH, bsz, 9, D_MODEL)
    wg, wu, wd = ffn_w_gate, ffn_w_up, ffn_w_down
    w_in_b, w_out_b, glu_w_b = w_in.astype(BF16), w_out.astype(BF16), glu_w.astype(BF16)
    ln_g4, ln_b4 = ln_g.reshape(DEPTH, 3, 1, D_MODEL), ln_b.reshape(DEPTH, 3, 1, D_MODEL)
    glu_b3 = glu_b.reshape(DEPTH, 1, D_SSM)
    pool_scale3 = pool_scale.reshape(DEPTH, 1, D_POOL)
    pool_w_bd = jax.vmap(_block_diag)(pool_w).astype(BF16)
    biases = [_branch_bias(rel_bias, window, dilation, x.shape[1] // dilation > ATT_BLOCK)
              for window, dilation in DILATED_PATTERNS]
    tables = _ssm_tables(ssm_a_re, ssm_a_im, ssm_log_dt, ssm_b_re, ssm_b_im, ssm_c_re, ssm_c_im, ssm_d)
    for l in range(DEPTH):
        x = _ffn(x, mod_all, wg, wu, wd, ln_g4, ln_b4, l, 0)
        qkv, u_ssm, y_pool = _inproj(x, mod_all, w_in_b, pool_w_bd, pool_scale3, l)
        att = _dilated_attention(qkv, biases)
        y_ssm = _ssm(u_ssm, tables, l)
        x = _outproj(x, mod_all, att, y_ssm, y_pool, glu_w_b, glu_b3, w_out_b, ln_g4, ln_b4, l)
        x = _ffn(x, mod_all, wg, wu, wd, ln_g4, ln_b4, l, 1)
    return x
```

```python
import functools
import math

import jax
import jax.numpy as jnp
import numpy as np
from jax import lax
from jax.experimental import pallas as pl
from jax.experimental.pallas import tpu as pltpu

F32 = jnp.float32
BF16 = jnp.bfloat16

D_MODEL = 1024
DEPTH = 2
HEAD_DIM = 64
N_HEADS = 8
D_ATT = N_HEADS * HEAD_DIM
DILATED_PATTERNS = ((128, 1), (512, 4), (2048, 16))
DILATIONS = tuple(d for _, d in DILATED_PATTERNS)
ATT_BLOCK = 128
SSM_GROUP = 16
D_SSM = 256
N_SSM_GROUPS = D_SSM // SSM_GROUP
SSM_STATE = 64
POOL_WINDOWS = (2, 4, 8, 16)
D_POOL = 256
POOL_GROUP = D_POOL // len(POOL_WINDOWS)
D_IN = 3 * D_ATT + D_SSM + D_POOL
D_FF = 2816
N_BUCKETS = 32
MAX_DISTANCE = 2048
ALPHA = (2 * DEPTH) ** 0.25
FFN_RES = 0.5
LN_EPS = 1e-5
NEG = -1e30

LANES = 128
TOKEN_TILE = 512
OUTPROJ_TOKEN_TILE = 1024
FF_CHUNK = 256
FFN_TOKEN_TILE = 1024
ADA_COL_TILE = 2304
SSM_CHUNK = 8
SSM_EXPAND_ROWS = 256
POOL_HALO = 2 * max(POOL_WINDOWS)
ATT_LOOKAHEAD = 8
ATT_STEP_BLOCKS = 16
MERGE_ROWS = 128
LOG2E = math.log2(math.e)
STAT_SUM_OFFSET = 16
VMEM_LIMIT = 56 * 1024 * 1024
FFN_VMEM_LIMIT = 60 * 1024 * 1024


def _sigmoid(x):
    return 1.0 / (1.0 + jnp.exp(-x))


def _layernorm(x):
    mu = jnp.mean(x, axis=-1, keepdims=True)
    xc = x - mu
    var = jnp.mean(xc * xc, axis=-1, keepdims=True)
    return xc * lax.rsqrt(var + LN_EPS)


def _resident(shape):
    zeros = (0,) * len(shape)
    return pl.BlockSpec(shape, lambda *_: zeros, pipeline_mode=pl.Buffered(1))


def _stacked(tail, *lead):
    idx = tuple(lead) + (0,) * len(tail)
    return pl.BlockSpec((None,) * len(lead) + tuple(tail), lambda *_: idx, pipeline_mode=pl.Buffered(1))


def _adaln_kernel(c_ref, w_ref, b_ref, o_ref):
    c = c_ref[...]
    cond = (c * _sigmoid(c)).astype(BF16)
    o_ref[...] = jnp.dot(cond, w_ref[...].astype(BF16), preferred_element_type=F32) + b_ref[...]


def _adaln(c, ada_w, ada_b):
    nl, d, n = ada_w.shape
    bsz = c.shape[0]
    return pl.pallas_call(
        _adaln_kernel,
        out_shape=jax.ShapeDtypeStruct((nl, bsz, n), F32),
        grid=(nl, n // ADA_COL_TILE),
        in_specs=[
            pl.BlockSpec((bsz, d), lambda l, j: (0, 0)),
            pl.BlockSpec((None, d, ADA_COL_TILE), lambda l, j: (l, 0, j)),
            pl.BlockSpec((None, 1, ADA_COL_TILE), lambda l, j: (l, 0, j)),
        ],
        out_specs=pl.BlockSpec((None, bsz, ADA_COL_TILE), lambda l, j: (l, 0, j)),
        compiler_params=pltpu.CompilerParams(vmem_limit_bytes=VMEM_LIMIT),
        name="adaln",
    )(c, ada_w, ada_b.reshape(nl, 1, n))


def _ff_chunks():
    chunks, c0 = [], 0
    while c0 < D_FF:
        cw = min(FF_CHUNK, D_FF - c0)
        chunks.append((c0, cw))
        c0 += cw
    return tuple(chunks)


def _ffn_kernel(x_ref, mod_ref, wg_ref, wu_ref, wd_ref, lg_ref, lb_ref, o_ref, acc_ref, *, sub):
    x = x_ref[...]
    shift = mod_ref[3 * sub + 0:3 * sub + 1, :]
    scale = mod_ref[3 * sub + 1:3 * sub + 2, :]
    gate = mod_ref[3 * sub + 2:3 * sub + 3, :]
    h = (_layernorm(x) * (1.0 + scale) + shift).astype(BF16)
    for idx, (c0, cw) in enumerate(_ff_chunks()):
        g = jnp.dot(h, wg_ref[:, c0:c0 + cw].astype(BF16), preferred_element_type=F32)
        u = jnp.dot(h, wu_ref[:, c0:c0 + cw].astype(BF16), preferred_element_type=F32)
        a = (g * _sigmoid(g) * u).astype(BF16)
        d = jnp.dot(a, wd_ref[c0:c0 + cw, :].astype(BF16), preferred_element_type=F32)
        if idx == 0:
            acc_ref[...] = d
        else:
            acc_ref[...] += d
    y = ALPHA * x + (FFN_RES * gate) * acc_ref[...]
    o_ref[...] = _layernorm(y) * lg_ref[...] + lb_ref[...]


def _mod_spec(l, d):
    return pl.BlockSpec((None, None, 9, d), lambda b, i: (l, b, 0, 0))


def _ffn(x, mod_all, wg, wu, wd, ln_g, ln_b, l, which):
    bsz, s, d = x.shape
    tm = FFN_TOKEN_TILE
    sub = 2 * which
    return pl.pallas_call(
        functools.partial(_ffn_kernel, sub=sub),
        out_shape=jax.ShapeDtypeStruct(x.shape, F32),
        grid=(bsz, s // tm),
        in_specs=[
            pl.BlockSpec((None, tm, d), lambda b, i: (b, i, 0)),
            _mod_spec(l, d),
            _stacked(wg.shape[2:], l, which),
            _stacked(wu.shape[2:], l, which),
            _stacked(wd.shape[2:], l, which),
            _stacked((1, d), l, sub),
            _stacked((1, d), l, sub),
        ],
        out_specs=pl.BlockSpec((None, tm, d), lambda b, i: (b, i, 0)),
        scratch_shapes=[pltpu.VMEM((tm, d), F32)],
        compiler_params=pltpu.CompilerParams(vmem_limit_bytes=FFN_VMEM_LIMIT),
        name=f"ffn{sub}",
    )(x, mod_all, wg, wu, wd, ln_g, ln_b)


def _inproj_kernel(x_ref, mod_ref, w_ref, pw_ref, ps_ref, *rest):
    qkv_refs = rest[:3 * len(DILATIONS)]
    us_ref, yp_ref, zs_ref, z4_ref = rest[3 * len(DILATIONS):3 * len(DILATIONS) + 4]
    e_refs = rest[3 * len(DILATIONS) + 4:]
    i = pl.program_id(1)
    tm = x_ref.shape[0]
    x = x_ref[...]
    shift = mod_ref[3:4, :]
    scale = mod_ref[4:5, :]
    h = (_layernorm(x) * (1.0 + scale) + shift).astype(BF16)
    z = jnp.dot(h, w_ref[...], preferred_element_type=F32)
    up = z[:, 3 * D_ATT + D_SSM:]

    slabs_per = D_ATT // LANES
    n_qkv = 3 * slabs_per
    n_slabs = (3 * D_ATT + D_SSM) // LANES
    for c in range(n_slabs):
        col = z[:, c * LANES:(c + 1) * LANES]
        zs_ref[c] = col * (HEAD_DIM ** -0.5 * LOG2E) if c < slabs_per else col
    assert DILATIONS == (1, 4, 16)
    q4rows, q16rows = tm // 4, tm // 16
    for c in range(n_qkv):
        which, lanes = c // slabs_per, slice((c % slabs_per) * LANES, (c % slabs_per + 1) * LANES)
        qkv_refs[which][0, :, lanes] = zs_ref[c].astype(BF16)
        for r4 in range(4):
            blk = zs_ref[c, pl.ds(r4, q4rows, stride=4), :]
            z4_ref[c, r4 * q4rows:(r4 + 1) * q4rows, :] = blk
            qkv_refs[3 + which][r4, :, lanes] = blk.astype(BF16)
        for r4 in range(4):
            for c4 in range(4):
                blk = z4_ref[c, pl.ds(r4 * q4rows + c4, q16rows, stride=4), :]
                qkv_refs[6 + which][r4 + 4 * c4, :, lanes] = blk.astype(BF16)
    for c in range(n_qkv, n_slabs):
        for j in range(SSM_CHUNK):
            lo = j * D_SSM + (c - n_qkv) * LANES
            us_ref[:, lo:lo + LANES] = zs_ref[c, pl.ds(j, tm // SSM_CHUNK, stride=SSM_CHUNK), :]

    e1, e2, e4, e8 = e_refs
    hl = POOL_HALO
    assert POOL_WINDOWS == (2, 4, 8, 16) and hl == 32

    @pl.when(i == 0)
    def _():
        e1[0:hl, :] = jnp.zeros((hl, D_POOL), F32)

    @pl.when(i > 0)
    def _():
        e1[0:hl, :] = e1[tm:tm + hl, :]

    e1[hl:hl + tm, :] = up
    e2[8:, :] = e1[8:, :] + e1[7:tm + hl - 1, :]
    e4[16:, :] = e2[16:, :] + e2[14:tm + hl - 2, :]
    e8[24:, :] = e4[24:, :] + e4[20:tm + hl - 4, :]
    sums = {2: e2[hl:, :], 4: e4[hl:, :], 8: e8[hl:, :], 16: e8[hl:, :] + e8[hl - 8:tm + hl - 8, :]}
    pos = (i * tm + lax.broadcasted_iota(jnp.int32, (tm, 1), 0) + 1).astype(F32)
    group = lax.broadcasted_iota(jnp.int32, (1, D_POOL), 1) // POOL_GROUP
    mean = sums[POOL_WINDOWS[-1]] / jnp.minimum(pos, float(POOL_WINDOWS[-1]))
    for gi in range(len(POOL_WINDOWS) - 2, -1, -1):
        w = POOL_WINDOWS[gi]
        mean = jnp.where(group == gi, sums[w] / jnp.minimum(pos, float(w)), mean)
    pooled = (mean - up).astype(BF16)
    yp = jnp.dot(pooled, pw_ref[...], preferred_element_type=F32) * ps_ref[...]
    yp_ref[...] = yp.astype(BF16)


def _inproj(x, mod_all, w_in, pool_w_bd, pool_scale, l):
    bsz, s, d = x.shape
    tm = TOKEN_TILE
    tok = lambda width: pl.BlockSpec((None, tm, width), lambda b, i: (b, i, 0))
    qkv_shapes, qkv_specs = [], []
    for dil in DILATIONS:
        for _ in range(3):
            qkv_shapes.append(jax.ShapeDtypeStruct((bsz, dil, s // dil, D_ATT), BF16))
            qkv_specs.append(pl.BlockSpec((None, dil, tm // dil, D_ATT), lambda b, i: (b, 0, i, 0)))
    res = pl.pallas_call(
        _inproj_kernel,
        out_shape=tuple(qkv_shapes) + (
            jax.ShapeDtypeStruct((bsz, s // SSM_CHUNK, SSM_CHUNK * D_SSM), F32),
            jax.ShapeDtypeStruct((bsz, s, D_POOL), BF16),
        ),
        grid=(bsz, s // tm),
        in_specs=[
            tok(d),
            _mod_spec(l, d),
            _stacked(w_in.shape[1:], l),
            _stacked(pool_w_bd.shape[1:], l),
            _stacked((1, D_POOL), l),
        ],
        out_specs=tuple(qkv_specs) + (
            pl.BlockSpec((None, tm // SSM_CHUNK, SSM_CHUNK * D_SSM), lambda b, i: (b, i, 0)), tok(D_POOL)),
        scratch_shapes=[pltpu.VMEM(((3 * D_ATT + D_SSM) // LANES, tm, LANES), F32),
                        pltpu.VMEM((3 * D_ATT // LANES, tm, LANES), F32)]
        + [pltpu.VMEM((tm + POOL_HALO, D_POOL), F32) for _ in range(4)],
        compiler_params=pltpu.CompilerParams(
            dimension_semantics=("arbitrary", "arbitrary"), vmem_limit_bytes=VMEM_LIMIT),
        name="inproj",
    )(x, mod_all, w_in, pool_w_bd, pool_scale)
    qkv = [tuple(res[3 * di:3 * di + 3]) for di in range(len(DILATIONS))]
    return qkv, res[-2], res[-1]


def _t5_bucket(dist):
    max_exact = N_BUCKETS // 2
    dd = np.maximum(dist, 1).astype(np.float32)
    large = max_exact + (np.log(dd / max_exact) / math.log(MAX_DISTANCE / max_exact)
                         * (N_BUCKETS - max_exact)).astype(np.int32)
    large = np.minimum(large, N_BUCKETS - 1)
    return np.where(dist < max_exact, dist, large).astype(np.int32)


def _branch_bias(rel_bias, window, dilation, has_prev):
    qb = ATT_BLOCK
    n_keys = window // dilation
    assert n_keys == qb
    period = 3 * qb
    dist = np.arange(n_keys, -1, -1)
    row = rel_bias[_t5_bucket(dist * dilation)].T.astype(F32) * LOG2E
    row = jnp.concatenate([row, jnp.full((N_HEADS, period - n_keys - 1), NEG, F32)], axis=1)
    flat = jnp.tile(row, (1, qb))[:, :qb * (period - 1)]
    bias = flat.reshape(N_HEADS, qb, period - 1)[:, :, :2 * qb]
    if not has_prev:
        return bias[:, :, qb:]
    first = jnp.concatenate([jnp.full((N_HEADS, qb, qb), NEG, F32), bias[:, :, qb:]], axis=2)
    return jnp.stack([bias, first], 0)


def _stat_lane(h):
    return h if h % 2 == 1 else HEAD_DIM + h


def _attn_kernel(*refs, has_prev, seq_blocks):
    qb = ATT_BLOCK
    if has_prev:
        q_ref, kc_ref, kh_ref, vc_ref, vh_ref, bias_ref, o_ref, st_ref, kbuf, vbuf = refs
        nsub = q_ref.shape[0] // qb
        kbuf[0:qb, :] = kh_ref[...]
        kbuf[qb:, :] = kc_ref[...]
        vbuf[0:qb, :] = vh_ref[...]
        vbuf[qb:, :] = vc_ref[...]
    else:
        q_ref, kc_ref, vc_ref, bias_ref, o_ref, st_ref = refs
        nsub = q_ref.shape[0]

    pair = 2 * HEAD_DIM
    assert pair == LANES
    lane = lax.broadcasted_iota(jnp.int32, (1, pair), 1)
    low = lane < HEAD_DIM
    stat_lane = lane

    def block(jj, carry):
        if has_prev:
            row0 = pl.multiple_of(jj * qb, qb)
            first = ((pl.program_id(2) * nsub + jj) % seq_blocks == 0).astype(jnp.int32)
            q_at = lambda cols: q_ref[pl.ds(row0, qb), cols]
            k_at = lambda cols: kbuf[pl.ds(row0, 2 * qb), cols]
            v_at = lambda cols: vbuf[pl.ds(row0, 2 * qb), cols]
            bias_at = lambda h: bias_ref[first, h]
        else:
            q_at = lambda cols: q_ref[jj, :, cols]
            k_at = lambda cols: kc_ref[jj, :, cols]
            v_at = lambda cols: vc_ref[jj, :, cols]
            bias_at = lambda h: bias_ref[h]

        def pair_cols(h):
            return slice((h // 2) * pair, (h // 2 + 1) * pair)

        def scores(h):
            q2 = q_at(pair_cols(h))
            qh = jnp.where(low, q2, jnp.zeros_like(q2)) if h % 2 == 0 else jnp.where(low, jnp.zeros_like(q2), q2)
            s = lax.dot_general(qh, k_at(pair_cols(h)), (((1,), (1,)), ((), ())), preferred_element_type=F32)
            return s + bias_at(h)

        pending = {h: scores(h) for h in range(min(ATT_LOOKAHEAD, N_HEADS))}
        outs = {}
        stats = jnp.ones((qb, LANES), F32)
        for h in range(N_HEADS):
            s = pending.pop(h)
            m = jnp.max(s, axis=-1, keepdims=True)
            p = jnp.exp2(s - m)
            if h + ATT_LOOKAHEAD < N_HEADS:
                pending[h + ATT_LOOKAHEAD] = scores(h + ATT_LOOKAHEAD)
            cols = pair_cols(h)
            v2 = v_at(cols)
            v2 = jnp.where(low, v2, jnp.ones_like(v2)) if h % 2 == 0 else jnp.where(low, jnp.ones_like(v2), v2)
            pv = jnp.dot(p.astype(BF16), v2, preferred_element_type=F32)
            outs[h] = pv
            stats = jnp.where(stat_lane == _stat_lane(h), m, stats)
            stats = jnp.where(stat_lane == _stat_lane(h) + STAT_SUM_OFFSET, pv, stats)
            if h % 2 == 1:
                o2 = jnp.where(low, outs.pop(h - 1), outs.pop(h)).astype(BF16)
                if has_prev:
                    o_ref[pl.ds(row0, qb), cols] = o2
                else:
                    o_ref[jj, :, cols] = o2
        if has_prev:
            st_ref[pl.ds(row0, qb), :] = stats
        else:
            st_ref[jj] = stats
        return carry

    lax.fori_loop(0, nsub, block, 0, unroll=True)


def _attn_branch(q, k, v, bias):
    bsz, d_res, ln_res, _ = q.shape
    qb = ATT_BLOCK
    has_prev = ln_res > qb
    seq_blocks = ln_res // qb
    if has_prev:
        q, k, v = (a.reshape(bsz, 1, d_res * ln_res, D_ATT) for a in (q, k, v))
    _, d, ln, _ = q.shape
    out_shape = (jax.ShapeDtypeStruct((bsz, d, ln, D_ATT), BF16),
                 jax.ShapeDtypeStruct((bsz, d, ln, LANES), F32))
    if has_prev:
        rows = min(ATT_STEP_BLOCKS * qb, ln)
        per = rows // qb
        cur = lambda width: pl.BlockSpec((None, None, rows, width), lambda b, r, j: (b, r, j, 0))
        halo = pl.BlockSpec((None, None, qb, D_ATT), lambda b, r, j: (b, r, jnp.maximum(j * per - 1, 0), 0))
        grid = (bsz, d, ln // rows)
        args = [q, k, k, v, v, bias]
        specs = [cur(D_ATT), cur(D_ATT), halo, cur(D_ATT), halo, _resident(bias.shape)]
        scratch = [pltpu.VMEM((rows + qb, D_ATT), BF16), pltpu.VMEM((rows + qb, D_ATT), BF16)]
    else:
        per = min(ATT_STEP_BLOCKS, d)
        cur = lambda width: pl.BlockSpec((None, per, qb, width), lambda b, r: (b, r, 0, 0))
        grid = (bsz, d // per)
        args = [q, k, v, bias]
        specs = [cur(D_ATT), cur(D_ATT), cur(D_ATT), _resident(bias.shape)]
        scratch = []
    acc, stats = pl.pallas_call(
        functools.partial(_attn_kernel, has_prev=has_prev, seq_blocks=seq_blocks),
        out_shape=out_shape,
        grid=grid,
        in_specs=specs,
        out_specs=(cur(D_ATT), cur(LANES)),
        scratch_shapes=scratch,
        compiler_params=pltpu.CompilerParams(vmem_limit_bytes=VMEM_LIMIT),
        name=f"attn_d{d_res}",
    )(*args)
    return acc.reshape(bsz, d_res, ln_res, D_ATT), stats.reshape(bsz, d_res, ln_res, LANES)


def _dilated_attention(qkv, biases):
    return [_attn_branch(q, k, v, bias) for (q, k, v), bias in zip(qkv, biases)]


def _ssm_tables(*params):
    t, g, p, c = SSM_CHUNK, N_SSM_GROUPS, SSM_STATE, SSM_GROUP
    n = t * g * c
    w_in, w_intra, w_out, abar_t, d_row = jax.vmap(_ssm_group_blocks)(*params)
    kk = np.arange(t * c)[:, None]
    cc = np.arange(n)[None, :]
    spread_rp = jnp.asarray((cc // (g * p) == kk // p) & (cc % p == kk % p), BF16)
    spread_jc = jnp.asarray((cc // (g * c) == kk // c) & (cc % c == kk % c), BF16)
    blocks = (w_in.astype(BF16), w_intra.astype(BF16), w_out.astype(BF16))
    return blocks, (spread_rp, spread_jc), abar_t, d_row


def _ssm_group_blocks(a_re, a_im, log_dt, b_re, b_im, c_re, c_im, d_skip):
    hi = lax.Precision.HIGHEST
    t, g, p, c = SSM_CHUNK, N_SSM_GROUPS, SSM_STATE, SSM_GROUP
    dt = jnp.exp(log_dt)[:, None]
    mag = jnp.exp(a_re * dt)
    ar, ai = mag * jnp.cos(a_im * dt), mag * jnp.sin(a_im * dt)
    den = a_re * a_re + a_im * a_im
    fr = ((ar - 1.0) * a_re + ai * a_im) / den
    fi = (ai * a_re - (ar - 1.0) * a_im) / den
    bbr = fr[:, :, None] * b_re - fi[:, :, None] * b_im
    bbi = fr[:, :, None] * b_im + fi[:, :, None] * b_re
    pr, pi_ = [jnp.ones_like(ar)], [jnp.zeros_like(ar)]
    for _ in range(t):
        pr.append(pr[-1] * ar - pi_[-1] * ai)
        pi_.append(pr[-2] * ai + pi_[-1] * ar)
    n = t * g * c
    assert n == 2 * g * p and t * c == 2 * p

    bbr_t, bbi_t = jnp.transpose(bbr, (0, 2, 1)), jnp.transpose(bbi, (0, 2, 1))
    win = []
    for j in range(t):
        qr, qi = pr[t - 1 - j][:, None, :], pi_[t - 1 - j][:, None, :]
        win.append(jnp.concatenate([qr * bbr_t - qi * bbi_t, qr * bbi_t + qi * bbr_t], axis=-1))
    w_in = jnp.stack(win, 0).reshape(n, 2 * p)

    c_re_t, c_im_t = jnp.transpose(c_re, (0, 2, 1)), jnp.transpose(c_im, (0, 2, 1))
    wr_cols, wi_cols, hs = [], [], []
    for j in range(t + 1):
        if j >= 1:
            wr_cols.append(c_re_t * pr[j][:, :, None] - c_im_t * pi_[j][:, :, None])
            wi_cols.append(c_re_t * pi_[j][:, :, None] + c_im_t * pr[j][:, :, None])
        if j < t:
            wr = c_re * pr[j][:, None, :] - c_im * pi_[j][:, None, :]
            wi = c_re * pi_[j][:, None, :] + c_im * pr[j][:, None, :]
            hs.append(jnp.einsum('gcp,gpd->gdc', wr, bbr, precision=hi)
                      - jnp.einsum('gcp,gpd->gdc', wi, bbi, precision=hi))
    w_out = jnp.stack([jnp.concatenate(wr_cols, axis=-1), -jnp.concatenate(wi_cols, axis=-1)],
                      0).reshape(n, t * c)

    hcat = jnp.concatenate(hs, axis=-1)
    rows = [hcat if jp == 0 else
            jnp.concatenate([jnp.zeros((g, c, jp * c), F32), hcat[:, :, :(t - jp) * c]], axis=-1)
            for jp in range(t)]
    w_intra = jnp.stack(rows, 0).reshape(n, t * c)

    abar_t = jnp.concatenate([pr[t].reshape(1, g * p), pi_[t].reshape(1, g * p)], axis=1)
    d_row = jnp.tile(d_skip.reshape(1, D_SSM), (1, t))
    return w_in, w_intra, w_out, abar_t, d_row


def _ssm_kernel(u_ref, win_b, wintra_b, wout_b, srp_ref, sjc_ref, abar_ref, d_ref, y_ref,
                z_ref, xp_ref, win_ref, wintra_ref, wout_ref):
    n = u_ref.shape[0]
    half = N_SSM_GROUPS * SSM_STATE
    c, p, g = SSM_GROUP, SSM_STATE, N_SSM_GROUPS

    @pl.when(pl.program_id(0) == 0)
    def _():
        size = win_ref.shape[0]
        rows_per = SSM_EXPAND_ROWS
        col = lax.broadcasted_iota(jnp.int32, (rows_per, size), 1)
        row = lax.broadcasted_iota(jnp.int32, (rows_per, size), 0)
        for blk, dst, row_div, col_div, spread in ((win_b, win_ref, c, p, srp_ref),
                                                   (wintra_b, wintra_ref, c, c, sjc_ref),
                                                   (wout_b, wout_ref, p, c, sjc_ref)):
            cg = (col // col_div) % g
            for r0 in range(0, size, rows_per):
                full = jnp.dot(blk[r0:r0 + rows_per, :], spread[...], preferred_element_type=F32)
                rg = ((row + r0) // row_div) % g
                dst[r0:r0 + rows_per, :] = jnp.where(rg == cg, full, 0.0).astype(BF16)

    u = u_ref[...]
    ub = u.astype(BF16)
    z_ref[...] = jnp.dot(ub, win_ref[...], preferred_element_type=F32)
    ar = abar_ref[:, 0:half]
    ai = abar_ref[:, half:]

    def step(k, carry):
        xr, xi = carry
        xp_ref[pl.ds(k, 1), 0:half] = xr
        xp_ref[pl.ds(k, 1), half:] = xi
        zr = z_ref[pl.ds(k, 1), 0:half]
        zi = z_ref[pl.ds(k, 1), half:]
        return ar * xr - ai * xi + zr, ar * xi + ai * xr + zi

    zero = jnp.zeros((1, half), F32)
    lax.fori_loop(0, n, step, (zero, zero))
    y = jnp.dot(ub, wintra_ref[...], preferred_element_type=F32)
    y = y + jnp.dot(xp_ref[...].astype(BF16), wout_ref[...], preferred_element_type=F32)
    y_ref[...] = y + d_ref[...] * u


def _ssm(u, tables, l):
    blocks, spreads, abar_t, d_row = tables
    bsz, n, width = u.shape
    size = blocks[0].shape[1]
    row = pl.BlockSpec((None, n, width), lambda b: (b, 0, 0))
    return pl.pallas_call(
        _ssm_kernel,
        out_shape=jax.ShapeDtypeStruct((bsz, n, width), F32),
        grid=(bsz,),
        in_specs=[row] + [_stacked(t.shape[1:], l) for t in blocks] + [_resident(sp.shape) for sp in spreads]
        + [_stacked(abar_t.shape[1:], l), _stacked(d_row.shape[1:], l)],
        out_specs=row,
        scratch_shapes=[pltpu.VMEM((n, size), F32), pltpu.VMEM((n, size), F32)]
        + [pltpu.VMEM((size, size), BF16)] * 3,
        compiler_params=pltpu.CompilerParams(dimension_semantics=("arbitrary",), vmem_limit_bytes=VMEM_LIMIT),
        name="ssm",
    )(u, *blocks, *spreads, abar_t, d_row)


def _outproj_kernel(x_ref, mod_ref, *rest):
    nd = len(DILATIONS)
    o_refs, l_refs = rest[0:nd], rest[nd:2 * nd]
    (ys_ref, yp_ref, gw_ref, gb_ref, wa_ref, ws_ref, wp_ref, lg_ref, lb_ref, o_ref) = rest[2 * nd:2 * nd + 10]
    on4, on16, ot4, ln4, ln16, lt4, ysn_ref, ya_ref = rest[2 * nd + 10:]
    tm = x_ref.shape[0]
    slabs = D_ATT // LANES
    assert DILATIONS == (1, 4, 16)
    r4rows, r16rows = tm // 4, tm // 16

    for j in range(SSM_CHUNK):
        for c in range(D_SSM // LANES):
            lo = j * D_SSM + c * LANES
            ysn_ref[c, pl.ds(j, tm // SSM_CHUNK, stride=SSM_CHUNK), :] = ys_ref[:, lo:lo + LANES]

    ys = jnp.concatenate([ysn_ref[c] for c in range(D_SSM // LANES)], axis=1)
    cdf = 0.5 * (1.0 + jnp.tanh(math.sqrt(2.0 / math.pi) * (ys + 0.044715 * (ys * ys * ys))))
    t = jnp.dot((ys * cdf).astype(BF16), gw_ref[...], preferred_element_type=F32) + gb_ref[...]
    yg = (ys * _sigmoid(t)).astype(BF16)
    y_rest = (jnp.dot(yg, ws_ref[...], preferred_element_type=F32)
              + jnp.dot(yp_ref[...], wp_ref[...], preferred_element_type=F32))

    for r4 in range(4):
        ln4[pl.ds(r4, r4rows, stride=4), :] = l_refs[1][r4]
        for c4 in range(4):
            lt4[pl.ds(r4 * r4rows + c4, r16rows, stride=4), :] = l_refs[2][r4 + 4 * c4]
        for c in range(slabs):
            lanes = slice(c * LANES, (c + 1) * LANES)
            on4[c, pl.ds(r4, r4rows, stride=4), :] = o_refs[1][r4, :, lanes].astype(F32)
            for c4 in range(4):
                ot4[c, pl.ds(r4 * r4rows + c4, r16rows, stride=4), :] = o_refs[2][r4 + 4 * c4, :, lanes].astype(F32)
    for r4 in range(4):
        ln16[pl.ds(r4, r4rows, stride=4), :] = lt4[r4 * r4rows:(r4 + 1) * r4rows, :]
        for c in range(slabs):
            on16[c, pl.ds(r4, r4rows, stride=4), :] = ot4[c, r4 * r4rows:(r4 + 1) * r4rows, :]

    head = lax.broadcasted_iota(jnp.int32, (LANES, D_ATT), 1) // HEAD_DIM
    stat_of_head = jnp.where(head % 2 == 1, head, HEAD_DIM + head)
    spread = (stat_of_head == lax.broadcasted_iota(jnp.int32, (LANES, D_ATT), 0)).astype(BF16)
    st_lane = lax.broadcasted_iota(jnp.int32, (1, LANES), 1)
    is_max_lane = functools.reduce(jnp.logical_or, [st_lane == _stat_lane(h) for h in range(N_HEADS)])
    for rc in range(tm // MERGE_ROWS):
        rs = slice(rc * MERGE_ROWS, (rc + 1) * MERGE_ROWS)
        st_nat = [l_refs[0][0, rs, :], ln4[rs, :], ln16[rs, :]]
        o_nat = [o_refs[0][0, rs, :].astype(F32),
                 jnp.concatenate([on4[c, rs, :] for c in range(slabs)], axis=1),
                 jnp.concatenate([on16[c, rs, :] for c in range(slabs)], axis=1)]
        mx = functools.reduce(jnp.maximum, st_nat)
        es = [jnp.exp2(st - mx) for st in st_nat]
        sums = [pltpu.roll(st, LANES - STAT_SUM_OFFSET, axis=1) for st in st_nat]
        inv = 1.0 / functools.reduce(lambda a, b: a + b, [e * l for e, l in zip(es, sums)])
        ya = None
        for e, ov in zip(es, o_nat):
            w = jnp.where(is_max_lane, e * inv, 0.0)
            wx = jnp.dot(w.astype(BF16), spread, preferred_element_type=F32)
            ya = wx * ov if ya is None else ya + wx * ov
        ya_ref[rs, :] = ya.astype(BF16)

    x = x_ref[...]
    gate = mod_ref[5:6, :]
    y = jnp.dot(ya_ref[...], wa_ref[...], preferred_element_type=F32) + y_rest
    r = ALPHA * x + gate * y
    o_ref[...] = _layernorm(r) * lg_ref[...] + lb_ref[...]


def _outproj(x, mod_all, att, y_ssm, y_pool, glu_w, glu_b, w_out, ln_g, ln_b, l):
    bsz, s, d = x.shape
    tm = OUTPROJ_TOKEN_TILE
    tok = lambda width: pl.BlockSpec((None, tm, width), lambda b, i: (b, i, 0))
    res = lambda dil, width: pl.BlockSpec((None, dil, tm // dil, width), lambda b, i: (b, 0, i, 0))
    rows_of = lambda r0, n: pl.BlockSpec((None, n, d), lambda *_: (l, r0 // n, 0), pipeline_mode=pl.Buffered(1))
    assert D_ATT % D_SSM == 0 and D_SSM == D_POOL
    slab = lambda n: pltpu.VMEM((n, tm, LANES), F32)
    scratch = [slab(D_ATT // LANES)] * 3 + [pltpu.VMEM((tm, LANES), F32)] * 3
    scratch += [slab(D_SSM // LANES), pltpu.VMEM((tm, D_ATT), BF16)]
    return pl.pallas_call(
        _outproj_kernel,
        out_shape=jax.ShapeDtypeStruct(x.shape, F32),
        grid=(bsz, s // tm),
        in_specs=[tok(d), _mod_spec(l, d)]
        + [res(dil, D_ATT) for dil in DILATIONS]
        + [res(dil, LANES) for dil in DILATIONS]
        + [pl.BlockSpec((None, tm // SSM_CHUNK, SSM_CHUNK * D_SSM), lambda b, i: (b, i, 0)), tok(D_POOL),
           _stacked(glu_w.shape[1:], l), _stacked((1, D_SSM), l),
           rows_of(0, D_ATT), rows_of(D_ATT, D_SSM), rows_of(D_ATT + D_SSM, D_POOL),
           _stacked((1, d), l, 1), _stacked((1, d), l, 1)],
        out_specs=tok(d),
        scratch_shapes=scratch,
        compiler_params=pltpu.CompilerParams(vmem_limit_bytes=VMEM_LIMIT),
        name="outproj",
    )(x, mod_all, *[o for o, _ in att], *[st for _, st in att], y_ssm, y_pool,
      glu_w, glu_b, w_out, w_out, w_out, ln_g, ln_b)


def _block_diag(w):
    g, n, _ = w.shape
    return jnp.einsum('gab,gh->gahb', w, jnp.eye(g, dtype=w.dtype)).reshape(g * n, g * n)


def kernel(x, c, rel_bias, ada_w, ada_b, ln_g, ln_b, ffn_w_gate, ffn_w_up, ffn_w_down, w_in, w_out,
           ssm_a_re, ssm_a_im, ssm_log_dt, ssm_b_re, ssm_b_im, ssm_c_re, ssm_c_im, ssm_d, glu_w, glu_b,
           pool_w, pool_scale):
    bsz = x.shape[0]
    mod_all = _adaln(c, ada_w, ada_b).reshape(DEPTH, bsz, 9, D_MODEL)
    wg, wu, wd = ffn_w_gate, ffn_w_up, ffn_w_down
    w_in_b, w_out_b, glu_w_b = w_in.astype(BF16), w_out.astype(BF16), glu_w.astype(BF16)
    ln_g4, ln_b4 = ln_g.reshape(DEPTH, 3, 1, D_MODEL), ln_b.reshape(DEPTH, 3, 1, D_MODEL)
    glu_b3 = glu_b.reshape(DEPTH, 1, D_SSM)
    pool_scale3 = pool_scale.reshape(DEPTH, 1, D_POOL)
    pool_w_bd = jax.vmap(_block_diag)(pool_w).astype(BF16)
    biases = [_branch_bias(rel_bias, window, dilation, x.shape[1] // dilation > ATT_BLOCK)
              for window, dilation in DILATED_PATTERNS]
    tables = _ssm_tables(ssm_a_re, ssm_a_im, ssm_log_dt, ssm_b_re, ssm_b_im, ssm_c_re, ssm_c_im, ssm_d)
    for l in range(DEPTH):
        x = _ffn(x, mod_all, wg, wu, wd, ln_g4, ln_b4, l, 0)
        qkv, u_ssm, y_pool = _inproj(x, mod_all, w_in_b, pool_w_bd, pool_scale3, l)
        att = _dilated_attention(qkv, biases)
        y_ssm = _ssm(u_ssm, tables, l)
        x = _outproj(x, mod_all, att, y_ssm, y_pool, glu_w_b, glu_b3, w_out_b, ln_g4, ln_b4, l)
        x = _ffn(x, mod_all, wg, wu, wd, ln_g4, ln_b4, l, 1)
    return x
```
